```python
import math
import jax, jax.numpy as jnp
from jax import lax
import numpy as np

D_MODEL = 4096
BATCH = 4
SEQ = 2048
DEPTH = 2
DEC_BATCH = 8
DEC_SEQ = 4
PAST_LEN = 16384
PAGE_SIZE = 128

D_MIX = D_MODEL
D_ATT = D_MIX // 2
D_LRU = D_MIX - D_ATT
HEAD_DIM = 128
N_HEADS = D_ATT // HEAD_DIM
N_LRU_BLOCKS = 16
LRU_BLOCK = D_LRU // N_LRU_BLOCKS
CONV_W = 4
LRU_C = 8.0
DILATED_GROUPS = ((128, 1), (512, 4), (2048, 16))
WINDOW_MAX = 2048
N_BUCKETS = 32
MAX_EXACT = N_BUCKETS // 2
MAX_DISTANCE = WINDOW_MAX
EPS = 1e-6
ATT_SCALE = HEAD_DIM ** -0.5
D_IN = 4 * D_ATT + 2 * D_LRU

kernel_name = "hymba_rglru_dilated_swa_step"


def rmsnorm(x, g):
    xf = x.astype(jnp.float32)
    y = xf * lax.rsqrt(jnp.mean(xf * xf, axis=-1, keepdims=True) + EPS)
    return (y * g.astype(jnp.float32)).astype(x.dtype)


def rel_bucket(dist):
    d = dist.astype(jnp.float32)
    large = MAX_EXACT + jnp.log(jnp.maximum(d, 1.0) / MAX_EXACT) / math.log(MAX_DISTANCE / MAX_EXACT) * (N_BUCKETS - MAX_EXACT)
    large = jnp.minimum(large.astype(jnp.int32), N_BUCKETS - 1)
    return jnp.where(dist < MAX_EXACT, dist, large)


def modulate(x, c, g_pre, w_ada, b_ada):
    mod = jnp.einsum('bd,de->be', jax.nn.silu(c), w_ada) + b_ada
    shift, scale, gate = jnp.split(mod, 3, axis=-1)
    h = rmsnorm(x, g_pre) * (1 + scale[:, None]) + shift[:, None]
    return h, gate


def split_proj(h, w_in):
    p = jnp.einsum('btd,de->bte', h, w_in)
    return jnp.split(p, [D_ATT, 2 * D_ATT, 3 * D_ATT, 4 * D_ATT, 4 * D_ATT + D_LRU], axis=-1)


def to_heads(t):
    return t.reshape(t.shape[0], t.shape[1], N_HEADS, HEAD_DIM)


def dilated_prompt(q, k, v, rel_table, window, dil):
    B, S, H, Dh = q.shape
    L = window // dil
    m = -(-S // (dil * L))
    S_pad = m * dil * L

    def to_res(t):
        t = jnp.pad(t, ((0, 0), (0, S_pad - S), (0, 0), (0, 0))).reshape(B, m * L, dil, H, Dh)
        return jnp.moveaxis(t, 2, 1).reshape(B, dil, m, L, H, Dh)

    def with_prev(t):
        prev = jnp.pad(t, ((0, 0), (0, 0), (1, 0), (0, 0), (0, 0), (0, 0)))[:, :, :m]
        return jnp.concatenate([prev, t], axis=3)

    qr = to_res(q)
    kb = with_prev(to_res(k))
    vb = with_prev(to_res(v))
    qi = jnp.arange(L)[:, None]
    kj = jnp.arange(2 * L)[None, :]
    dist = qi + L - kj
    band = (dist >= 0) & (dist <= L)
    has_prev = jnp.arange(m)[:, None, None] > 0
    valid = band[None] & (has_prev | (kj[None] >= L))
    bias = rel_table[rel_bucket(jnp.clip(dist, 0, L) * dil)]
    bias = jnp.moveaxis(bias, 2, 0).astype(jnp.float32)
    logits = jnp.einsum('brmqhd,brmkhd->brmhqk', qr, kb, preferred_element_type=jnp.float32) * ATT_SCALE + bias
    logits = jnp.where(valid[:, None], logits, -jnp.inf)
    lse = jax.nn.logsumexp(logits, axis=-1)
    p = jnp.exp(logits - lse[..., None])
    o = jnp.einsum('brmhqk,brmkhd->brmqhd', p.astype(vb.dtype), vb, preferred_element_type=jnp.float32)

    def from_res(t):
        rest = t.shape[4:]
        t = t.reshape((B, dil, m * L) + rest)
        return jnp.moveaxis(t, 1, 2).reshape((B, S_pad) + rest)[:, :S]

    return from_res(o), from_res(jnp.moveaxis(lse, 4, 3))


def dilated_sample(q, k_all, v_all, rel_table, window, dil):
    T = q.shape[1]
    c_len = k_all.shape[1] - T
    L = window // dil
    offs = jnp.arange(L + 1) * dil
    idx = c_len + jnp.arange(T)[:, None] - offs[None, :]
    valid = idx >= 0
    idx = jnp.maximum(idx, 0)
    kg = k_all[:, idx]
    vg = v_all[:, idx]
    bias = rel_table[rel_bucket(offs)].T.astype(jnp.float32)
    logits = jnp.einsum('bthd,btkhd->bthk', q, kg, preferred_element_type=jnp.float32) * ATT_SCALE + bias
    logits = jnp.where(valid[:, None, :], logits, -jnp.inf)
    lse = jax.nn.logsumexp(logits, axis=-1)
    p = jnp.exp(logits - lse[..., None])
    o = jnp.einsum('bthk,btkhd->bthd', p.astype(vg.dtype), vg, preferred_element_type=jnp.float32)
    return o, lse


def mix_dilations(outs):
    o = jnp.stack([a for a, _ in outs])
    lse = jnp.stack([b for _, b in outs])
    w = jax.nn.softmax(lse, axis=0)
    return jnp.einsum('gbth,gbthd->bthd', w, o)


def lru_branch(x_ext, h0, w_conv, b_conv, w_a, b_a, w_x, b_x, lam):
    B = x_ext.shape[0]
    T = x_ext.shape[1] - (CONV_W - 1)
    xc = b_conv + x_ext[:, 0:T] * w_conv[0]
    for j in range(1, CONV_W):
        xc = xc + x_ext[:, j:j + T] * w_conv[j]
    xb = xc.reshape(B, T, N_LRU_BLOCKS, LRU_BLOCK)
    r = jax.nn.sigmoid((jnp.einsum('btnd,nde->btne', xb, w_a).reshape(B, T, D_LRU) + b_a).astype(jnp.float32))
    i = jax.nn.sigmoid((jnp.einsum('btnd,nde->btne', xb, w_x).reshape(B, T, D_LRU) + b_x).astype(jnp.float32))
    log_a = -LRU_C * r * jax.nn.softplus(-lam.astype(jnp.float32))
    a = jnp.exp(log_a)
    b = jnp.sqrt(-jnp.expm1(2.0 * log_a)) * (i * xc.astype(jnp.float32))

    def step(h, ab):
        a_t, b_t = ab
        h = a_t * h + b_t
        return h, h

    h_last, hs = lax.scan(step, h0.astype(jnp.float32), (jnp.swapaxes(a, 0, 1), jnp.swapaxes(b, 0, 1)))
    return jnp.swapaxes(hs, 0, 1).astype(x_ext.dtype), h_last


def finish(x, gate, o_att, g_att, y_lru, g_lru, w_out, g_post):
    B, T = x.shape[0], x.shape[1]
    u = jnp.concatenate([o_att.reshape(B, T, D_ATT).astype(x.dtype) * jax.nn.silu(g_att),
                         y_lru * jax.nn.silu(g_lru)], axis=-1)
    y = rmsnorm(jnp.einsum('btm,md->btd', u, w_out), g_post)
    return x + gate[:, None] * y


def setup_inputs(seed: int = 0) -> dict:
    key = jax.random.key(seed)
    ks = jax.random.split(key, 22)
    f32 = jnp.float32
    c_len = min(WINDOW_MAX, PAST_LEN)

    def nrm(k, shape, s=1.0):
        return s * jax.random.normal(k, shape, f32)

    u = jax.random.uniform(ks[19], (DEPTH, D_LRU), f32, 0.9, 0.999)
    a0 = u ** (1.0 / LRU_C)
    return {
        "x_prompt": nrm(ks[0], (BATCH, SEQ, D_MODEL)),
        "x_sample": nrm(ks[1], (DEC_BATCH, DEC_SEQ, D_MODEL)),
        "cache_k": nrm(ks[2], (DEPTH, DEC_BATCH, c_len, N_HEADS, HEAD_DIM)),
        "cache_v": nrm(ks[3], (DEPTH, DEC_BATCH, c_len, N_HEADS, HEAD_DIM)),
        "state_h": nrm(ks[4], (DEPTH, DEC_BATCH, D_LRU), 0.5),
        "state_conv": nrm(ks[5], (DEPTH, DEC_BATCH, CONV_W - 1, D_LRU)),
        "c_prompt": nrm(ks[6], (BATCH, D_MODEL)),
        "c_sample": nrm(ks[7], (DEC_BATCH, D_MODEL)),
        "rel_table": nrm(ks[8], (N_BUCKETS, N_HEADS), 0.5),
        "w_ada": nrm(ks[9], (DEPTH, D_MODEL, 3 * D_MODEL), D_MODEL ** -0.5),
        "b_ada": nrm(ks[10], (DEPTH, 3 * D_MODEL), 0.01),
        "g_pre": 1.0 + nrm(ks[11], (DEPTH, D_MODEL), 0.05),
        "w_in": nrm(ks[12], (DEPTH, D_MODEL, D_IN), D_MODEL ** -0.5),
        "w_conv": nrm(ks[13], (DEPTH, CONV_W, D_LRU), CONV_W ** -0.5),
        "b_conv": nrm(ks[14], (DEPTH, D_LRU), 0.01),
        "w_a": nrm(ks[15], (DEPTH, N_LRU_BLOCKS, LRU_BLOCK, LRU_BLOCK), LRU_BLOCK ** -0.5),
        "b_a": nrm(ks[16], (DEPTH, D_LRU), 0.01),
        "w_x": nrm(ks[17], (DEPTH, N_LRU_BLOCKS, LRU_BLOCK, LRU_BLOCK), LRU_BLOCK ** -0.5),
        "b_x": nrm(ks[18], (DEPTH, D_LRU), 0.01),
        "lam": jnp.log(a0) - jnp.log1p(-a0),
        "w_out": nrm(ks[20], (DEPTH, D_MIX, D_MODEL), D_MIX ** -0.5),
        "g_post": 1.0 + nrm(ks[21], (DEPTH, D_MODEL), 0.05),
    }


def reference(x_prompt, x_sample, cache_k, cache_v, state_h, state_conv, c_prompt, c_sample,
              rel_table, w_ada, b_ada, g_pre, w_in, w_conv, b_conv, w_a, b_a, w_x, b_x, lam, w_out, g_post):
    xp, xs = x_prompt, x_sample
    keep = min(WINDOW_MAX, xp.shape[1])
    kp_l, vp_l, hp_l, cp_l = [], [], [], []
    ks_l, vs_l, hs_l, cs_l = [], [], [], []
    for l in range(DEPTH):
        lru_w = (w_conv[l], b_conv[l], w_a[l], b_a[l], w_x[l], b_x[l], lam[l])
        hn, gate = modulate(xp, c_prompt, g_pre[l], w_ada[l], b_ada[l])
        q, k, v, g_att, x_lru, g_lru = split_proj(hn, w_in[l])
        qh, kh, vh = to_heads(q), to_heads(k), to_heads(v)
        o_att = mix_dilations([dilated_prompt(qh, kh, vh, rel_table, w, d) for (w, d) in DILATED_GROUPS])
        x_ext = jnp.pad(x_lru, ((0, 0), (CONV_W - 1, 0), (0, 0)))
        h0 = jnp.zeros((xp.shape[0], D_LRU), jnp.float32)
        y_lru, h_last = lru_branch(x_ext, h0, *lru_w)
        xp = finish(xp, gate, o_att, g_att, y_lru, g_lru, w_out[l], g_post[l])
        kp_l.append(kh[:, -keep:])
        vp_l.append(vh[:, -keep:])
        hp_l.append(h_last.astype(x_lru.dtype))
        cp_l.append(x_ext[:, -(CONV_W - 1):])
        hn, gate = modulate(xs, c_sample, g_pre[l], w_ada[l], b_ada[l])
        q, k, v, g_att, x_lru, g_lru = split_proj(hn, w_in[l])
        qh, kh, vh = to_heads(q), to_heads(k), to_heads(v)
        k_all = jnp.concatenate([cache_k[l].astype(kh.dtype), kh], axis=1)
        v_all = jnp.concatenate([cache_v[l].astype(vh.dtype), vh], axis=1)
        o_att = mix_dilations([dilated_sample(qh, k_all, v_all, rel_table, w, d) for (w, d) in DILATED_GROUPS])
        x_ext = jnp.concatenate([state_conv[l].astype(x_lru.dtype), x_lru], axis=1)
        y_lru, h_last = lru_branch(x_ext, state_h[l], *lru_w)
        xs = finish(xs, gate, o_att, g_att, y_lru, g_lru, w_out[l], g_post[l])
        ks_l.append(kh)
        vs_l.append(vh)
        hs_l.append(h_last.astype(x_lru.dtype))
        cs_l.append(x_ext[:, -(CONV_W - 1):])
    return (xp, xs, jnp.stack(kp_l), jnp.stack(vp_l), jnp.stack(hp_l), jnp.stack(cp_l),
            jnp.stack(ks_l), jnp.stack(vs_l), jnp.stack(hs_l), jnp.stack(cs_l))
```

```python
import functools
import math

import numpy as np
import jax
import jax.numpy as jnp
from jax import lax
from jax.experimental import pallas as pl
from jax.experimental.pallas import tpu as pltpu

D_MODEL = 4096
D_ATT = 2048
D_LRU = 2048
HEAD_DIM = 128
N_HEADS = 16
N_LRU_BLOCKS = 16
LRU_BLOCK = 128
CONV_W = 4
LRU_C = 8.0
DILATED_GROUPS = ((128, 1), (512, 4), (2048, 16))
N_GROUPS = len(DILATED_GROUPS)
WINDOW_MAX = 2048
N_BUCKETS = 32
MAX_EXACT = N_BUCKETS // 2
MAX_DISTANCE = WINDOW_MAX
EPS = 1e-6
ATT_SCALE = HEAD_DIM ** -0.5
D_IN = 4 * D_ATT + 2 * D_LRU
N_SEG = D_IN // D_ATT
L_BAND = 128
MASKED = -1e30
MOD_ROWS = 16
SAMPLE_KEYS = 2048 + 128
MIB = 1024 * 1024
NORM_CHUNK = 64
KV_HEADS = 8

F32 = jnp.float32
BF16 = jnp.bfloat16


def _params(semantics, vmem_mib):
    return pltpu.CompilerParams(dimension_semantics=semantics, vmem_limit_bytes=vmem_mib * MIB)


def _rel_bucket_np(dist):
    d = dist.astype(np.float32)
    large = np.float32(MAX_EXACT) + np.log(np.maximum(d, np.float32(1.0)) / np.float32(MAX_EXACT)) / np.float32(
        math.log(MAX_DISTANCE / MAX_EXACT)) * np.float32(N_BUCKETS - MAX_EXACT)
    large = np.minimum(large.astype(np.int32), N_BUCKETS - 1)
    return np.where(dist < MAX_EXACT, dist, large).astype(np.int32)


def _prompt_bucket_index():
    qi = np.arange(L_BAND)[:, None]
    kj = np.arange(2 * L_BAND)[None, :]
    dist = qi + L_BAND - kj
    band = (dist >= 0) & (dist <= L_BAND)
    out = []
    for _, dil in DILATED_GROUPS:
        b = _rel_bucket_np(np.clip(dist, 0, L_BAND) * dil)
        out.append(np.where(band, b, -1))
    return np.stack(out).astype(np.int32)


def _sample_bucket_index(c_len, t_new):
    t = np.arange(8)[:, None]
    idx = np.arange(SAMPLE_KEYS)[None, :]
    delta = c_len + t - idx
    out = []
    for window, dil in DILATED_GROUPS:
        valid = (delta >= 0) & (delta % dil == 0) & (delta <= window) & (idx < c_len + t_new) & (t < t_new)
        b = _rel_bucket_np(np.clip(delta, 0, window))
        b = np.where(valid, b, -1)
        b = np.where(t >= t_new, 0, b)
        out.append(b)
    return np.stack(out).astype(np.int32)


def _ada_kernel(c_ref, w_ref, b_ref, o_ref):
    c = c_ref[...]
    a = (c * jax.nn.sigmoid(c)).astype(BF16)
    o_ref[...] = jnp.dot(a, w_ref[...].astype(BF16), preferred_element_type=F32) + b_ref[...]


def _ada(c_all, w_ada, b_ada):
    depth = w_ada.shape[0]
    tn = 512
    return pl.pallas_call(
        _ada_kernel,
        grid=(depth, 3 * D_MODEL // tn),
        in_specs=[
            pl.BlockSpec((MOD_ROWS, D_MODEL), lambda l, j: (0, 0)),
            pl.BlockSpec((None, D_MODEL, tn), lambda l, j: (l, 0, j)),
            pl.BlockSpec((None, 1, tn), lambda l, j: (l, 0, j)),
        ],
        out_specs=pl.BlockSpec((None, MOD_ROWS, tn), lambda l, j: (l, 0, j)),
        out_shape=jax.ShapeDtypeStruct((depth, MOD_ROWS, 3 * D_MODEL), F32),
        compiler_params=_params(("arbitrary", "arbitrary"), 40),
        name="ada",
    )(c_all, w_ada, b_ada.reshape(depth, 1, 3 * D_MODEL))


def _bias_kernel(tab_ref, pidx_ref, sidx_ref, pb_ref, sb_ref):
    h = pl.program_id(0)
    for idx_ref, out_ref in ((pidx_ref, pb_ref), (sidx_ref, sb_ref)):
        for g in range(N_GROUPS):
            idx = idx_ref[g]
            acc = jnp.full(idx.shape, MASKED, F32)
            for b in range(N_BUCKETS):
                acc = jnp.where(idx == b, tab_ref[b, h], acc)
            out_ref[g] = acc


def _bias_tables(rel_table, c_len, t_new):
    pidx = jnp.asarray(_prompt_bucket_index())
    sidx = jnp.asarray(_sample_bucket_index(c_len, t_new))
    return pl.pallas_call(
        _bias_kernel,
        grid=(N_HEADS,),
        in_specs=[
            pl.BlockSpec(memory_space=pltpu.SMEM),
            pl.BlockSpec((N_GROUPS, L_BAND, 2 * L_BAND), lambda h: (0, 0, 0)),
            pl.BlockSpec((N_GROUPS, 8, SAMPLE_KEYS), lambda h: (0, 0, 0)),
        ],
        out_specs=[
            pl.BlockSpec((N_GROUPS, None, L_BAND, 2 * L_BAND), lambda h: (0, h, 0, 0)),
            pl.BlockSpec((N_GROUPS, None, 8, SAMPLE_KEYS), lambda h: (0, h, 0, 0)),
        ],
        out_shape=[
            jax.ShapeDtypeStruct((N_GROUPS, N_HEADS, L_BAND, 2 * L_BAND), F32),
            jax.ShapeDtypeStruct((N_GROUPS, N_HEADS, 8, SAMPLE_KEYS), F32),
        ],
        compiler_params=_params(("arbitrary",), 32),
        name="bias_tables",
    )(rel_table, pidx, sidx)


def _inproj_kernel(x_ref, shift_ref, scale_ref, g_ref, w_ref, *rest, nj_seg):
    a_ref, k_ref, v_ref, h_scr = rest[-4:]
    j = pl.program_id(1)

    tm = x_ref.shape[0]
    chunk = min(tm, NORM_CHUNK)

    @pl.when(j == 0)
    def _():
        per_row = shift_ref.shape[0] == tm

        def norm(c, _):
            rows = pl.ds(pl.multiple_of(c * chunk, chunk), chunk)
            mrows = rows if per_row else slice(None)
            x = x_ref[rows, :]
            y = x * lax.rsqrt(jnp.mean(x * x, axis=-1, keepdims=True) + EPS)
            y = y * g_ref[...]
            h_scr[rows, :] = (y * (1.0 + scale_ref[mrows, :]) + shift_ref[mrows, :]).astype(BF16)
            return 0

        lax.fori_loop(0, tm // chunk, norm, 0)

    acc = jnp.dot(h_scr[...], w_ref[...], preferred_element_type=F32)
    seg = j // nj_seg

    def store_heads(ref):
        heads = ref.shape[1]
        flat = ref.reshape(tm * heads, HEAD_DIM)
        for hh in range(heads):
            flat[pl.ds(hh, tm, stride=heads), :] = acc[:, hh * HEAD_DIM:(hh + 1) * HEAD_DIM]

    @pl.when(seg == 1)
    def _():
        store_heads(k_ref)

    @pl.when(seg == 2)
    def _():
        store_heads(v_ref)

    @pl.when((seg != 1) & (seg != 2))
    def _():
        a_ref[...] = acc


def _inproj(x2, shift, scale, mod_idx, g_pre, w_in, l, kv_prev, *, tm, tn, rows_per_mod, vmem_mib):
    m_rows = x2.shape[0]
    depth = w_in.shape[0]
    nj_seg = D_ATT // tn
    nj = N_SEG * nj_seg

    def a_idx(i, j):
        seg, half = j // nj_seg, j % nj_seg
        slot = jnp.where(seg <= 2, 0, seg - 2)
        col = jnp.where((seg == 1) | (seg == 2), nj_seg - 1, half)
        return slot, i, col

    def kv_idx(which):
        def idx(i, j):
            seg, half = j // nj_seg, j % nj_seg
            return l, i, jnp.where(seg < which, 0, jnp.where(seg == which, half, nj_seg - 1)), 0
        return idx

    heads = tn // HEAD_DIM
    assert heads % KV_HEADS == 0

    if mod_idx is None:
        mod_specs = [pl.BlockSpec((tm, D_MODEL), lambda i, j: (i, 0))] * 2
    else:
        mod_specs = [
            pl.BlockSpec((None, 1, D_MODEL), lambda i, j: (mod_idx(i // rows_per_mod, 0), 0, 0)),
            pl.BlockSpec((None, 1, D_MODEL), lambda i, j: (mod_idx(i // rows_per_mod, 1), 0, 0)),
        ]
    in_specs = [pl.BlockSpec((tm, D_MODEL), lambda i, j: (i, 0))] + mod_specs + [
        pl.BlockSpec((None, 1, D_MODEL), lambda i, j: (l, 0, 0)),
        pl.BlockSpec((None, D_MODEL, tn), lambda i, j: (l, 0, j)),
    ]
    args = [x2, shift, scale, g_pre, w_in]
    aliases = {}
    if kv_prev is not None:
        in_specs += [pl.BlockSpec(memory_space=pl.ANY)] * 2
        args += list(kv_prev)
        aliases = {5: 1, 6: 2}
    return pl.pallas_call(
        functools.partial(_inproj_kernel, nj_seg=nj_seg),
        grid=(m_rows // tm, nj),
        in_specs=in_specs,
        out_specs=[
            pl.BlockSpec((None, tm, tn), a_idx),
            pl.BlockSpec((None, tm, heads, HEAD_DIM), kv_idx(1)),
            pl.BlockSpec((None, tm, heads, HEAD_DIM), kv_idx(2)),
        ],
        out_shape=[
            jax.ShapeDtypeStruct((4, m_rows, D_ATT), F32),
            jax.ShapeDtypeStruct((depth, m_rows, N_HEADS, HEAD_DIM), F32),
            jax.ShapeDtypeStruct((depth, m_rows, N_HEADS, HEAD_DIM), F32),
        ],
        scratch_shapes=[pltpu.VMEM((tm, D_MODEL), BF16)],
        input_output_aliases=aliases,
        compiler_params=_params(("arbitrary", "arbitrary"), vmem_mib),
        name="inproj",
    )(*args)


def _rows(start, size, stride):
    return pl.ds(start, size) if stride == 1 else pl.ds(start, size, stride=stride)


def _attn_prompt_kernel(q_ref, kh_ref, vh_ref, g_ref, bias_ref, u_ref, k_ref, v_ref, o_scr, m_scr, l_scr, *, seq):
    hh = pl.program_id(1) % KV_HEADS
    k_ref[...] = kh_ref.reshape(seq * KV_HEADS, HEAD_DIM)[pl.ds(hh, seq, stride=KV_HEADS), :]
    v_ref[...] = vh_ref.reshape(seq * KV_HEADS, HEAD_DIM)[pl.ds(hh, seq, stride=KV_HEADS), :]

    def block(gi, dil, start, has_prev):
        cur = _rows(start, L_BAND, dil)
        qb = q_ref[cur, :].astype(BF16)
        kk = k_ref[cur, :].astype(BF16)
        vv = v_ref[cur, :].astype(BF16)
        if has_prev:
            prev = _rows(start - L_BAND * dil, L_BAND, dil)
            kk = jnp.concatenate([k_ref[prev, :].astype(BF16), kk], axis=0)
            vv = jnp.concatenate([v_ref[prev, :].astype(BF16), vv], axis=0)
            bias = bias_ref[gi]
        else:
            bias = bias_ref[gi, :, L_BAND:]
        s = lax.dot_general(qb, kk, (((1,), (1,)), ((), ())), preferred_element_type=F32) * ATT_SCALE + bias
        mx = jnp.max(s, axis=-1, keepdims=True)
        p = jnp.exp(s - mx)
        den = jnp.sum(p, axis=-1, keepdims=True)
        o = jnp.dot(p.astype(BF16), vv, preferred_element_type=F32)
        o_scr[gi, cur, :] = o
        m_scr[gi, cur, :] = jnp.broadcast_to(mx, (L_BAND, HEAD_DIM))
        l_scr[gi, cur, :] = jnp.broadcast_to(den, (L_BAND, HEAD_DIM))

    for gi, (window, dil) in enumerate(DILATED_GROUPS):
        n_blocks = seq // (dil * L_BAND)

        def residue(r, _, gi=gi, dil=dil, n_blocks=n_blocks):
            block(gi, dil, r, False)

            def later(mb, _):
                block(gi, dil, r + mb * (L_BAND * dil), True)
                return 0

            if n_blocks > 1:
                lax.fori_loop(1, n_blocks, later, 0)
            return 0

        if dil == 1:
            residue(0, 0)
        else:
            lax.fori_loop(0, dil, residue, 0)

    chunk = 256

    def combine(c, _):
        rows = pl.ds(pl.multiple_of(c * chunk, chunk), chunk)
        ms = [m_scr[gi, rows, :] for gi in range(N_GROUPS)]
        top = functools.reduce(jnp.maximum, ms)
        ws = [jnp.exp(m - top) for m in ms]
        num = sum(w * o_scr[gi, rows, :] for gi, w in enumerate(ws))
        den = sum(w * l_scr[gi, rows, :] for gi, w in enumerate(ws))
        g = g_ref[rows, :]
        u_ref[rows, :] = ((num / den) * (g * jax.nn.sigmoid(g))).astype(u_ref.dtype)
        return 0

    lax.fori_loop(0, seq // chunk, combine, 0)


def _attn_prompt(a4, kc, vc, pbias, l, batch, seq):
    blk = (None, None, seq, HEAD_DIM)
    kv_blk = (None, None, seq, KV_HEADS, HEAD_DIM)
    return pl.pallas_call(
        functools.partial(_attn_prompt_kernel, seq=seq),
        grid=(batch, N_HEADS),
        in_specs=[
            pl.BlockSpec(blk, lambda b, h: (0, b, 0, h)),
            pl.BlockSpec(kv_blk, lambda b, h: (l, b, 0, h // KV_HEADS, 0)),
            pl.BlockSpec(kv_blk, lambda b, h: (l, b, 0, h // KV_HEADS, 0)),
            pl.BlockSpec(blk, lambda b, h: (1, b, 0, h)),
            pl.BlockSpec((N_GROUPS, None, L_BAND, 2 * L_BAND), lambda b, h: (0, h, 0, 0)),
        ],
        out_specs=pl.BlockSpec((None, seq, HEAD_DIM), lambda b, h: (b, 0, h)),
        out_shape=jax.ShapeDtypeStruct((batch, seq, D_MODEL), BF16),
        scratch_shapes=[pltpu.VMEM((seq, HEAD_DIM), F32)] * 2 + [pltpu.VMEM((N_GROUPS, seq, HEAD_DIM), F32)] * 3,
        compiler_params=_params(("arbitrary", "arbitrary"), 56),
        name="attn_prompt",
    )(a4, kc, vc, a4, pbias)


def _attn_sample_kernel(q_ref, kn_ref, vn_ref, ck_ref, cv_ref, g_ref, bias_ref, u_ref, q_scr, k_scr, v_scr,
                        *, c_len, t_new):
    pad = SAMPLE_KEYS - c_len - t_new
    hh = pl.program_id(1) % KV_HEADS
    q_scr[0:t_new, :] = q_ref[...]
    q_scr[t_new:8, :] = jnp.zeros((8 - t_new, HEAD_DIM), F32)
    for new_ref, cache_ref, scr in ((kn_ref, ck_ref, k_scr), (vn_ref, cv_ref, v_scr)):
        scr[0:c_len, :] = cache_ref.reshape(c_len * KV_HEADS, HEAD_DIM)[pl.ds(hh, c_len, stride=KV_HEADS), :]
        scr[c_len:c_len + t_new, :] = new_ref.reshape(t_new * KV_HEADS, HEAD_DIM)[pl.ds(hh, t_new, stride=KV_HEADS), :]
        scr[c_len + t_new:SAMPLE_KEYS, :] = jnp.zeros((pad, HEAD_DIM), F32)
    kk = k_scr[...].astype(BF16)
    vv = v_scr[...].astype(BF16)
    s = lax.dot_general(q_scr[...].astype(BF16), kk, (((1,), (1,)), ((), ())),
                        preferred_element_type=F32) * ATT_SCALE
    ms, ls, os_ = [], [], []
    for gi in range(N_GROUPS):
        sg = s + bias_ref[gi]
        mx = jnp.max(sg, axis=-1, keepdims=True)
        p = jnp.exp(sg - mx)
        ms.append(mx)
        ls.append(jnp.sum(p, axis=-1, keepdims=True))
        os_.append(jnp.dot(p.astype(BF16), vv, preferred_element_type=F32))
    top = functools.reduce(jnp.maximum, ms)
    ws = [jnp.exp(m - top) for m in ms]
    num = sum(w * o for w, o in zip(ws, os_))
    den = sum(w * d for w, d in zip(ws, ls))
    g = g_ref[...]
    u_ref[...] = (num / den)[0:t_new, :] * (g * jax.nn.sigmoid(g))


def _attn_sample(a4, kn, vn, cache_k, cache_v, sbias, l, batch, t_new, c_len):
    new_blk = (None, None, t_new, HEAD_DIM)
    new_kv_blk = (None, None, t_new, KV_HEADS, HEAD_DIM)
    cache_blk = (None, None, c_len, KV_HEADS, HEAD_DIM)
    return pl.pallas_call(
        functools.partial(_attn_sample_kernel, c_len=c_len, t_new=t_new),
        grid=(batch, N_HEADS),
        in_specs=[
            pl.BlockSpec(new_blk, lambda b, h: (0, b, 0, h)),
            pl.BlockSpec(new_kv_blk, lambda b, h: (l, b, 0, h // KV_HEADS, 0)),
            pl.BlockSpec(new_kv_blk, lambda b, h: (l, b, 0, h // KV_HEADS, 0)),
            pl.BlockSpec(cache_blk, lambda b, h: (l, b, 0, h // KV_HEADS, 0)),
            pl.BlockSpec(cache_blk, lambda b, h: (l, b, 0, h // KV_HEADS, 0)),
            pl.BlockSpec(new_blk, lambda b, h: (1, b, 0, h)),
            pl.BlockSpec((N_GROUPS, None, 8, SAMPLE_KEYS), lambda b, h: (0, h, 0, 0)),
        ],
        out_specs=pl.BlockSpec((None, t_new, HEAD_DIM), lambda b, h: (b, 0, h)),
        out_shape=jax.ShapeDtypeStruct((batch, t_new, D_MODEL), F32),
        scratch_shapes=[pltpu.VMEM((8, HEAD_DIM), F32), pltpu.VMEM((SAMPLE_KEYS, HEAD_DIM), F32),
                        pltpu.VMEM((SAMPLE_KEYS, HEAD_DIM), F32)],
        compiler_params=_params(("arbitrary", "arbitrary"), 48),
        name="attn_sample",
    )(a4, kn, vn, cache_k, cache_v, a4, sbias)


def _lru_gates(xc, wa_ref, ba_ref, wx_ref, bx_ref, lam_ref):
    xcb = xc.astype(BF16)
    r = jax.nn.sigmoid(jnp.dot(xcb, wa_ref[...].astype(BF16), preferred_element_type=F32) + ba_ref[...])
    i = jax.nn.sigmoid(jnp.dot(xcb, wx_ref[...].astype(BF16), preferred_element_type=F32) + bx_ref[...])
    nl = -lam_ref[...]
    softplus = jnp.maximum(nl, 0.0) + jnp.log1p(jnp.exp(-jnp.abs(nl)))
    log_a = -LRU_C * r * softplus
    a = jnp.exp(log_a)
    t = jnp.tanh(log_a)
    b = jnp.sqrt(-2.0 * t / (1.0 - t)) * (i * xc)
    return a, b


def _lru_prompt_kernel(x_ref, g_ref, wc_ref, bc_ref, wa_ref, ba_ref, wx_ref, bx_ref, lam_ref, u_any,
                       u_ref, h_ref, cs_ref, xpad, a_scr, b_scr, *, seq):
    del u_any
    x = x_ref[...]
    xpad[0:8, :] = jnp.zeros((8, LRU_BLOCK), F32)
    xpad[8:8 + seq, :] = x
    xc = bc_ref[...] + xpad[pl.ds(8 - 3, seq), :] * wc_ref[0:1, :]
    xc = xc + xpad[pl.ds(8 - 2, seq), :] * wc_ref[1:2, :]
    xc = xc + xpad[pl.ds(8 - 1, seq), :] * wc_ref[2:3, :]
    xc = xc + x * wc_ref[3:4, :]
    a, b = _lru_gates(xc, wa_ref, ba_ref, wx_ref, bx_ref, lam_ref)
    a_scr[...] = a
    b_scr[...] = b

    row = lax.broadcasted_iota(jnp.int32, (8, LRU_BLOCK), 0)

    def body(c, h_prev):
        rows = pl.ds(pl.multiple_of(c * 8, 8), 8)
        ca = a_scr[rows, :]
        cb = b_scr[rows, :]
        for k in (1, 2, 4):
            a_sh = jnp.where(row >= k, pltpu.roll(ca, k, 0), 1.0)
            b_sh = jnp.where(row >= k, pltpu.roll(cb, k, 0), 0.0)
            cb = ca * b_sh + cb
            ca = ca * a_sh
        h = ca * h_prev + cb
        b_scr[rows, :] = h
        return jnp.broadcast_to(h[7:8, :], (8, LRU_BLOCK))

    lax.fori_loop(0, seq // 8, body, jnp.zeros((8, LRU_BLOCK), F32), unroll=8)
    y = b_scr[...]
    g = g_ref[...]
    u_ref[...] = (y * (g * jax.nn.sigmoid(g))).astype(u_ref.dtype)
    h_ref[...] = b_scr[seq - 1:seq, :]
    cs_ref[...] = x[seq - (CONV_W - 1):seq, :]


def _lru_weight_specs(l, n_of):
    return [
        pl.BlockSpec((None, CONV_W, LRU_BLOCK), lambda *g: (l, 0, n_of(*g))),
        pl.BlockSpec((None, 1, LRU_BLOCK), lambda *g: (l, 0, n_of(*g))),
        pl.BlockSpec((None, None, LRU_BLOCK, LRU_BLOCK), lambda *g: (l, n_of(*g), 0, 0)),
        pl.BlockSpec((None, 1, LRU_BLOCK), lambda *g: (l, 0, n_of(*g))),
        pl.BlockSpec((None, None, LRU_BLOCK, LRU_BLOCK), lambda *g: (l, n_of(*g), 0, 0)),
        pl.BlockSpec((None, 1, LRU_BLOCK), lambda *g: (l, 0, n_of(*g))),
        pl.BlockSpec((None, 1, LRU_BLOCK), lambda *g: (l, 0, n_of(*g))),
    ]


def _lru_prompt(a4, lru_w, u, l, batch, seq):
    blk = (None, None, seq, LRU_BLOCK)
    n_of = lambda b, n: n
    return pl.pallas_call(
        functools.partial(_lru_prompt_kernel, seq=seq),
        grid=(batch, N_LRU_BLOCKS),
        in_specs=[
            pl.BlockSpec(blk, lambda b, n: (2, b, 0, n)),
            pl.BlockSpec(blk, lambda b, n: (3, b, 0, n)),
        ] + _lru_weight_specs(l, n_of) + [pl.BlockSpec(memory_space=pl.ANY)],
        out_specs=[
            pl.BlockSpec((None, seq, LRU_BLOCK), lambda b, n: (b, 0, N_HEADS + n)),
            pl.BlockSpec((None, 1, LRU_BLOCK), lambda b, n: (b, 0, n)),
            pl.BlockSpec((None, CONV_W - 1, LRU_BLOCK), lambda b, n: (b, 0, n)),
        ],
        out_shape=[
            jax.ShapeDtypeStruct(u.shape, u.dtype),
            jax.ShapeDtypeStruct((batch, 1, D_LRU), F32),
            jax.ShapeDtypeStruct((batch, CONV_W - 1, D_LRU), F32),
        ],
        scratch_shapes=[pltpu.VMEM((seq + 8, LRU_BLOCK), F32), pltpu.VMEM((seq, LRU_BLOCK), F32),
                        pltpu.VMEM((seq, LRU_BLOCK), F32)],
        input_output_aliases={9: 0},
        compiler_params=_params(("arbitrary", "arbitrary"), 40),
        name="lru_prompt",
    )(a4, a4, *lru_w, u)


def _lru_sample_kernel(x_ref, g_ref, sc_ref, h0_ref, wc_ref, bc_ref, wa_ref, ba_ref, wx_ref, bx_ref, lam_ref,
                       u_any, u_ref, h_ref, cs_ref, xc_scr, a_scr, b_scr, y_scr, *, batch, t_new):
    del u_any
    n_state = CONV_W - 1

    def ext_row(b, i):
        if i < n_state:
            return sc_ref[b, i:i + 1, :]
        return x_ref[b, i - n_state:i - n_state + 1, :]

    for b in range(batch):
        for t in range(t_new):
            acc = bc_ref[...] + ext_row(b, t) * wc_ref[0:1, :]
            for j in range(1, CONV_W):
                acc = acc + ext_row(b, t + j) * wc_ref[j:j + 1, :]
            xc_scr[b * t_new + t:b * t_new + t + 1, :] = acc
    a, bb = _lru_gates(xc_scr[...], wa_ref, ba_ref, wx_ref, bx_ref, lam_ref)
    a_scr[...] = a
    b_scr[...] = bb
    for b in range(batch):
        h = h0_ref[b:b + 1, :]
        for t in range(t_new):
            r = b * t_new + t
            h = a_scr[r:r + 1, :] * h + b_scr[r:r + 1, :]
            y_scr[r:r + 1, :] = h
        h_ref[b:b + 1, :] = h
        for i in range(n_state):
            cs_ref[b, i:i + 1, :] = ext_row(b, t_new + i)
    for b in range(batch):
        g = g_ref[b]
        u_ref[b] = y_scr[b * t_new:(b + 1) * t_new, :] * (g * jax.nn.sigmoid(g))


def _lru_sample(a4, state_conv, state_h, lru_w, u, l, batch, t_new):
    blk = (None, batch, t_new, LRU_BLOCK)
    n_of = lambda n: n
    rows = batch * t_new
    return pl.pallas_call(
        functools.partial(_lru_sample_kernel, batch=batch, t_new=t_new),
        grid=(N_LRU_BLOCKS,),
        in_specs=[
            pl.BlockSpec(blk, lambda n: (2, 0, 0, n)),
            pl.BlockSpec(blk, lambda n: (3, 0, 0, n)),
            pl.BlockSpec((None, batch, CONV_W - 1, LRU_BLOCK), lambda n: (l, 0, 0, n)),
            pl.BlockSpec((None, batch, LRU_BLOCK), lambda n: (l, 0, n)),
        ] + _lru_weight_specs(l, n_of) + [pl.BlockSpec(memory_space=pl.ANY)],
        out_specs=[
            pl.BlockSpec((batch, t_new, LRU_BLOCK), lambda n: (0, 0, N_HEADS + n)),
            pl.BlockSpec((batch, LRU_BLOCK), lambda n: (0, n)),
            pl.BlockSpec((batch, CONV_W - 1, LRU_BLOCK), lambda n: (0, 0, n)),
        ],
        out_shape=[
            jax.ShapeDtypeStruct(u.shape, u.dtype),
            jax.ShapeDtypeStruct((batch, D_LRU), F32),
            jax.ShapeDtypeStruct((batch, CONV_W - 1, D_LRU), F32),
        ],
        scratch_shapes=[pltpu.VMEM((rows, LRU_BLOCK), F32)] * 4,
        input_output_aliases={11: 0},
        compiler_params=_params(("arbitrary",), 32),
        name="lru_sample",
    )(a4, a4, state_conv, state_h, *lru_w, u)


def _outproj_kernel(u_ref, w_ref, x_ref, gate_ref, gp_ref, y_ref, *, nj, tn):
    j = pl.program_id(1)
    acc = jnp.dot(u_ref[...].astype(BF16), w_ref[...], preferred_element_type=F32)
    for jj in range(nj):
        @pl.when(j == jj)
        def _(jj=jj):
            y_ref[:, jj * tn:(jj + 1) * tn] = acc

    @pl.when(j == nj - 1)
    def _():
        y = y_ref[...]
        y = y * lax.rsqrt(jnp.mean(y * y, axis=-1, keepdims=True) + EPS)
        y_ref[...] = x_ref[...] + gate_ref[...] * (y * gp_ref[...])


def _outproj(u2, w_out, x2, gate, mod_idx, g_post, l, *, tm, tn, rows_per_mod, vmem_mib):
    m_rows = x2.shape[0]
    nj = D_MODEL // tn
    if mod_idx is None:
        gate_spec = pl.BlockSpec((tm, D_MODEL), lambda i, j: (i, 0))
    else:
        gate_spec = pl.BlockSpec((None, 1, D_MODEL), lambda i, j: (mod_idx(i // rows_per_mod, 2), 0, 0))
    return pl.pallas_call(
        functools.partial(_outproj_kernel, nj=nj, tn=tn),
        grid=(m_rows // tm, nj),
        in_specs=[
            pl.BlockSpec((tm, D_MODEL), lambda i, j: (i, 0)),
            pl.BlockSpec((None, D_MODEL, tn), lambda i, j: (l, 0, j)),
            pl.BlockSpec((tm, D_MODEL), lambda i, j: (i, 0)),
            gate_spec,
            pl.BlockSpec((None, 1, D_MODEL), lambda i, j: (l, 0, 0)),
        ],
        out_specs=pl.BlockSpec((tm, D_MODEL), lambda i, j: (i, 0)),
        out_shape=jax.ShapeDtypeStruct((m_rows, D_MODEL), F32),
        compiler_params=_params(("arbitrary", "arbitrary"), vmem_mib),
        name="outproj",
    )(u2, w_out, x2, gate, g_post)


def kernel(x_prompt, x_sample, cache_k, cache_v, state_h, state_conv, c_prompt, c_sample, rel_table, w_ada, b_ada,
           g_pre, w_in, w_conv, b_conv, w_a, b_a, w_x, b_x, lam, w_out, g_post):
    depth = w_in.shape[0]
    bp, seq, _ = x_prompt.shape
    bs, t_new, _ = x_sample.shape
    c_len = cache_k.shape[2]
    assert bp + bs <= MOD_ROWS and c_len + t_new <= SAMPLE_KEYS and t_new <= 8
    assert seq % (L_BAND * DILATED_GROUPS[-1][1]) == 0 and c_len >= WINDOW_MAX

    c_all = jnp.concatenate([c_prompt, c_sample, jnp.zeros((MOD_ROWS - bp - bs, D_MODEL), F32)], axis=0)
    mod = _ada(c_all, w_ada, b_ada)
    mod_tab = mod.reshape(depth * MOD_ROWS * 3, 1, D_MODEL)
    pbias, sbias = _bias_tables(rel_table, c_len, t_new)

    w_in_b = w_in.astype(BF16)
    w_out_b = w_out.astype(BF16)
    g_pre3 = g_pre.reshape(depth, 1, D_MODEL)
    g_post3 = g_post.reshape(depth, 1, D_MODEL)

    xp = x_prompt.reshape(bp * seq, D_MODEL)
    xs = x_sample.reshape(bs * t_new, D_MODEL)
    lru_w = (w_conv, b_conv.reshape(depth, 1, D_LRU), w_a, b_a.reshape(depth, 1, D_LRU),
             w_x, b_x.reshape(depth, 1, D_LRU), lam.reshape(depth, 1, D_LRU))
    kv_p = kv_s = None
    hp_l, cp_l, hs_l, cs_l = [], [], [], []
    tm_p = 512
    for l in range(depth):
        p_idx = lambda b, which, l=l: (l * MOD_ROWS + b) * 3 + which
        a4, kp, vp = _inproj(xp, mod_tab, mod_tab, p_idx, g_pre3, w_in_b, l, kv_p,
                             tm=tm_p, tn=1024, rows_per_mod=seq // tm_p, vmem_mib=56)
        kv_p = (kp, vp)
        a4 = a4.reshape(4, bp, seq, D_ATT)
        kv5 = (depth, bp, seq, N_HEADS, HEAD_DIM)
        u = _attn_prompt(a4, kp.reshape(kv5), vp.reshape(kv5), pbias, l, bp, seq)
        u, h_last, conv = _lru_prompt(a4, lru_w, u, l, bp, seq)
        xp = _outproj(u.reshape(bp * seq, D_MODEL), w_out_b, xp, mod_tab, p_idx, g_post3, l,
                      tm=256, tn=1024, rows_per_mod=seq // 256, vmem_mib=48)
        hp_l.append(h_last.reshape(bp, D_LRU))
        cp_l.append(conv)
        mod_s = jnp.repeat(mod[l, bp:bp + bs], t_new, axis=0)
        rows = bs * t_new
        a4, ks, vs = _inproj(xs, mod_s[:, :D_MODEL], mod_s[:, D_MODEL:2 * D_MODEL], None, g_pre3, w_in_b, l, kv_s,
                             tm=rows, tn=1024, rows_per_mod=1, vmem_mib=40)
        kv_s = (ks, vs)
        a4 = a4.reshape(4, bs, t_new, D_ATT)
        kv5 = (depth, bs, t_new, N_HEADS, HEAD_DIM)
        u = _attn_sample(a4, ks.reshape(kv5), vs.reshape(kv5), cache_k, cache_v, sbias, l, bs, t_new, c_len)
        u, h_last, conv = _lru_sample(a4, state_conv, state_h, lru_w, u, l, bs, t_new)
        xs = _outproj(u.reshape(rows, D_MODEL), w_out_b, xs, mod_s[:, 2 * D_MODEL:], None, g_post3, l,
                      tm=rows, tn=512, rows_per_mod=1, vmem_mib=40)
        hs_l.append(h_last)
        cs_l.append(conv)

    kp, vp = kv_p
    ks, vs = kv_s
    return (xp.reshape(bp, seq, D_MODEL), xs.reshape(bs, t_new, D_MODEL),
            kp.reshape(depth, bp, seq, N_HEADS, HEAD_DIM), vp.reshape(depth, bp, seq, N_HEADS, HEAD_DIM),
            jnp.stack(hp_l), jnp.stack(cp_l),
            ks.reshape(depth, bs, t_new, N_HEADS, HEAD_DIM), vs.reshape(depth, bs, t_new, N_HEADS, HEAD_DIM),
            jnp.stack(hs_l), jnp.stack(cs_l))
```

```python
import functools
import math

import numpy as np
import jax
import jax.numpy as jnp
from jax import lax
from jax.experimental import pallas as pl
from jax.experimental.pallas import tpu as pltpu

D_MODEL = 4096
D_ATT = 2048
D_LRU = 2048
HEAD_DIM = 128
N_HEADS = 16
N_LRU_BLOCKS = 16
LRU_BLOCK = 128
CONV_W = 4
LRU_C = 8.0
DILATED_GROUPS = ((128, 1), (512, 4), (2048, 16))
N_GROUPS = len(DILATED_GROUPS)
WINDOW_MAX = 2048
N_BUCKETS = 32
MAX_EXACT = N_BUCKETS // 2
MAX_DISTANCE = WINDOW_MAX
EPS = 1e-6
ATT_SCALE = HEAD_DIM ** -0.5
D_IN = 4 * D_ATT + 2 * D_LRU
N_SEG = D_IN // D_ATT
L_BAND = 128
MASKED = -1e30
MOD_ROWS = 16
SAMPLE_KEYS = 2048 + 128
MIB = 1024 * 1024
NORM_CHUNK = 64
KV_HEADS = 8

F32 = jnp.float32
BF16 = jnp.bfloat16


def _params(semantics, vmem_mib):
    return pltpu.CompilerParams(dimension_semantics=semantics, vmem_limit_bytes=vmem_mib * MIB)


def _rel_bucket_np(dist):
    d = dist.astype(np.float32)
    large = np.float32(MAX_EXACT) + np.log(np.maximum(d, np.float32(1.0)) / np.float32(MAX_EXACT)) / np.float32(
        math.log(MAX_DISTANCE / MAX_EXACT)) * np.float32(N_BUCKETS - MAX_EXACT)
    large = np.minimum(large.astype(np.int32), N_BUCKETS - 1)
    return np.where(dist < MAX_EXACT, dist, large).astype(np.int32)


def _prompt_bucket_index():
    qi = np.arange(L_BAND)[:, None]
    kj = np.arange(2 * L_BAND)[None, :]
    dist = qi + L_BAND - kj
    band = (dist >= 0) & (dist <= L_BAND)
    out = []
    for _, dil in DILATED_GROUPS:
        b = _rel_bucket_np(np.clip(dist, 0, L_BAND) * dil)
        out.append(np.where(band, b, -1))
    return np.stack(out).astype(np.int32)


def _sample_bucket_index(c_len, t_new):
    t = np.arange(8)[:, None]
    idx = np.arange(SAMPLE_KEYS)[None, :]
    delta = c_len + t - idx
    out = []
    for window, dil in DILATED_GROUPS:
        valid = (delta >= 0) & (delta % dil == 0) & (delta <= window) & (idx < c_len + t_new) & (t < t_new)
        b = _rel_bucket_np(np.clip(delta, 0, window))
        b = np.where(valid, b, -1)
        b = np.where(t >= t_new, 0, b)
        out.append(b)
    return np.stack(out).astype(np.int32)


def _ada_kernel(c_ref, w_ref, b_ref, o_ref):
    c = c_ref[...]
    a = (c * jax.nn.sigmoid(c)).astype(BF16)
    o_ref[...] = jnp.dot(a, w_ref[...].astype(BF16), preferred_element_type=F32) + b_ref[...]


def _ada(c_all, w_ada, b_ada):
    depth = w_ada.shape[0]
    tn = 512
    return pl.pallas_call(
        _ada_kernel,
        grid=(depth, 3 * D_MODEL // tn),
        in_specs=[
            pl.BlockSpec((MOD_ROWS, D_MODEL), lambda l, j: (0, 0)),
            pl.BlockSpec((None, D_MODEL, tn), lambda l, j: (l, 0, j)),
            pl.BlockSpec((None, 1, tn), lambda l, j: (l, 0, j)),
        ],
        out_specs=pl.BlockSpec((None, MOD_ROWS, tn), lambda l, j: (l, 0, j)),
        out_shape=jax.ShapeDtypeStruct((depth, MOD_ROWS, 3 * D_MODEL), F32),
        compiler_params=_params(("arbitrary", "arbitrary"), 40),
        name="ada",
    )(c_all, w_ada, b_ada.reshape(depth, 1, 3 * D_MODEL))


def _bias_kernel(tab_ref, pidx_ref, sidx_ref, pb_ref, sb_ref):
    h = pl.program_id(0)
    for idx_ref, out_ref in ((pidx_ref, pb_ref), (sidx_ref, sb_ref)):
        for g in range(N_GROUPS):
            idx = idx_ref[g]
            acc = jnp.full(idx.shape, MASKED, F32)
            for b in range(N_BUCKETS):
                acc = jnp.where(idx == b, tab_ref[b, h], acc)
            out_ref[g] = acc


def _bias_tables(rel_table, c_len, t_new):
    pidx = jnp.asarray(_prompt_bucket_index())
    sidx = jnp.asarray(_sample_bucket_index(c_len, t_new))
    return pl.pallas_call(
        _bias_kernel,
        grid=(N_HEADS,),
        in_specs=[
            pl.BlockSpec(memory_space=pltpu.SMEM),
            pl.BlockSpec((N_GROUPS, L_BAND, 2 * L_BAND), lambda h: (0, 0, 0)),
            pl.BlockSpec((N_GROUPS, 8, SAMPLE_KEYS), lambda h: (0, 0, 0)),
        ],
        out_specs=[
            pl.BlockSpec((N_GROUPS, None, L_BAND, 2 * L_BAND), lambda h: (0, h, 0, 0)),
            pl.BlockSpec((N_GROUPS, None, 8, SAMPLE_KEYS), lambda h: (0, h, 0, 0)),
        ],
        out_shape=[
            jax.ShapeDtypeStruct((N_GROUPS, N_HEADS, L_BAND, 2 * L_BAND), F32),
            jax.ShapeDtypeStruct((N_GROUPS, N_HEADS, 8, SAMPLE_KEYS), F32),
        ],
        compiler_params=_params(("arbitrary",), 32),
        name="bias_tables",
    )(rel_table, pidx, sidx)


def _inproj_kernel(x_ref, shift_ref, scale_ref, g_ref, w_ref, *rest, nj_seg):
    a_ref, k_ref, v_ref, h_scr = rest[-4:]
    j = pl.program_id(1)

    tm = x_ref.shape[0]
    chunk = min(tm, NORM_CHUNK)

    @pl.when(j == 0)
    def _():
        per_row = shift_ref.shape[0] == tm

        def norm(c, _):
            rows = pl.ds(pl.multiple_of(c * chunk, chunk), chunk)
            mrows = rows if per_row else slice(None)
            x = x_ref[rows, :]
            y = x * lax.rsqrt(jnp.mean(x * x, axis=-1, keepdims=True) + EPS)
            y = y * g_ref[...]
            h_scr[rows, :] = (y * (1.0 + scale_ref[mrows, :]) + shift_ref[mrows, :]).astype(BF16)
            return 0

        lax.fori_loop(0, tm // chunk, norm, 0)

    acc = jnp.dot(h_scr[...], w_ref[...], preferred_element_type=F32)
    seg = j // nj_seg

    def store_heads(ref):
        heads = ref.shape[1]
        flat = ref.reshape(tm * heads, HEAD_DIM)
        for hh in range(heads):
            flat[pl.ds(hh, tm, stride=heads), :] = acc[:, hh * HEAD_DIM:(hh + 1) * HEAD_DIM]

    @pl.when(seg == 1)
    def _():
        store_heads(k_ref)

    @pl.when(seg == 2)
    def _():
        store_heads(v_ref)

    a_ref[...] = acc


def _inproj(x2, shift, scale, mod_idx, g_pre, w_in, l, kv_prev, *, tm, tn, rows_per_mod, vmem_mib):
    m_rows = x2.shape[0]
    depth = w_in.shape[0]
    nj_seg = D_ATT // tn
    nj = N_SEG * nj_seg

    def a_idx(i, j):
        return j // nj_seg, i, j % nj_seg

    def kv_idx(which):
        def idx(i, j):
            seg, half = j // nj_seg, j % nj_seg
            return l, i, jnp.where(seg < which, 0, jnp.where(seg == which, half, nj_seg - 1)), 0
        return idx

    heads = tn // HEAD_DIM
    assert heads % KV_HEADS == 0

    if mod_idx is None:
        mod_specs = [pl.BlockSpec((tm, D_MODEL), lambda i, j: (i, 0))] * 2
    else:
        mod_specs = [
            pl.BlockSpec((None, 1, D_MODEL), lambda i, j: (mod_idx(i // rows_per_mod, 0), 0, 0)),
            pl.BlockSpec((None, 1, D_MODEL), lambda i, j: (mod_idx(i // rows_per_mod, 1), 0, 0)),
        ]
    in_specs = [pl.BlockSpec((tm, D_MODEL), lambda i, j: (i, 0))] + mod_specs + [
        pl.BlockSpec((None, 1, D_MODEL), lambda i, j: (l, 0, 0)),
        pl.BlockSpec((None, D_MODEL, tn), lambda i, j: (l, 0, j)),
    ]
    args = [x2, shift, scale, g_pre, w_in]
    aliases = {}
    if kv_prev is not None:
        in_specs += [pl.BlockSpec(memory_space=pl.ANY)] * 2
        args += list(kv_prev)
        aliases = {5: 1, 6: 2}
    return pl.pallas_call(
        functools.partial(_inproj_kernel, nj_seg=nj_seg),
        grid=(m_rows // tm, nj),
        in_specs=in_specs,
        out_specs=[
            pl.BlockSpec((None, tm, tn), a_idx),
            pl.BlockSpec((None, tm, heads, HEAD_DIM), kv_idx(1)),
            pl.BlockSpec((None, tm, heads, HEAD_DIM), kv_idx(2)),
        ],
        out_shape=[
            jax.ShapeDtypeStruct((N_SEG, m_rows, D_ATT), F32),
            jax.ShapeDtypeStruct((depth, m_rows, N_HEADS, HEAD_DIM), F32),
            jax.ShapeDtypeStruct((depth, m_rows, N_HEADS, HEAD_DIM), F32),
        ],
        scratch_shapes=[pltpu.VMEM((tm, D_MODEL), BF16)],
        input_output_aliases=aliases,
        compiler_params=_params(("arbitrary", "arbitrary"), vmem_mib),
        name="inproj",
    )(*args)


def _rows(start, size, stride):
    return pl.ds(start, size) if stride == 1 else pl.ds(start, size, stride=stride)


def _attn_prompt_kernel(q_ref, k_ref, v_ref, g_ref, bias_ref, u_ref, qa, ka, va, qb, kb, vb, o_scr, e_scr, *, seq):
    n_blk = seq // L_BAND

    def regroup(dst, src, stride):
        sub = seq // stride
        for r in range(stride):
            dst[r * sub:(r + 1) * sub, :] = src[pl.ds(r, sub, stride=stride), :]

    def branch(gi, dil, q_src, k_src, v_src):
        per_res = n_blk // dil
        q3 = q_src[...].astype(BF16).reshape(n_blk, L_BAND, HEAD_DIM)
        k3 = k_src[...].astype(BF16).reshape(n_blk, L_BAND, HEAD_DIM)
        v3 = v_src[...].astype(BF16).reshape(n_blk, L_BAND, HEAD_DIM)
        if per_res > 1:
            kk = jnp.concatenate([jnp.concatenate([k3[:1], k3[:-1]], axis=0), k3], axis=1)
            vv = jnp.concatenate([jnp.concatenate([v3[:1], v3[:-1]], axis=0), v3], axis=1)
            blk = lax.broadcasted_iota(jnp.int32, (n_blk, 1, 2 * L_BAND), 0)
            col = lax.broadcasted_iota(jnp.int32, (n_blk, 1, 2 * L_BAND), 2)
            first = jnp.where((blk % per_res == 0) & (col < L_BAND), MASKED, 0.0)
            bias = bias_ref[gi][None] + first
        else:
            kk, vv = k3, v3
            bias = bias_ref[gi, :, L_BAND:][None]
        s = jnp.einsum('bqd,bkd->bqk', q3, kk, preferred_element_type=F32) * ATT_SCALE + bias
        mx = jnp.max(s, axis=-1, keepdims=True)
        p = jnp.exp(s - mx)
        den = jnp.sum(p, axis=-1, keepdims=True)
        o = jnp.einsum('bqk,bkd->bqd', p.astype(BF16), vv, preferred_element_type=F32) / den
        lse = jnp.broadcast_to(mx + jnp.log(den), (n_blk, L_BAND, HEAD_DIM))
        o = o.reshape(seq, HEAD_DIM)
        lse = lse.reshape(seq, HEAD_DIM)
        if dil == 1:
            o_scr[gi] = o
            e_scr[gi] = lse
        else:
            sub = seq // dil
            for r in range(dil):
                o_scr[gi, pl.ds(r, sub, stride=dil), :] = o[r * sub:(r + 1) * sub, :]
                e_scr[gi, pl.ds(r, sub, stride=dil), :] = lse[r * sub:(r + 1) * sub, :]

    (_, d0), (_, d1), (_, d2) = DILATED_GROUPS
    assert d0 == 1 and d2 == d1 * d1
    branch(0, d0, q_ref, k_ref, v_ref)
    for dst, src in ((qa, q_ref), (ka, k_ref), (va, v_ref)):
        regroup(dst, src, d1)
    branch(1, d1, qa, ka, va)
    for dst, src in ((qb, qa), (kb, ka), (vb, va)):
        regroup(dst, src, d1)
    branch(2, d2, qb, kb, vb)

    chunk = 256

    def combine(c, _):
        rows = pl.ds(pl.multiple_of(c * chunk, chunk), chunk)
        es = [e_scr[gi, rows, :] for gi in range(N_GROUPS)]
        top = functools.reduce(jnp.maximum, es)
        ws = [jnp.exp(e - top) for e in es]
        num = sum(w * o_scr[gi, rows, :] for gi, w in enumerate(ws))
        den = sum(ws)
        g = g_ref[rows, :]
        u_ref[rows, :] = ((num / den) * (g * jax.nn.sigmoid(g))).astype(u_ref.dtype)
        return 0

    lax.fori_loop(0, seq // chunk, combine, 0)


def _attn_prompt(a6, pbias, batch, seq):
    blk = (None, None, seq, HEAD_DIM)
    return pl.pallas_call(
        functools.partial(_attn_prompt_kernel, seq=seq),
        grid=(batch, N_HEADS),
        in_specs=[pl.BlockSpec(blk, lambda b, h, slot=slot: (slot, b, 0, h)) for slot in range(4)] + [
            pl.BlockSpec((N_GROUPS, None, L_BAND, 2 * L_BAND), lambda b, h: (0, h, 0, 0)),
        ],
        out_specs=pl.BlockSpec((None, seq, HEAD_DIM), lambda b, h: (b, 0, h)),
        out_shape=jax.ShapeDtypeStruct((batch, seq, D_MODEL), BF16),
        scratch_shapes=[pltpu.VMEM((seq, HEAD_DIM), F32)] * 6 + [pltpu.VMEM((N_GROUPS, seq, HEAD_DIM), F32)] * 2,
        compiler_params=_params(("arbitrary", "arbitrary"), 48),
        name="attn_prompt",
    )(a6, a6, a6, a6, pbias)


def _attn_sample_kernel(q_ref, kn_ref, vn_ref, ck_ref, cv_ref, g_ref, bias_ref, u_ref, q_scr, k_scr, v_scr,
                        *, c_len, t_new):
    pad = SAMPLE_KEYS - c_len - t_new
    hh = pl.program_id(1) % KV_HEADS
    q_scr[0:t_new, :] = q_ref[...]
    q_scr[t_new:8, :] = jnp.zeros((8 - t_new, HEAD_DIM), F32)
    for new_ref, cache_ref, scr in ((kn_ref, ck_ref, k_scr), (vn_ref, cv_ref, v_scr)):
        scr[0:c_len, :] = cache_ref.reshape(c_len * KV_HEADS, HEAD_DIM)[pl.ds(hh, c_len, stride=KV_HEADS), :]
        scr[c_len:c_len + t_new, :] = new_ref[...]
        scr[c_len + t_new:SAMPLE_KEYS, :] = jnp.zeros((pad, HEAD_DIM), F32)
    kk = k_scr[...].astype(BF16)
    vv = v_scr[...].astype(BF16)
    s = lax.dot_general(q_scr[...].astype(BF16), kk, (((1,), (1,)), ((), ())),
                        preferred_element_type=F32) * ATT_SCALE
    ms, ls, os_ = [], [], []
    for gi in range(N_GROUPS):
        sg = s + bias_ref[gi]
        mx = jnp.max(sg, axis=-1, keepdims=True)
        p = jnp.exp(sg - mx)
        ms.append(mx)
        ls.append(jnp.sum(p, axis=-1, keepdims=True))
        os_.append(jnp.dot(p.astype(BF16), vv, preferred_element_type=F32))
    top = functools.reduce(jnp.maximum, ms)
    ws = [jnp.exp(m - top) for m in ms]
    num = sum(w * o for w, o in zip(ws, os_))
    den = sum(w * d for w, d in zip(ws, ls))
    g = g_ref[...]
    u_ref[...] = (num / den)[0:t_new, :] * (g * jax.nn.sigmoid(g))


def _attn_sample(a6, cache_k, cache_v, sbias, l, batch, t_new, c_len):
    new_blk = (None, None, t_new, HEAD_DIM)
    cache_blk = (None, None, c_len, KV_HEADS, HEAD_DIM)
    return pl.pallas_call(
        functools.partial(_attn_sample_kernel, c_len=c_len, t_new=t_new),
        grid=(batch, N_HEADS),
        in_specs=[
            pl.BlockSpec(new_blk, lambda b, h: (0, b, 0, h)),
            pl.BlockSpec(new_blk, lambda b, h: (1, b, 0, h)),
            pl.BlockSpec(new_blk, lambda b, h: (2, b, 0, h)),
            pl.BlockSpec(cache_blk, lambda b, h: (l, b, 0, h // KV_HEADS, 0)),
            pl.BlockSpec(cache_blk, lambda b, h: (l, b, 0, h // KV_HEADS, 0)),
            pl.BlockSpec(new_blk, lambda b, h: (3, b, 0, h)),
            pl.BlockSpec((N_GROUPS, None, 8, SAMPLE_KEYS), lambda b, h: (0, h, 0, 0)),
        ],
        out_specs=pl.BlockSpec((None, t_new, HEAD_DIM), lambda b, h: (b, 0, h)),
        out_shape=jax.ShapeDtypeStruct((batch, t_new, D_MODEL), F32),
        scratch_shapes=[pltpu.VMEM((8, HEAD_DIM), F32), pltpu.VMEM((SAMPLE_KEYS, HEAD_DIM), F32),
                        pltpu.VMEM((SAMPLE_KEYS, HEAD_DIM), F32)],
        compiler_params=_params(("arbitrary", "arbitrary"), 48),
        name="attn_sample",
    )(a6, a6, a6, cache_k, cache_v, a6, sbias)


def _lru_gates(xc, wa_ref, ba_ref, wx_ref, bx_ref, lam_ref):
    xcb = xc.astype(BF16)
    r = jax.nn.sigmoid(jnp.dot(xcb, wa_ref[...].astype(BF16), preferred_element_type=F32) + ba_ref[...])
    i = jax.nn.sigmoid(jnp.dot(xcb, wx_ref[...].astype(BF16), preferred_element_type=F32) + bx_ref[...])
    nl = -lam_ref[...]
    softplus = jnp.maximum(nl, 0.0) + jnp.log1p(jnp.exp(-jnp.abs(nl)))
    log_a = -LRU_C * r * softplus
    a = jnp.exp(log_a)
    t = jnp.tanh(log_a)
    b = jnp.sqrt(-2.0 * t / (1.0 - t)) * (i * xc)
    return a, b


def _lru_prompt_kernel(x_ref, g_ref, wc_ref, bc_ref, wa_ref, ba_ref, wx_ref, bx_ref, lam_ref, u_any,
                       u_ref, h_ref, cs_ref, xpad, a_scr, b_scr, *, seq):
    del u_any
    x = x_ref[...]
    xpad[0:8, :] = jnp.zeros((8, LRU_BLOCK), F32)
    xpad[8:8 + seq, :] = x
    xc = bc_ref[...] + xpad[pl.ds(8 - 3, seq), :] * wc_ref[0:1, :]
    xc = xc + xpad[pl.ds(8 - 2, seq), :] * wc_ref[1:2, :]
    xc = xc + xpad[pl.ds(8 - 1, seq), :] * wc_ref[2:3, :]
    xc = xc + x * wc_ref[3:4, :]
    a, b = _lru_gates(xc, wa_ref, ba_ref, wx_ref, bx_ref, lam_ref)
    a_scr[...] = a
    b_scr[...] = b

    row = lax.broadcasted_iota(jnp.int32, (8, LRU_BLOCK), 0)

    def body(c, h_prev):
        rows = pl.ds(pl.multiple_of(c * 8, 8), 8)
        ca = a_scr[rows, :]
        cb = b_scr[rows, :]
        for k in (1, 2, 4):
            a_sh = jnp.where(row >= k, pltpu.roll(ca, k, 0), 1.0)
            b_sh = jnp.where(row >= k, pltpu.roll(cb, k, 0), 0.0)
            cb = ca * b_sh + cb
            ca = ca * a_sh
        h = ca * h_prev + cb
        b_scr[rows, :] = h
        return jnp.broadcast_to(h[7:8, :], (8, LRU_BLOCK))

    lax.fori_loop(0, seq // 8, body, jnp.zeros((8, LRU_BLOCK), F32), unroll=8)
    y = b_scr[...]
    g = g_ref[...]
    u_ref[...] = (y * (g * jax.nn.sigmoid(g))).astype(u_ref.dtype)
    h_ref[...] = b_scr[seq - 1:seq, :]
    cs_ref[...] = x[seq - (CONV_W - 1):seq, :]


def _lru_weight_specs(l, n_of):
    return [
        pl.BlockSpec((None, CONV_W, LRU_BLOCK), lambda *g: (l, 0, n_of(*g))),
        pl.BlockSpec((None, 1, LRU_BLOCK), lambda *g: (l, 0, n_of(*g))),
        pl.BlockSpec((None, None, LRU_BLOCK, LRU_BLOCK), lambda *g: (l, n_of(*g), 0, 0)),
        pl.BlockSpec((None, 1, LRU_BLOCK), lambda *g: (l, 0, n_of(*g))),
        pl.BlockSpec((None, None, LRU_BLOCK, LRU_BLOCK), lambda *g: (l, n_of(*g), 0, 0)),
        pl.BlockSpec((None, 1, LRU_BLOCK), lambda *g: (l, 0, n_of(*g))),
        pl.BlockSpec((None, 1, LRU_BLOCK), lambda *g: (l, 0, n_of(*g))),
    ]


def _lru_prompt(a6, lru_w, u, l, batch, seq):
    blk = (None, None, seq, LRU_BLOCK)
    n_of = lambda b, n: n
    return pl.pallas_call(
        functools.partial(_lru_prompt_kernel, seq=seq),
        grid=(batch, N_LRU_BLOCKS),
        in_specs=[
            pl.BlockSpec(blk, lambda b, n: (4, b, 0, n)),
            pl.BlockSpec(blk, lambda b, n: (5, b, 0, n)),
        ] + _lru_weight_specs(l, n_of) + [pl.BlockSpec(memory_space=pl.ANY)],
        out_specs=[
            pl.BlockSpec((None, seq, LRU_BLOCK), lambda b, n: (b, 0, N_HEADS + n)),
            pl.BlockSpec((None, 1, LRU_BLOCK), lambda b, n: (b, 0, n)),
            pl.BlockSpec((None, CONV_W - 1, LRU_BLOCK), lambda b, n: (b, 0, n)),
        ],
        out_shape=[
            jax.ShapeDtypeStruct(u.shape, u.dtype),
            jax.ShapeDtypeStruct((batch, 1, D_LRU), F32),
            jax.ShapeDtypeStruct((batch, CONV_W - 1, D_LRU), F32),
        ],
        scratch_shapes=[pltpu.VMEM((seq + 8, LRU_BLOCK), F32), pltpu.VMEM((seq, LRU_BLOCK), F32),
                        pltpu.VMEM((seq, LRU_BLOCK), F32)],
        input_output_aliases={9: 0},
        compiler_params=_params(("arbitrary", "arbitrary"), 40),
        name="lru_prompt",
    )(a6, a6, *lru_w, u)


def _lru_sample_kernel(x_ref, g_ref, sc_ref, h0_ref, wc_ref, bc_ref, wa_ref, ba_ref, wx_ref, bx_ref, lam_ref,
                       u_any, u_ref, h_ref, cs_ref, xc_scr, a_scr, b_scr, y_scr, *, batch, t_new):
    del u_any
    n_state = CONV_W - 1

    def ext_row(b, i):
        if i < n_state:
            return sc_ref[b, i:i + 1, :]
        return x_ref[b, i - n_state:i - n_state + 1, :]

    for b in range(batch):
        for t in range(t_new):
            acc = bc_ref[...] + ext_row(b, t) * wc_ref[0:1, :]
            for j in range(1, CONV_W):
                acc = acc + ext_row(b, t + j) * wc_ref[j:j + 1, :]
            xc_scr[b * t_new + t:b * t_new + t + 1, :] = acc
    a, bb = _lru_gates(xc_scr[...], wa_ref, ba_ref, wx_ref, bx_ref, lam_ref)
    a_scr[...] = a
    b_scr[...] = bb
    for b in range(batch):
        h = h0_ref[b:b + 1, :]
        for t in range(t_new):
            r = b * t_new + t
            h = a_scr[r:r + 1, :] * h + b_scr[r:r + 1, :]
            y_scr[r:r + 1, :] = h
        h_ref[b:b + 1, :] = h
        for i in range(n_state):
            cs_ref[b, i:i + 1, :] = ext_row(b, t_new + i)
    for b in range(batch):
        g = g_ref[b]
        u_ref[b] = y_scr[b * t_new:(b + 1) * t_new, :] * (g * jax.nn.sigmoid(g))


def _lru_sample(a6, state_conv, state_h, lru_w, u, l, batch, t_new):
    blk = (None, batch, t_new, LRU_BLOCK)
    n_of = lambda n: n
    rows = batch * t_new
    return pl.pallas_call(
        functools.partial(_lru_sample_kernel, batch=batch, t_new=t_new),
        grid=(N_LRU_BLOCKS,),
        in_specs=[
            pl.BlockSpec(blk, lambda n: (4, 0, 0, n)),
            pl.BlockSpec(blk, lambda n: (5, 0, 0, n)),
            pl.BlockSpec((None, batch, CONV_W - 1, LRU_BLOCK), lambda n: (l, 0, 0, n)),
            pl.BlockSpec((None, batch, LRU_BLOCK), lambda n: (l, 0, n)),
        ] + _lru_weight_specs(l, n_of) + [pl.BlockSpec(memory_space=pl.ANY)],
        out_specs=[
            pl.BlockSpec((batch, t_new, LRU_BLOCK), lambda n: (0, 0, N_HEADS + n)),
            pl.BlockSpec((batch, LRU_BLOCK), lambda n: (0, n)),
            pl.BlockSpec((batch, CONV_W - 1, LRU_BLOCK), lambda n: (0, 0, n)),
        ],
        out_shape=[
            jax.ShapeDtypeStruct(u.shape, u.dtype),
            jax.ShapeDtypeStruct((batch, D_LRU), F32),
            jax.ShapeDtypeStruct((batch, CONV_W - 1, D_LRU), F32),
        ],
        scratch_shapes=[pltpu.VMEM((rows, LRU_BLOCK), F32)] * 4,
        input_output_aliases={11: 0},
        compiler_params=_params(("arbitrary",), 32),
        name="lru_sample",
    )(a6, a6, state_conv, state_h, *lru_w, u)


def _outproj_kernel(u_ref, w_ref, x_ref, gate_ref, gp_ref, y_ref, *, nj, tn):
    j = pl.program_id(1)
    acc = jnp.dot(u_ref[...].astype(BF16), w_ref[...], preferred_element_type=F32)
    for jj in range(nj):
        @pl.when(j == jj)
        def _(jj=jj):
            y_ref[:, jj * tn:(jj + 1) * tn] = acc

    tm = x_ref.shape[0]
    chunk = min(tm, NORM_CHUNK)

    @pl.when(j == nj - 1)
    def _():
        per_row = gate_ref.shape[0] == tm

        def finish(c, _):
            rows = pl.ds(pl.multiple_of(c * chunk, chunk), chunk)
            y = y_ref[rows, :]
            y = y * lax.rsqrt(jnp.mean(y * y, axis=-1, keepdims=True) + EPS)
            y_ref[rows, :] = x_ref[rows, :] + gate_ref[rows if per_row else slice(None), :] * (y * gp_ref[...])
            return 0

        lax.fori_loop(0, tm // chunk, finish, 0)


def _outproj(u2, w_out, x2, gate, mod_idx, g_post, l, *, tm, tn, rows_per_mod, vmem_mib):
    m_rows = x2.shape[0]
    nj = D_MODEL // tn
    if mod_idx is None:
        gate_spec = pl.BlockSpec((tm, D_MODEL), lambda i, j: (i, 0))
    else:
        gate_spec = pl.BlockSpec((None, 1, D_MODEL), lambda i, j: (mod_idx(i // rows_per_mod, 2), 0, 0))
    return pl.pallas_call(
        functools.partial(_outproj_kernel, nj=nj, tn=tn),
        grid=(m_rows // tm, nj),
        in_specs=[
            pl.BlockSpec((tm, D_MODEL), lambda i, j: (i, 0)),
            pl.BlockSpec((None, D_MODEL, tn), lambda i, j: (l, 0, j)),
            pl.BlockSpec((tm, D_MODEL), lambda i, j: (i, 0)),
            gate_spec,
            pl.BlockSpec((None, 1, D_MODEL), lambda i, j: (l, 0, 0)),
        ],
        out_specs=pl.BlockSpec((tm, D_MODEL), lambda i, j: (i, 0)),
        out_shape=jax.ShapeDtypeStruct((m_rows, D_MODEL), F32),
        compiler_params=_params(("arbitrary", "arbitrary"), vmem_mib),
        name="outproj",
    )(u2, w_out, x2, gate, g_post)


def kernel(x_prompt, x_sample, cache_k, cache_v, state_h, state_conv, c_prompt, c_sample, rel_table, w_ada, b_ada,
           g_pre, w_in, w_conv, b_conv, w_a, b_a, w_x, b_x, lam, w_out, g_post):
    depth = w_in.shape[0]
    bp, seq, _ = x_prompt.shape
    bs, t_new, _ = x_sample.shape
    c_len = cache_k.shape[2]
    assert bp + bs <= MOD_ROWS and c_len + t_new <= SAMPLE_KEYS and t_new <= 8
    assert seq % (L_BAND * DILATED_GROUPS[-1][1]) == 0 and c_len >= WINDOW_MAX

    c_all = jnp.concatenate([c_prompt, c_sample, jnp.zeros((MOD_ROWS - bp - bs, D_MODEL), F32)], axis=0)
    mod = _ada(c_all, w_ada, b_ada)
    mod_tab = mod.reshape(depth * MOD_ROWS * 3, 1, D_MODEL)
    pbias, sbias = _bias_tables(rel_table, c_len, t_new)

    w_in_b = w_in.astype(BF16)
    w_out_b = w_out.astype(BF16)
    g_pre3 = g_pre.reshape(depth, 1, D_MODEL)
    g_post3 = g_post.reshape(depth, 1, D_MODEL)

    xp = x_prompt.reshape(bp * seq, D_MODEL)
    xs = x_sample.reshape(bs * t_new, D_MODEL)
    lru_w = (w_conv, b_conv.reshape(depth, 1, D_LRU), w_a, b_a.reshape(depth, 1, D_LRU),
             w_x, b_x.reshape(depth, 1, D_LRU), lam.reshape(depth, 1, D_LRU))
    kv_p = kv_s = None
    hp_l, cp_l, hs_l, cs_l = [], [], [], []
    tm_p = 512
    for l in range(depth):
        p_idx = lambda b, which, l=l: (l * MOD_ROWS + b) * 3 + which
        a6, kp, vp = _inproj(xp, mod_tab, mod_tab, p_idx, g_pre3, w_in_b, l, kv_p,
                             tm=tm_p, tn=1024, rows_per_mod=seq // tm_p, vmem_mib=56)
        kv_p = (kp, vp)
        a6 = a6.reshape(N_SEG, bp, seq, D_ATT)
        u = _attn_prompt(a6, pbias, bp, seq)
        u, h_last, conv = _lru_prompt(a6, lru_w, u, l, bp, seq)
        xp = _outproj(u.reshape(bp * seq, D_MODEL), w_out_b, xp, mod_tab, p_idx, g_post3, l,
                      tm=tm_p, tn=512, rows_per_mod=seq // tm_p, vmem_mib=56)
        hp_l.append(h_last.reshape(bp, D_LRU))
        cp_l.append(conv)
        mod_s = jnp.repeat(mod[l, bp:bp + bs], t_new, axis=0)
        rows = bs * t_new
        a6, ks, vs = _inproj(xs, mod_s[:, :D_MODEL], mod_s[:, D_MODEL:2 * D_MODEL], None, g_pre3, w_in_b, l, kv_s,
                             tm=rows, tn=1024, rows_per_mod=1, vmem_mib=40)
        kv_s = (ks, vs)
        a6 = a6.reshape(N_SEG, bs, t_new, D_ATT)
        u = _attn_sample(a6, cache_k, cache_v, sbias, l, bs, t_new, c_len)
        u, h_last, conv = _lru_sample(a6, state_conv, state_h, lru_w, u, l, bs, t_new)
        xs = _outproj(u.reshape(rows, D_MODEL), w_out_b, xs, mod_s[:, 2 * D_MODEL:], None, g_post3, l,
                      tm=rows, tn=512, rows_per_mod=1, vmem_mib=40)
        hs_l.append(h_last)
        cs_l.append(conv)

    kp, vp = kv_p
    ks, vs = kv_s
    return (xp.reshape(bp, seq, D_MODEL), xs.reshape(bs, t_new, D_MODEL),
            kp.reshape(depth, bp, seq, N_HEADS, HEAD_DIM), vp.reshape(depth, bp, seq, N_HEADS, HEAD_DIM),
            jnp.stack(hp_l), jnp.stack(cp_l),
            ks.reshape(depth, bs, t_new, N_HEADS, HEAD_DIM), vs.reshape(depth, bs, t_new, N_HEADS, HEAD_DIM),
            jnp.stack(hs_l), jnp.stack(cs_l))
```

```python
import functools
import math

import numpy as np
import jax
import jax.numpy as jnp
from jax import lax
from jax.experimental import pallas as pl
from jax.experimental.pallas import tpu as pltpu

D_MODEL = 4096
D_ATT = 2048
D_LRU = 2048
HEAD_DIM = 128
N_HEADS = 16
N_LRU_BLOCKS = 16
LRU_BLOCK = 128
CONV_W = 4
LRU_C = 8.0
DILATED_GROUPS = ((128, 1), (512, 4), (2048, 16))
N_GROUPS = len(DILATED_GROUPS)
WINDOW_MAX = 2048
N_BUCKETS = 32
MAX_EXACT = N_BUCKETS // 2
MAX_DISTANCE = WINDOW_MAX
EPS = 1e-6
ATT_SCALE = HEAD_DIM ** -0.5
D_IN = 4 * D_ATT + 2 * D_LRU
N_SEG = D_IN // D_ATT
L_BAND = 128
MASKED = -1e30
MOD_ROWS = 16
SAMPLE_KEYS = 2048 + 128
MIB = 1024 * 1024
NORM_CHUNK = 64
KV_HEADS = 8

F32 = jnp.float32
BF16 = jnp.bfloat16


def _params(semantics, vmem_mib):
    return pltpu.CompilerParams(dimension_semantics=semantics, vmem_limit_bytes=vmem_mib * MIB)


def _rel_bucket_np(dist):
    d = dist.astype(np.float32)
    large = np.float32(MAX_EXACT) + np.log(np.maximum(d, np.float32(1.0)) / np.float32(MAX_EXACT)) / np.float32(
        math.log(MAX_DISTANCE / MAX_EXACT)) * np.float32(N_BUCKETS - MAX_EXACT)
    large = np.minimum(large.astype(np.int32), N_BUCKETS - 1)
    return np.where(dist < MAX_EXACT, dist, large).astype(np.int32)


def _prompt_bucket_index():
    qi = np.arange(L_BAND)[:, None]
    kj = np.arange(2 * L_BAND)[None, :]
    dist = qi + L_BAND - kj
    band = (dist >= 0) & (dist <= L_BAND)
    out = []
    for _, dil in DILATED_GROUPS:
        b = _rel_bucket_np(np.clip(dist, 0, L_BAND) * dil)
        out.append(np.where(band, b, -1))
    return np.stack(out).astype(np.int32)


def _sample_bucket_index(c_len, t_new):
    t = np.arange(8)[:, None]
    idx = np.arange(SAMPLE_KEYS)[None, :]
    delta = c_len + t - idx
    out = []
    for window, dil in DILATED_GROUPS:
        valid = (delta >= 0) & (delta % dil == 0) & (delta <= window) & (idx < c_len + t_new) & (t < t_new)
        b = _rel_bucket_np(np.clip(delta, 0, window))
        b = np.where(valid, b, -1)
        b = np.where(t >= t_new, 0, b)
        out.append(b)
    return np.stack(out).astype(np.int32)


def _ada_kernel(c_ref, w_ref, b_ref, o_ref):
    c = c_ref[...]
    a = (c * jax.nn.sigmoid(c)).astype(BF16)
    o_ref[...] = jnp.dot(a, w_ref[...].astype(BF16), preferred_element_type=F32) + b_ref[...]


def _ada(c_all, w_ada, b_ada):
    depth = w_ada.shape[0]
    tn = 512
    return pl.pallas_call(
        _ada_kernel,
        grid=(depth, 3 * D_MODEL // tn),
        in_specs=[
            pl.BlockSpec((MOD_ROWS, D_MODEL), lambda l, j: (0, 0)),
            pl.BlockSpec((None, D_MODEL, tn), lambda l, j: (l, 0, j)),
            pl.BlockSpec((None, 1, tn), lambda l, j: (l, 0, j)),
        ],
        out_specs=pl.BlockSpec((None, MOD_ROWS, tn), lambda l, j: (l, 0, j)),
        out_shape=jax.ShapeDtypeStruct((depth, MOD_ROWS, 3 * D_MODEL), F32),
        compiler_params=_params(("arbitrary", "arbitrary"), 40),
        name="ada",
    )(c_all, w_ada, b_ada.reshape(depth, 1, 3 * D_MODEL))


def _bias_kernel(tab_ref, pidx_ref, sidx_ref, pb_ref, sb_ref):
    h = pl.program_id(0)
    for idx_ref, out_ref in ((pidx_ref, pb_ref), (sidx_ref, sb_ref)):
        for g in range(N_GROUPS):
            idx = idx_ref[g]
            acc = jnp.full(idx.shape, MASKED, F32)
            for b in range(N_BUCKETS):
                acc = jnp.where(idx == b, tab_ref[b, h], acc)
            out_ref[g] = acc


def _bias_tables(rel_table, c_len, t_new):
    pidx = jnp.asarray(_prompt_bucket_index())
    sidx = jnp.asarray(_sample_bucket_index(c_len, t_new))
    return pl.pallas_call(
        _bias_kernel,
        grid=(N_HEADS,),
        in_specs=[
            pl.BlockSpec(memory_space=pltpu.SMEM),
            pl.BlockSpec((N_GROUPS, L_BAND, 2 * L_BAND), lambda h: (0, 0, 0)),
            pl.BlockSpec((N_GROUPS, 8, SAMPLE_KEYS), lambda h: (0, 0, 0)),
        ],
        out_specs=[
            pl.BlockSpec((N_GROUPS, None, L_BAND, 2 * L_BAND), lambda h: (0, h, 0, 0)),
            pl.BlockSpec((N_GROUPS, None, 8, SAMPLE_KEYS), lambda h: (0, h, 0, 0)),
        ],
        out_shape=[
            jax.ShapeDtypeStruct((N_GROUPS, N_HEADS, L_BAND, 2 * L_BAND), F32),
            jax.ShapeDtypeStruct((N_GROUPS, N_HEADS, 8, SAMPLE_KEYS), F32),
        ],
        compiler_params=_params(("arbitrary",), 32),
        name="bias_tables",
    )(rel_table, pidx, sidx)


def _inproj_kernel(x_ref, shift_ref, scale_ref, g_ref, w_ref, *rest, nj_seg):
    a_ref, k_ref, v_ref, h_scr = rest[-4:]
    j = pl.program_id(1)

    tm = x_ref.shape[0]
    chunk = min(tm, NORM_CHUNK)

    @pl.when(j == 0)
    def _():
        per_row = shift_ref.shape[0] == tm

        def norm(c, _):
            rows = pl.ds(pl.multiple_of(c * chunk, chunk), chunk)
            mrows = rows if per_row else slice(None)
            x = x_ref[rows, :]
            y = x * lax.rsqrt(jnp.mean(x * x, axis=-1, keepdims=True) + EPS)
            y = y * g_ref[...]
            h_scr[rows, :] = (y * (1.0 + scale_ref[mrows, :]) + shift_ref[mrows, :]).astype(BF16)
            return 0

        lax.fori_loop(0, tm // chunk, norm, 0)

    acc = jnp.dot(h_scr[...], w_ref[...], preferred_element_type=F32)
    seg = j // nj_seg

    def store_heads(ref):
        heads = ref.shape[1]
        flat = ref.reshape(tm * heads, HEAD_DIM)
        for hh in range(heads):
            flat[pl.ds(hh, tm, stride=heads), :] = acc[:, hh * HEAD_DIM:(hh + 1) * HEAD_DIM]

    @pl.when(seg == 1)
    def _():
        store_heads(k_ref)

    @pl.when(seg == 2)
    def _():
        store_heads(v_ref)

    a_ref[...] = acc


def _inproj(x2, shift, scale, mod_idx, g_pre, w_in, l, kv_prev, *, tm, tn, rows_per_mod, vmem_mib):
    m_rows = x2.shape[0]
    depth = w_in.shape[0]
    nj_seg = D_ATT // tn
    nj = N_SEG * nj_seg

    def a_idx(i, j):
        return j // nj_seg, i, j % nj_seg

    def kv_idx(which):
        def idx(i, j):
            seg, half = j // nj_seg, j % nj_seg
            return l, i, jnp.where(seg < which, 0, jnp.where(seg == which, half, nj_seg - 1)), 0
        return idx

    heads = tn // HEAD_DIM
    assert heads % KV_HEADS == 0

    if mod_idx is None:
        mod_specs = [pl.BlockSpec((tm, D_MODEL), lambda i, j: (i, 0))] * 2
    else:
        mod_specs = [
            pl.BlockSpec((None, 1, D_MODEL), lambda i, j: (mod_idx(i // rows_per_mod, 0), 0, 0)),
            pl.BlockSpec((None, 1, D_MODEL), lambda i, j: (mod_idx(i // rows_per_mod, 1), 0, 0)),
        ]
    in_specs = [pl.BlockSpec((tm, D_MODEL), lambda i, j: (i, 0))] + mod_specs + [
        pl.BlockSpec((None, 1, D_MODEL), lambda i, j: (l, 0, 0)),
        pl.BlockSpec((None, D_MODEL, tn), lambda i, j: (l, 0, j)),
    ]
    args = [x2, shift, scale, g_pre, w_in]
    aliases = {}
    if kv_prev is not None:
        in_specs += [pl.BlockSpec(memory_space=pl.ANY)] * 2
        args += list(kv_prev)
        aliases = {5: 1, 6: 2}
    return pl.pallas_call(
        functools.partial(_inproj_kernel, nj_seg=nj_seg),
        grid=(m_rows // tm, nj),
        in_specs=in_specs,
        out_specs=[
            pl.BlockSpec((None, tm, tn), a_idx),
            pl.BlockSpec((None, tm, heads, HEAD_DIM), kv_idx(1)),
            pl.BlockSpec((None, tm, heads, HEAD_DIM), kv_idx(2)),
        ],
        out_shape=[
            jax.ShapeDtypeStruct((N_SEG, m_rows, D_ATT), F32),
            jax.ShapeDtypeStruct((depth, m_rows, N_HEADS, HEAD_DIM), F32),
            jax.ShapeDtypeStruct((depth, m_rows, N_HEADS, HEAD_DIM), F32),
        ],
        scratch_shapes=[pltpu.VMEM((tm, D_MODEL), BF16)],
        input_output_aliases=aliases,
        compiler_params=_params(("arbitrary", "arbitrary"), vmem_mib),
        name="inproj",
    )(*args)


def _rows(start, size, stride):
    return pl.ds(start, size) if stride == 1 else pl.ds(start, size, stride=stride)


def _attn_prompt_kernel(q_ref, k_ref, v_ref, g_ref, bias_ref, u_ref, qa, ka, va, qb, kb, vb, o_scr, e_scr, *, seq):
    n_blk = seq // L_BAND

    def regroup(dst, src, stride):
        sub = seq // stride
        for r in range(stride):
            dst[r * sub:(r + 1) * sub, :] = src[pl.ds(r, sub, stride=stride), :]

    def branch(gi, dil, q_src, k_src, v_src):
        per_res = n_blk // dil
        q3 = q_src[...].astype(BF16).reshape(n_blk, L_BAND, HEAD_DIM)
        k3 = k_src[...].astype(BF16).reshape(n_blk, L_BAND, HEAD_DIM)
        v3 = v_src[...].astype(BF16).reshape(n_blk, L_BAND, HEAD_DIM)
        if per_res > 1:
            kk = jnp.concatenate([jnp.concatenate([k3[:1], k3[:-1]], axis=0), k3], axis=1)
            vv = jnp.concatenate([jnp.concatenate([v3[:1], v3[:-1]], axis=0), v3], axis=1)
            blk = lax.broadcasted_iota(jnp.int32, (n_blk, 1, 2 * L_BAND), 0)
            col = lax.broadcasted_iota(jnp.int32, (n_blk, 1, 2 * L_BAND), 2)
            first = jnp.where((blk % per_res == 0) & (col < L_BAND), MASKED, 0.0)
            bias = bias_ref[gi][None] + first
        else:
            kk, vv = k3, v3
            bias = bias_ref[gi, :, L_BAND:][None]
        s = jnp.einsum('bqd,bkd->bqk', q3, kk, preferred_element_type=F32) * ATT_SCALE + bias
        mx = jnp.max(s, axis=-1, keepdims=True)
        p = jnp.exp(s - mx)
        den = jnp.sum(p, axis=-1, keepdims=True)
        o = jnp.einsum('bqk,bkd->bqd', p.astype(BF16), vv, preferred_element_type=F32) / den
        lse = jnp.broadcast_to(mx + jnp.log(den), (n_blk, L_BAND, HEAD_DIM))
        o = o.reshape(seq, HEAD_DIM)
        lse = lse.reshape(seq, HEAD_DIM)
        if dil == 1:
            o_scr[gi] = o
            e_scr[gi] = lse
        else:
            sub = seq // dil
            for r in range(dil):
                o_scr[gi, pl.ds(r, sub, stride=dil), :] = o[r * sub:(r + 1) * sub, :]
                e_scr[gi, pl.ds(r, sub, stride=dil), :] = lse[r * sub:(r + 1) * sub, :]

    (_, d0), (_, d1), (_, d2) = DILATED_GROUPS
    assert d0 == 1 and d2 == d1 * d1
    branch(0, d0, q_ref, k_ref, v_ref)
    for dst, src in ((qa, q_ref), (ka, k_ref), (va, v_ref)):
        regroup(dst, src, d1)
    branch(1, d1, qa, ka, va)
    for dst, src in ((qb, qa), (kb, ka), (vb, va)):
        regroup(dst, src, d1)
    branch(2, d2, qb, kb, vb)

    chunk = 256

    def combine(c, _):
        rows = pl.ds(pl.multiple_of(c * chunk, chunk), chunk)
        es = [e_scr[gi, rows, :] for gi in range(N_GROUPS)]
        top = functools.reduce(jnp.maximum, es)
        ws = [jnp.exp(e - top) for e in es]
        num = sum(w * o_scr[gi, rows, :] for gi, w in enumerate(ws))
        den = sum(ws)
        g = g_ref[rows, :]
        u_ref[rows, :] = ((num / den) * (g * jax.nn.sigmoid(g))).astype(u_ref.dtype)
        return 0

    lax.fori_loop(0, seq // chunk, combine, 0)


def _attn_prompt(a6, pbias, batch, seq):
    blk = (None, None, seq, HEAD_DIM)
    return pl.pallas_call(
        functools.partial(_attn_prompt_kernel, seq=seq),
        grid=(batch, N_HEADS),
        in_specs=[pl.BlockSpec(blk, lambda b, h, slot=slot: (slot, b, 0, h)) for slot in range(4)] + [
            pl.BlockSpec((N_GROUPS, None, L_BAND, 2 * L_BAND), lambda b, h: (0, h, 0, 0)),
        ],
        out_specs=pl.BlockSpec((None, seq, HEAD_DIM), lambda b, h: (b, 0, h)),
        out_shape=jax.ShapeDtypeStruct((batch, seq, D_MODEL), BF16),
        scratch_shapes=[pltpu.VMEM((seq, HEAD_DIM), F32)] * 6 + [pltpu.VMEM((N_GROUPS, seq, HEAD_DIM), F32)] * 2,
        compiler_params=_params(("arbitrary", "arbitrary"), 48),
        name="attn_prompt",
    )(a6, a6, a6, a6, pbias)


def _attn_sample_kernel(q_ref, kn_ref, vn_ref, ck_ref, cv_ref, g_ref, bias_ref, u_ref, q_scr, k_scr, v_scr,
                        *, c_len, t_new):
    pad = SAMPLE_KEYS - c_len - t_new
    q_scr[t_new:8, :] = jnp.zeros((8 - t_new, HEAD_DIM), F32)
    for slot in range(2):
        k_scr[slot, c_len + t_new:SAMPLE_KEYS, :] = jnp.zeros((pad, HEAD_DIM), F32)
        v_scr[slot, c_len + t_new:SAMPLE_KEYS, :] = jnp.zeros((pad, HEAD_DIM), F32)
    for hh in range(KV_HEADS):
        lanes = slice(hh * HEAD_DIM, (hh + 1) * HEAD_DIM)
        slot = hh % 2
        q_scr[0:t_new, :] = q_ref[:, lanes]
        for new_ref, cache_ref, scr in ((kn_ref, ck_ref, k_scr), (vn_ref, cv_ref, v_scr)):
            heads = cache_ref.reshape(c_len * KV_HEADS, HEAD_DIM)
            scr[slot, 0:c_len, :] = heads[pl.ds(hh, c_len, stride=KV_HEADS), :]
            scr[slot, c_len:c_len + t_new, :] = new_ref[:, lanes]
        kk = k_scr[slot].astype(BF16)
        vv = v_scr[slot].astype(BF16)
        s = lax.dot_general(q_scr[...].astype(BF16), kk, (((1,), (1,)), ((), ())),
                            preferred_element_type=F32) * ATT_SCALE
        ms, ls, os_ = [], [], []
        for gi in range(N_GROUPS):
            sg = s + bias_ref[gi, hh]
            mx = jnp.max(sg, axis=-1, keepdims=True)
            p = jnp.exp(sg - mx)
            ms.append(mx)
            ls.append(jnp.sum(p, axis=-1, keepdims=True))
            os_.append(jnp.dot(p.astype(BF16), vv, preferred_element_type=F32))
        top = functools.reduce(jnp.maximum, ms)
        ws = [jnp.exp(m - top) for m in ms]
        num = sum(w * o for w, o in zip(ws, os_))
        den = sum(w * d for w, d in zip(ws, ls))
        g = g_ref[:, lanes]
        u_ref[:, lanes] = (num / den)[0:t_new, :] * (g * jax.nn.sigmoid(g))


def _attn_sample(a6, cache_k, cache_v, sbias, l, batch, t_new, c_len):
    new_blk = (None, None, t_new, KV_HEADS * HEAD_DIM)
    cache_blk = (None, None, c_len, KV_HEADS, HEAD_DIM)
    return pl.pallas_call(
        functools.partial(_attn_sample_kernel, c_len=c_len, t_new=t_new),
        grid=(batch, N_HEADS // KV_HEADS),
        in_specs=[
            pl.BlockSpec(new_blk, lambda b, hg: (0, b, 0, hg)),
            pl.BlockSpec(new_blk, lambda b, hg: (1, b, 0, hg)),
            pl.BlockSpec(new_blk, lambda b, hg: (2, b, 0, hg)),
            pl.BlockSpec(cache_blk, lambda b, hg: (l, b, 0, hg, 0)),
            pl.BlockSpec(cache_blk, lambda b, hg: (l, b, 0, hg, 0)),
            pl.BlockSpec(new_blk, lambda b, hg: (3, b, 0, hg)),
            pl.BlockSpec((N_GROUPS, KV_HEADS, 8, SAMPLE_KEYS), lambda b, hg: (0, hg, 0, 0)),
        ],
        out_specs=pl.BlockSpec((None, t_new, KV_HEADS * HEAD_DIM), lambda b, hg: (b, 0, hg)),
        out_shape=jax.ShapeDtypeStruct((batch, t_new, D_MODEL), F32),
        scratch_shapes=[pltpu.VMEM((8, HEAD_DIM), F32), pltpu.VMEM((2, SAMPLE_KEYS, HEAD_DIM), F32),
                        pltpu.VMEM((2, SAMPLE_KEYS, HEAD_DIM), F32)],
        compiler_params=_params(("arbitrary", "arbitrary"), 48),
        name="attn_sample",
    )(a6, a6, a6, cache_k, cache_v, a6, sbias)


def _lru_gates(xc, wa_ref, ba_ref, wx_ref, bx_ref, lam_ref):
    xcb = xc.astype(BF16)
    r = jax.nn.sigmoid(jnp.dot(xcb, wa_ref[...].astype(BF16), preferred_element_type=F32) + ba_ref[...])
    i = jax.nn.sigmoid(jnp.dot(xcb, wx_ref[...].astype(BF16), preferred_element_type=F32) + bx_ref[...])
    nl = -lam_ref[...]
    softplus = jnp.maximum(nl, 0.0) + jnp.log1p(jnp.exp(-jnp.abs(nl)))
    log_a = -LRU_C * r * softplus
    a = jnp.exp(log_a)
    t = jnp.tanh(log_a)
    b = jnp.sqrt(-2.0 * t / (1.0 - t)) * (i * xc)
    return a, b


def _lru_prompt_kernel(x_ref, g_ref, wc_ref, bc_ref, wa_ref, ba_ref, wx_ref, bx_ref, lam_ref, u_any,
                       u_ref, h_ref, cs_ref, xpad, a_scr, b_scr, *, seq):
    del u_any
    x = x_ref[...]
    xpad[0:8, :] = jnp.zeros((8, LRU_BLOCK), F32)
    xpad[8:8 + seq, :] = x
    xc = bc_ref[...] + xpad[pl.ds(8 - 3, seq), :] * wc_ref[0:1, :]
    xc = xc + xpad[pl.ds(8 - 2, seq), :] * wc_ref[1:2, :]
    xc = xc + xpad[pl.ds(8 - 1, seq), :] * wc_ref[2:3, :]
    xc = xc + x * wc_ref[3:4, :]
    a, b = _lru_gates(xc, wa_ref, ba_ref, wx_ref, bx_ref, lam_ref)
    a_scr[...] = a
    b_scr[...] = b

    row = lax.broadcasted_iota(jnp.int32, (8, LRU_BLOCK), 0)

    def body(c, h_prev):
        rows = pl.ds(pl.multiple_of(c * 8, 8), 8)
        ca = a_scr[rows, :]
        cb = b_scr[rows, :]
        for k in (1, 2, 4):
            a_sh = jnp.where(row >= k, pltpu.roll(ca, k, 0), 1.0)
            b_sh = jnp.where(row >= k, pltpu.roll(cb, k, 0), 0.0)
            cb = ca * b_sh + cb
            ca = ca * a_sh
        h = ca * h_prev + cb
        b_scr[rows, :] = h
        return jnp.broadcast_to(h[7:8, :], (8, LRU_BLOCK))

    lax.fori_loop(0, seq // 8, body, jnp.zeros((8, LRU_BLOCK), F32), unroll=8)
    y = b_scr[...]
    g = g_ref[...]
    u_ref[...] = (y * (g * jax.nn.sigmoid(g))).astype(u_ref.dtype)
    h_ref[...] = b_scr[seq - 1:seq, :]
    cs_ref[...] = x[seq - (CONV_W - 1):seq, :]


def _lru_weight_specs(l, n_of):
    return [
        pl.BlockSpec((None, CONV_W, LRU_BLOCK), lambda *g: (l, 0, n_of(*g))),
        pl.BlockSpec((None, 1, LRU_BLOCK), lambda *g: (l, 0, n_of(*g))),
        pl.BlockSpec((None, None, LRU_BLOCK, LRU_BLOCK), lambda *g: (l, n_of(*g), 0, 0)),
        pl.BlockSpec((None, 1, LRU_BLOCK), lambda *g: (l, 0, n_of(*g))),
        pl.BlockSpec((None, None, LRU_BLOCK, LRU_BLOCK), lambda *g: (l, n_of(*g), 0, 0)),
        pl.BlockSpec((None, 1, LRU_BLOCK), lambda *g: (l, 0, n_of(*g))),
        pl.BlockSpec((None, 1, LRU_BLOCK), lambda *g: (l, 0, n_of(*g))),
    ]


def _lru_prompt(a6, lru_w, u, l, batch, seq):
    blk = (None, None, seq, LRU_BLOCK)
    n_of = lambda b, n: n
    return pl.pallas_call(
        functools.partial(_lru_prompt_kernel, seq=seq),
        grid=(batch, N_LRU_BLOCKS),
        in_specs=[
            pl.BlockSpec(blk, lambda b, n: (4, b, 0, n)),
            pl.BlockSpec(blk, lambda b, n: (5, b, 0, n)),
        ] + _lru_weight_specs(l, n_of) + [pl.BlockSpec(memory_space=pl.ANY)],
        out_specs=[
            pl.BlockSpec((None, seq, LRU_BLOCK), lambda b, n: (b, 0, N_HEADS + n)),
            pl.BlockSpec((None, 1, LRU_BLOCK), lambda b, n: (b, 0, n)),
            pl.BlockSpec((None, CONV_W - 1, LRU_BLOCK), lambda b, n: (b, 0, n)),
        ],
        out_shape=[
            jax.ShapeDtypeStruct(u.shape, u.dtype),
            jax.ShapeDtypeStruct((batch, 1, D_LRU), F32),
            jax.ShapeDtypeStruct((batch, CONV_W - 1, D_LRU), F32),
        ],
        scratch_shapes=[pltpu.VMEM((seq + 8, LRU_BLOCK), F32), pltpu.VMEM((seq, LRU_BLOCK), F32),
                        pltpu.VMEM((seq, LRU_BLOCK), F32)],
        input_output_aliases={9: 0},
        compiler_params=_params(("arbitrary", "arbitrary"), 40),
        name="lru_prompt",
    )(a6, a6, *lru_w, u)


def _lru_sample_kernel(x_ref, g_ref, sc_ref, h0_ref, wc_ref, bc_ref, wa_ref, ba_ref, wx_ref, bx_ref, lam_ref,
                       u_any, u_ref, h_ref, cs_ref, xc_scr, a_scr, b_scr, y_scr, *, batch, t_new):
    del u_any
    n_state = CONV_W - 1

    def ext_row(b, i):
        if i < n_state:
            return sc_ref[b, i:i + 1, :]
        return x_ref[b, i - n_state:i - n_state + 1, :]

    for b in range(batch):
        for t in range(t_new):
            acc = bc_ref[...] + ext_row(b, t) * wc_ref[0:1, :]
            for j in range(1, CONV_W):
                acc = acc + ext_row(b, t + j) * wc_ref[j:j + 1, :]
            xc_scr[b * t_new + t:b * t_new + t + 1, :] = acc
    a, bb = _lru_gates(xc_scr[...], wa_ref, ba_ref, wx_ref, bx_ref, lam_ref)
    a_scr[...] = a
    b_scr[...] = bb
    for b in range(batch):
        h = h0_ref[b:b + 1, :]
        for t in range(t_new):
            r = b * t_new + t
            h = a_scr[r:r + 1, :] * h + b_scr[r:r + 1, :]
            y_scr[r:r + 1, :] = h
        h_ref[b:b + 1, :] = h
        for i in range(n_state):
            cs_ref[b, i:i + 1, :] = ext_row(b, t_new + i)
    for b in range(batch):
        g = g_ref[b]
        u_ref[b] = y_scr[b * t_new:(b + 1) * t_new, :] * (g * jax.nn.sigmoid(g))


def _lru_sample(a6, state_conv, state_h, lru_w, u, l, batch, t_new):
    blk = (None, batch, t_new, LRU_BLOCK)
    n_of = lambda n: n
    rows = batch * t_new
    return pl.pallas_call(
        functools.partial(_lru_sample_kernel, batch=batch, t_new=t_new),
        grid=(N_LRU_BLOCKS,),
        in_specs=[
            pl.BlockSpec(blk, lambda n: (4, 0, 0, n)),
            pl.BlockSpec(blk, lambda n: (5, 0, 0, n)),
            pl.BlockSpec((None, batch, CONV_W - 1, LRU_BLOCK), lambda n: (l, 0, 0, n)),
            pl.BlockSpec((None, batch, LRU_BLOCK), lambda n: (l, 0, n)),
        ] + _lru_weight_specs(l, n_of) + [pl.BlockSpec(memory_space=pl.ANY)],
        out_specs=[
            pl.BlockSpec((batch, t_new, LRU_BLOCK), lambda n: (0, 0, N_HEADS + n)),
            pl.BlockSpec((batch, LRU_BLOCK), lambda n: (0, n)),
            pl.BlockSpec((batch, CONV_W - 1, LRU_BLOCK), lambda n: (0, 0, n)),
        ],
        out_shape=[
            jax.ShapeDtypeStruct(u.shape, u.dtype),
            jax.ShapeDtypeStruct((batch, D_LRU), F32),
            jax.ShapeDtypeStruct((batch, CONV_W - 1, D_LRU), F32),
        ],
        scratch_shapes=[pltpu.VMEM((rows, LRU_BLOCK), F32)] * 4,
        input_output_aliases={11: 0},
        compiler_params=_params(("arbitrary",), 32),
        name="lru_sample",
    )(a6, a6, state_conv, state_h, *lru_w, u)


def _outproj_kernel(u_ref, w_ref, x_ref, gate_ref, gp_ref, y_ref):
    tm = x_ref.shape[0]
    chunk = min(tm, NORM_CHUNK)
    per_row = gate_ref.shape[0] == tm
    y_ref[...] = jnp.dot(u_ref[...].astype(BF16), w_ref[...], preferred_element_type=F32)

    def finish(c, _):
        rows = pl.ds(pl.multiple_of(c * chunk, chunk), chunk)
        y = y_ref[rows, :]
        y = y * lax.rsqrt(jnp.mean(y * y, axis=-1, keepdims=True) + EPS)
        y_ref[rows, :] = x_ref[rows, :] + gate_ref[rows if per_row else slice(None), :] * (y * gp_ref[...])
        return 0

    lax.fori_loop(0, tm // chunk, finish, 0)


def _outproj(u2, w_out, x2, gate, mod_idx, g_post, l, *, tm, rows_per_mod, vmem_mib):
    m_rows = x2.shape[0]
    if mod_idx is None:
        gate_spec = pl.BlockSpec((tm, D_MODEL), lambda i: (i, 0))
    else:
        gate_spec = pl.BlockSpec((None, 1, D_MODEL), lambda i: (mod_idx(i // rows_per_mod, 2), 0, 0))
    return pl.pallas_call(
        _outproj_kernel,
        grid=(m_rows // tm,),
        in_specs=[
            pl.BlockSpec((tm, D_MODEL), lambda i: (i, 0)),
            pl.BlockSpec((None, D_MODEL, D_MODEL), lambda i: (l, 0, 0), pipeline_mode=pl.Buffered(1)),
            pl.BlockSpec((tm, D_MODEL), lambda i: (i, 0)),
            gate_spec,
            pl.BlockSpec((None, 1, D_MODEL), lambda i: (l, 0, 0)),
        ],
        out_specs=pl.BlockSpec((tm, D_MODEL), lambda i: (i, 0)),
        out_shape=jax.ShapeDtypeStruct((m_rows, D_MODEL), F32),
        compiler_params=_params(("arbitrary",), vmem_mib),
        name="outproj",
    )(u2, w_out, x2, gate, g_post)


def kernel(x_prompt, x_sample, cache_k, cache_v, state_h, state_conv, c_prompt, c_sample, rel_table, w_ada, b_ada,
           g_pre, w_in, w_conv, b_conv, w_a, b_a, w_x, b_x, lam, w_out, g_post):
    depth = w_in.shape[0]
    bp, seq, _ = x_prompt.shape
    bs, t_new, _ = x_sample.shape
    c_len = cache_k.shape[2]
    assert bp + bs <= MOD_ROWS and c_len + t_new <= SAMPLE_KEYS and t_new <= 8
    assert seq % (L_BAND * DILATED_GROUPS[-1][1]) == 0 and c_len >= WINDOW_MAX

    c_all = jnp.concatenate([c_prompt, c_sample, jnp.zeros((MOD_ROWS - bp - bs, D_MODEL), F32)], axis=0)
    mod = _ada(c_all, w_ada, b_ada)
    mod_tab = mod.reshape(depth * MOD_ROWS * 3, 1, D_MODEL)
    pbias, sbias = _bias_tables(rel_table, c_len, t_new)

    w_in_b = w_in.astype(BF16)
    w_out_b = w_out.astype(BF16)
    g_pre3 = g_pre.reshape(depth, 1, D_MODEL)
    g_post3 = g_post.reshape(depth, 1, D_MODEL)

    xp = x_prompt.reshape(bp * seq, D_MODEL)
    xs = x_sample.reshape(bs * t_new, D_MODEL)
    lru_w = (w_conv, b_conv.reshape(depth, 1, D_LRU), w_a, b_a.reshape(depth, 1, D_LRU),
             w_x, b_x.reshape(depth, 1, D_LRU), lam.reshape(depth, 1, D_LRU))
    kv_p = kv_s = None
    hp_l, cp_l, hs_l, cs_l = [], [], [], []
    tm_p = 512
    for l in range(depth):
        p_idx = lambda b, which, l=l: (l * MOD_ROWS + b) * 3 + which
        a6, kp, vp = _inproj(xp, mod_tab, mod_tab, p_idx, g_pre3, w_in_b, l, kv_p,
                             tm=tm_p, tn=1024, rows_per_mod=seq // tm_p, vmem_mib=56)
        kv_p = (kp, vp)
        a6 = a6.reshape(N_SEG, bp, seq, D_ATT)
        u = _attn_prompt(a6, pbias, bp, seq)
        u, h_last, conv = _lru_prompt(a6, lru_w, u, l, bp, seq)
        xp = _outproj(u.reshape(bp * seq, D_MODEL), w_out_b, xp, mod_tab, p_idx, g_post3, l,
                      tm=256, rows_per_mod=seq // 256, vmem_mib=58)
        hp_l.append(h_last.reshape(bp, D_LRU))
        cp_l.append(conv)
        mod_s = jnp.repeat(mod[l, bp:bp + bs], t_new, axis=0)
        rows = bs * t_new
        a6, ks, vs = _inproj(xs, mod_s[:, :D_MODEL], mod_s[:, D_MODEL:2 * D_MODEL], None, g_pre3, w_in_b, l, kv_s,
                             tm=rows, tn=1024, rows_per_mod=1, vmem_mib=40)
        kv_s = (ks, vs)
        a6 = a6.reshape(N_SEG, bs, t_new, D_ATT)
        u = _attn_sample(a6, cache_k, cache_v, sbias, l, bs, t_new, c_len)
        u, h_last, conv = _lru_sample(a6, state_conv, state_h, lru_w, u, l, bs, t_new)
        xs = _outproj(u.reshape(rows, D_MODEL), w_out_b, xs, mod_s[:, 2 * D_MODEL:], None, g_post3, l,
                      tm=rows, rows_per_mod=1, vmem_mib=48)
        hs_l.append(h_last)
        cs_l.append(conv)

    kp, vp = kv_p
    ks, vs = kv_s
    return (xp.reshape(bp, seq, D_MODEL), xs.reshape(bs, t_new, D_MODEL),
            kp.reshape(depth, bp, seq, N_HEADS, HEAD_DIM), vp.reshape(depth, bp, seq, N_HEADS, HEAD_DIM),
            jnp.stack(hp_l), jnp.stack(cp_l),
            ks.reshape(depth, bs, t_new, N_HEADS, HEAD_DIM), vs.reshape(depth, bs, t_new, N_HEADS, HEAD_DIM),
            jnp.stack(hs_l), jnp.stack(cs_l))
```

```python
import functools
import math

import numpy as np
import jax
import jax.numpy as jnp
from jax import lax
from jax.experimental import pallas as pl
from jax.experimental.pallas import tpu as pltpu

D_MODEL = 4096
D_ATT = 2048
D_LRU = 2048
HEAD_DIM = 128
N_HEADS = 16
N_LRU_BLOCKS = 16
LRU_BLOCK = 128
CONV_W = 4
LRU_C = 8.0
DILATED_GROUPS = ((128, 1), (512, 4), (2048, 16))
N_GROUPS = len(DILATED_GROUPS)
WINDOW_MAX = 2048
N_BUCKETS = 32
MAX_EXACT = N_BUCKETS // 2
MAX_DISTANCE = WINDOW_MAX
EPS = 1e-6
ATT_SCALE = HEAD_DIM ** -0.5
D_IN = 4 * D_ATT + 2 * D_LRU
N_SEG = D_IN // D_ATT
SEG_K, SEG_V = 1, 2
L_BAND = 128
MASKED = -1e30
MOD_ROWS = 16
SAMPLE_KEYS = 2048 + 128
MIB = 1024 * 1024
NORM_CHUNK = 64
KV_HEADS = 8

F32 = jnp.float32
BF16 = jnp.bfloat16


def _params(semantics, vmem_mib):
    return pltpu.CompilerParams(dimension_semantics=semantics, vmem_limit_bytes=vmem_mib * MIB)


def _rel_bucket_np(dist):
    d = dist.astype(np.float32)
    large = np.float32(MAX_EXACT) + np.log(np.maximum(d, np.float32(1.0)) / np.float32(MAX_EXACT)) / np.float32(
        math.log(MAX_DISTANCE / MAX_EXACT)) * np.float32(N_BUCKETS - MAX_EXACT)
    large = np.minimum(large.astype(np.int32), N_BUCKETS - 1)
    return np.where(dist < MAX_EXACT, dist, large).astype(np.int32)


def _prompt_bucket_index():
    qi = np.arange(L_BAND)[:, None]
    kj = np.arange(2 * L_BAND)[None, :]
    dist = qi + L_BAND - kj
    band = (dist >= 0) & (dist <= L_BAND)
    out = []
    for _, dil in DILATED_GROUPS:
        b = _rel_bucket_np(np.clip(dist, 0, L_BAND) * dil)
        out.append(np.where(band, b, -1))
    return np.stack(out).astype(np.int32)


def _sample_bucket_index(c_len, t_new):
    t = np.arange(8)[:, None]
    idx = np.arange(SAMPLE_KEYS)[None, :]
    delta = c_len + t - idx
    out = []
    for window, dil in DILATED_GROUPS:
        valid = (delta >= 0) & (delta % dil == 0) & (delta <= window) & (idx < c_len + t_new) & (t < t_new)
        b = _rel_bucket_np(np.clip(delta, 0, window))
        b = np.where(valid, b, -1)
        b = np.where(t >= t_new, 0, b)
        out.append(b)
    return np.stack(out).astype(np.int32)


def _ada_kernel(c_ref, w_ref, b_ref, o_ref):
    c = c_ref[...]
    a = (c * jax.nn.sigmoid(c)).astype(BF16)
    o_ref[...] = jnp.dot(a, w_ref[...].astype(BF16), preferred_element_type=F32) + b_ref[...]


def _ada(c_all, w_ada, b_ada):
    depth = w_ada.shape[0]
    tn = 512
    return pl.pallas_call(
        _ada_kernel,
        grid=(depth, 3 * D_MODEL // tn),
        in_specs=[
            pl.BlockSpec((MOD_ROWS, D_MODEL), lambda l, j: (0, 0)),
            pl.BlockSpec((None, D_MODEL, tn), lambda l, j: (l, 0, j)),
            pl.BlockSpec((None, 1, tn), lambda l, j: (l, 0, j)),
        ],
        out_specs=pl.BlockSpec((None, MOD_ROWS, tn), lambda l, j: (l, 0, j)),
        out_shape=jax.ShapeDtypeStruct((depth, MOD_ROWS, 3 * D_MODEL), F32),
        compiler_params=_params(("arbitrary", "arbitrary"), 40),
        name="ada",
    )(c_all, w_ada, b_ada.reshape(depth, 1, 3 * D_MODEL))


def _bias_kernel(tab_ref, pidx_ref, sidx_ref, pb_ref, sb_ref):
    h = pl.program_id(0)
    for idx_ref, out_ref in ((pidx_ref, pb_ref), (sidx_ref, sb_ref)):
        for g in range(N_GROUPS):
            idx = idx_ref[g]
            acc = jnp.full(idx.shape, MASKED, F32)
            for b in range(N_BUCKETS):
                acc = jnp.where(idx == b, tab_ref[b, h], acc)
            out_ref[g] = acc


def _bias_tables(rel_table, c_len, t_new):
    pidx = jnp.asarray(_prompt_bucket_index())
    sidx = jnp.asarray(_sample_bucket_index(c_len, t_new))
    return pl.pallas_call(
        _bias_kernel,
        grid=(N_HEADS,),
        in_specs=[
            pl.BlockSpec(memory_space=pltpu.SMEM),
            pl.BlockSpec((N_GROUPS, L_BAND, 2 * L_BAND), lambda h: (0, 0, 0)),
            pl.BlockSpec((N_GROUPS, 8, SAMPLE_KEYS), lambda h: (0, 0, 0)),
        ],
        out_specs=[
            pl.BlockSpec((N_GROUPS, None, L_BAND, 2 * L_BAND), lambda h: (0, h, 0, 0)),
            pl.BlockSpec((N_GROUPS, None, 8, SAMPLE_KEYS), lambda h: (0, h, 0, 0)),
        ],
        out_shape=[
            jax.ShapeDtypeStruct((N_GROUPS, N_HEADS, L_BAND, 2 * L_BAND), F32),
            jax.ShapeDtypeStruct((N_GROUPS, N_HEADS, 8, SAMPLE_KEYS), F32),
        ],
        compiler_params=_params(("arbitrary",), 32),
        name="bias_tables",
    )(rel_table, pidx, sidx)


def _mod_specs(mod_idx, which, tm, rows_per_mod):
    if mod_idx is None:
        return pl.BlockSpec((tm, D_MODEL), lambda i, *_: (i, 0))
    return pl.BlockSpec((None, 1, D_MODEL), lambda i, *_: (mod_idx(i // rows_per_mod, which), 0, 0))


def _prenorm_kernel(x_ref, shift_ref, scale_ref, g_ref, h_ref):
    tm = x_ref.shape[0]
    chunk = min(tm, NORM_CHUNK)
    per_row = shift_ref.shape[0] == tm

    def norm(c, _):
        rows = pl.ds(pl.multiple_of(c * chunk, chunk), chunk)
        mrows = rows if per_row else slice(None)
        x = x_ref[rows, :]
        y = x * lax.rsqrt(jnp.mean(x * x, axis=-1, keepdims=True) + EPS)
        y = y * g_ref[...]
        h_ref[rows, :] = (y * (1.0 + scale_ref[mrows, :]) + shift_ref[mrows, :]).astype(h_ref.dtype)
        return 0

    lax.fori_loop(0, tm // chunk, norm, 0)


def _prenorm(x2, shift, scale, mod_idx, g_pre, l, *, tm, rows_per_mod):
    m_rows = x2.shape[0]
    return pl.pallas_call(
        _prenorm_kernel,
        grid=(m_rows // tm,),
        in_specs=[
            pl.BlockSpec((tm, D_MODEL), lambda i: (i, 0)),
            _mod_specs(mod_idx, 0, tm, rows_per_mod),
            _mod_specs(mod_idx, 1, tm, rows_per_mod),
            pl.BlockSpec((None, 1, D_MODEL), lambda i: (l, 0, 0)),
        ],
        out_specs=pl.BlockSpec((tm, D_MODEL), lambda i: (i, 0)),
        out_shape=jax.ShapeDtypeStruct((m_rows, D_MODEL), BF16),
        compiler_params=_params(("arbitrary",), 40),
        name="prenorm",
    )(x2, shift, scale, g_pre)


def _inproj_kernel(h_ref, w_ref, *rest, nj_seg, layer):
    a_ref, k_hbm, v_hbm, stage, sem = rest[-5:]
    i, j = pl.program_id(0), pl.program_id(1)
    tm = h_ref.shape[0]
    seg, part = j // nj_seg, j % nj_seg
    first_kv, last_kv = SEG_K * nj_seg, (SEG_V + 1) * nj_seg - 1

    def head_copy(dst_hbm, group):
        rows = pl.ds(i * tm, tm)
        heads = pl.ds(group * KV_HEADS, KV_HEADS)
        return pltpu.make_async_copy(stage, dst_hbm.at[layer, rows, heads, :], sem.at[0])

    acc = jnp.dot(h_ref[...], w_ref[...], preferred_element_type=F32)
    a_ref[...] = acc

    @pl.when((j > first_kv) & (j <= last_kv + 1))
    def _():
        head_copy(k_hbm, 0).wait()

    def stage_and_send(dst_hbm):
        flat = stage.reshape(tm * KV_HEADS, HEAD_DIM)
        for hh in range(KV_HEADS):
            flat[pl.ds(hh, tm, stride=KV_HEADS), :] = acc[:, hh * HEAD_DIM:(hh + 1) * HEAD_DIM]
        head_copy(dst_hbm, part).start()

    @pl.when(seg == SEG_K)
    def _():
        stage_and_send(k_hbm)

    @pl.when(seg == SEG_V)
    def _():
        stage_and_send(v_hbm)


def _inproj(h2, w_in, l, kv_prev, *, tm, vmem_mib):
    m_rows = h2.shape[0]
    depth = w_in.shape[0]
    tn = KV_HEADS * HEAD_DIM
    nj_seg = D_ATT // tn
    in_specs = [
        pl.BlockSpec((tm, D_MODEL), lambda i, j: (i, 0)),
        pl.BlockSpec((None, D_MODEL, tn), lambda i, j: (l, 0, j)),
    ]
    args = [h2, w_in]
    aliases = {}
    if kv_prev is not None:
        in_specs += [pl.BlockSpec(memory_space=pl.ANY)] * 2
        args += list(kv_prev)
        aliases = {2: 1, 3: 2}
    kv_shape = jax.ShapeDtypeStruct((depth, m_rows, N_HEADS, HEAD_DIM), F32)
    return pl.pallas_call(
        functools.partial(_inproj_kernel, nj_seg=nj_seg, layer=l),
        grid=(m_rows // tm, N_SEG * nj_seg),
        in_specs=in_specs,
        out_specs=[
            pl.BlockSpec((None, tm, tn), lambda i, j: (j // nj_seg, i, j % nj_seg)),
            pl.BlockSpec(memory_space=pl.ANY),
            pl.BlockSpec(memory_space=pl.ANY),
        ],
        out_shape=[jax.ShapeDtypeStruct((N_SEG, m_rows, D_ATT), F32), kv_shape, kv_shape],
        scratch_shapes=[pltpu.VMEM((tm, KV_HEADS, HEAD_DIM), F32), pltpu.SemaphoreType.DMA((1,))],
        input_output_aliases=aliases,
        compiler_params=_params(("arbitrary", "arbitrary"), vmem_mib),
        name="inproj",
    )(*args)


def _attn_prompt_kernel(q_ref, k_ref, v_ref, g_ref, bias_ref, u_ref, qa, ka, va, qb, kb, vb, o_scr, e_scr, *, seq):
    n_blk = seq // L_BAND

    def regroup(dst, src, stride):
        sub = seq // stride
        for r in range(stride):
            dst[r * sub:(r + 1) * sub, :] = src[pl.ds(r, sub, stride=stride), :]

    def branch(gi, dil, q_src, k_src, v_src):
        per_res = n_blk // dil
        q3 = q_src[...].astype(BF16).reshape(n_blk, L_BAND, HEAD_DIM)
        k3 = k_src[...].astype(BF16).reshape(n_blk, L_BAND, HEAD_DIM)
        v3 = v_src[...].astype(BF16).reshape(n_blk, L_BAND, HEAD_DIM)
        if per_res > 1:
            kk = jnp.concatenate([jnp.concatenate([k3[:1], k3[:-1]], axis=0), k3], axis=1)
            vv = jnp.concatenate([jnp.concatenate([v3[:1], v3[:-1]], axis=0), v3], axis=1)
            blk = lax.broadcasted_iota(jnp.int32, (n_blk, 1, 2 * L_BAND), 0)
            col = lax.broadcasted_iota(jnp.int32, (n_blk, 1, 2 * L_BAND), 2)
            first = jnp.where((blk % per_res == 0) & (col < L_BAND), MASKED, 0.0)
            bias = bias_ref[gi][None] + first
        else:
            kk, vv = k3, v3
            bias = bias_ref[gi, :, L_BAND:][None]
        s = jnp.einsum('bqd,bkd->bqk', q3, kk, preferred_element_type=F32) * ATT_SCALE + bias
        mx = jnp.max(s, axis=-1, keepdims=True)
        p = jnp.exp(s - mx)
        den = jnp.sum(p, axis=-1, keepdims=True)
        o = jnp.einsum('bqk,bkd->bqd', p.astype(BF16), vv, preferred_element_type=F32) / den
        lse = jnp.broadcast_to(mx + jnp.log(den), (n_blk, L_BAND, HEAD_DIM))
        o = o.reshape(seq, HEAD_DIM)
        lse = lse.reshape(seq, HEAD_DIM)
        if dil == 1:
            o_scr[gi] = o
            e_scr[gi] = lse
        else:
            sub = seq // dil
            for r in range(dil):
                o_scr[gi, pl.ds(r, sub, stride=dil), :] = o[r * sub:(r + 1) * sub, :]
                e_scr[gi, pl.ds(r, sub, stride=dil), :] = lse[r * sub:(r + 1) * sub, :]

    (_, d0), (_, d1), (_, d2) = DILATED_GROUPS
    assert d0 == 1 and d2 == d1 * d1
    branch(0, d0, q_ref, k_ref, v_ref)
    for dst, src in ((qa, q_ref), (ka, k_ref), (va, v_ref)):
        regroup(dst, src, d1)
    branch(1, d1, qa, ka, va)
    for dst, src in ((qb, qa), (kb, ka), (vb, va)):
        regroup(dst, src, d1)
    branch(2, d2, qb, kb, vb)

    chunk = 256

    def combine(c, _):
        rows = pl.ds(pl.multiple_of(c * chunk, chunk), chunk)
        es = [e_scr[gi, rows, :] for gi in range(N_GROUPS)]
        top = functools.reduce(jnp.maximum, es)
        ws = [jnp.exp(e - top) for e in es]
        num = sum(w * o_scr[gi, rows, :] for gi, w in enumerate(ws))
        den = sum(ws)
        g = g_ref[rows, :]
        u_ref[rows, :] = ((num / den) * (g * jax.nn.sigmoid(g))).astype(u_ref.dtype)
        return 0

    lax.fori_loop(0, seq // chunk, combine, 0)


def _attn_prompt(a6, pbias, batch, seq):
    blk = (None, None, seq, HEAD_DIM)
    return pl.pallas_call(
        functools.partial(_attn_prompt_kernel, seq=seq),
        grid=(batch, N_HEADS),
        in_specs=[pl.BlockSpec(blk, lambda b, h, slot=slot: (slot, b, 0, h)) for slot in range(4)] + [
            pl.BlockSpec((N_GROUPS, None, L_BAND, 2 * L_BAND), lambda b, h: (0, h, 0, 0)),
        ],
        out_specs=pl.BlockSpec((None, seq, HEAD_DIM), lambda b, h: (b, 0, h)),
        out_shape=jax.ShapeDtypeStruct((batch, seq, D_ATT), BF16),
        scratch_shapes=[pltpu.VMEM((seq, HEAD_DIM), F32)] * 6 + [pltpu.VMEM((N_GROUPS, seq, HEAD_DIM), F32)] * 2,
        compiler_params=_params(("arbitrary", "arbitrary"), 48),
        name="attn_prompt",
    )(a6, a6, a6, a6, pbias)


def _attn_sample_kernel(q_ref, kn_ref, vn_ref, ck_ref, cv_ref, g_ref, bias_ref, u_ref, q_scr, k_scr, v_scr,
                        *, c_len, t_new):
    pad = SAMPLE_KEYS - c_len - t_new
    q_scr[t_new:8, :] = jnp.zeros((8 - t_new, HEAD_DIM), F32)
    for slot in range(2):
        k_scr[slot, c_len + t_new:SAMPLE_KEYS, :] = jnp.zeros((pad, HEAD_DIM), F32)
        v_scr[slot, c_len + t_new:SAMPLE_KEYS, :] = jnp.zeros((pad, HEAD_DIM), F32)
    for hh in range(KV_HEADS):
        lanes = slice(hh * HEAD_DIM, (hh + 1) * HEAD_DIM)
        slot = hh % 2
        q_scr[0:t_new, :] = q_ref[:, lanes]
        for new_ref, cache_ref, scr in ((kn_ref, ck_ref, k_scr), (vn_ref, cv_ref, v_scr)):
            heads = cache_ref.reshape(c_len * KV_HEADS, HEAD_DIM)
            scr[slot, 0:c_len, :] = heads[pl.ds(hh, c_len, stride=KV_HEADS), :]
            scr[slot, c_len:c_len + t_new, :] = new_ref[:, lanes]
        kk = k_scr[slot].astype(BF16)
        vv = v_scr[slot].astype(BF16)
        s = lax.dot_general(q_scr[...].astype(BF16), kk, (((1,), (1,)), ((), ())),
                            preferred_element_type=F32) * ATT_SCALE
        ms, ls, os_ = [], [], []
        for gi in range(N_GROUPS):
            sg = s + bias_ref[gi, hh]
            mx = jnp.max(sg, axis=-1, keepdims=True)
            p = jnp.exp(sg - mx)
            ms.append(mx)
            ls.append(jnp.sum(p, axis=-1, keepdims=True))
            os_.append(jnp.dot(p.astype(BF16), vv, preferred_element_type=F32))
        top = functools.reduce(jnp.maximum, ms)
        ws = [jnp.exp(m - top) for m in ms]
        num = sum(w * o for w, o in zip(ws, os_))
        den = sum(w * d for w, d in zip(ws, ls))
        g = g_ref[:, lanes]
        u_ref[:, lanes] = (num / den)[0:t_new, :] * (g * jax.nn.sigmoid(g))


def _attn_sample(a6, cache_k, cache_v, sbias, l, batch, t_new, c_len):
    new_blk = (None, None, t_new, KV_HEADS * HEAD_DIM)
    cache_blk = (None, None, c_len, KV_HEADS, HEAD_DIM)
    return pl.pallas_call(
        functools.partial(_attn_sample_kernel, c_len=c_len, t_new=t_new),
        grid=(batch, N_HEADS // KV_HEADS),
        in_specs=[
            pl.BlockSpec(new_blk, lambda b, hg: (0, b, 0, hg)),
            pl.BlockSpec(new_blk, lambda b, hg: (1, b, 0, hg)),
            pl.BlockSpec(new_blk, lambda b, hg: (2, b, 0, hg)),
            pl.BlockSpec(cache_blk, lambda b, hg: (l, b, 0, hg, 0)),
            pl.BlockSpec(cache_blk, lambda b, hg: (l, b, 0, hg, 0)),
            pl.BlockSpec(new_blk, lambda b, hg: (3, b, 0, hg)),
            pl.BlockSpec((N_GROUPS, KV_HEADS, 8, SAMPLE_KEYS), lambda b, hg: (0, hg, 0, 0)),
        ],
        out_specs=pl.BlockSpec((None, t_new, KV_HEADS * HEAD_DIM), lambda b, hg: (b, 0, hg)),
        out_shape=jax.ShapeDtypeStruct((batch, t_new, D_ATT), F32),
        scratch_shapes=[pltpu.VMEM((8, HEAD_DIM), F32), pltpu.VMEM((2, SAMPLE_KEYS, HEAD_DIM), F32),
                        pltpu.VMEM((2, SAMPLE_KEYS, HEAD_DIM), F32)],
        compiler_params=_params(("arbitrary", "arbitrary"), 48),
        name="attn_sample",
    )(a6, a6, a6, cache_k, cache_v, a6, sbias)


def _lru_gates(xc, wa_ref, ba_ref, wx_ref, bx_ref, lam_ref):
    xcb = xc.astype(BF16)
    r = jax.nn.sigmoid(jnp.dot(xcb, wa_ref[...].astype(BF16), preferred_element_type=F32) + ba_ref[...])
    i = jax.nn.sigmoid(jnp.dot(xcb, wx_ref[...].astype(BF16), preferred_element_type=F32) + bx_ref[...])
    nl = -lam_ref[...]
    softplus = jnp.maximum(nl, 0.0) + jnp.log1p(jnp.exp(-jnp.abs(nl)))
    log_a = -LRU_C * r * softplus
    a = jnp.exp(log_a)
    t = jnp.tanh(log_a)
    b = jnp.sqrt(-2.0 * t / (1.0 - t)) * (i * xc)
    return a, b


def _lru_prompt_kernel(x_ref, g_ref, wc_ref, bc_ref, wa_ref, ba_ref, wx_ref, bx_ref, lam_ref,
                       u_ref, h_ref, cs_ref, xpad, a_scr, b_scr, *, seq):
    x = x_ref[...]
    xpad[0:8, :] = jnp.zeros((8, LRU_BLOCK), F32)
    xpad[8:8 + seq, :] = x
    xc = bc_ref[...] + xpad[pl.ds(8 - 3, seq), :] * wc_ref[0:1, :]
    xc = xc + xpad[pl.ds(8 - 2, seq), :] * wc_ref[1:2, :]
    xc = xc + xpad[pl.ds(8 - 1, seq), :] * wc_ref[2:3, :]
    xc = xc + x * wc_ref[3:4, :]
    a, b = _lru_gates(xc, wa_ref, ba_ref, wx_ref, bx_ref, lam_ref)
    a_scr[...] = a
    b_scr[...] = b

    row = lax.broadcasted_iota(jnp.int32, (8, LRU_BLOCK), 0)

    def body(c, h_prev):
        rows = pl.ds(pl.multiple_of(c * 8, 8), 8)
        ca = a_scr[rows, :]
        cb = b_scr[rows, :]
        for k in (1, 2, 4):
            a_sh = jnp.where(row >= k, pltpu.roll(ca, k, 0), 1.0)
            b_sh = jnp.where(row >= k, pltpu.roll(cb, k, 0), 0.0)
            cb = ca * b_sh + cb
            ca = ca * a_sh
        h = ca * h_prev + cb
        b_scr[rows, :] = h
        return jnp.broadcast_to(h[7:8, :], (8, LRU_BLOCK))

    lax.fori_loop(0, seq // 8, body, jnp.zeros((8, LRU_BLOCK), F32), unroll=8)
    y = b_scr[...]
    g = g_ref[...]
    u_ref[...] = (y * (g * jax.nn.sigmoid(g))).astype(u_ref.dtype)
    h_ref[...] = b_scr[seq - 1:seq, :]
    cs_ref[...] = x[seq - (CONV_W - 1):seq, :]


def _lru_weight_specs(l, n_of):
    return [
        pl.BlockSpec((None, CONV_W, LRU_BLOCK), lambda *g: (l, 0, n_of(*g))),
        pl.BlockSpec((None, 1, LRU_BLOCK), lambda *g: (l, 0, n_of(*g))),
        pl.BlockSpec((None, None, LRU_BLOCK, LRU_BLOCK), lambda *g: (l, n_of(*g), 0, 0)),
        pl.BlockSpec((None, 1, LRU_BLOCK), lambda *g: (l, 0, n_of(*g))),
        pl.BlockSpec((None, None, LRU_BLOCK, LRU_BLOCK), lambda *g: (l, n_of(*g), 0, 0)),
        pl.BlockSpec((None, 1, LRU_BLOCK), lambda *g: (l, 0, n_of(*g))),
        pl.BlockSpec((None, 1, LRU_BLOCK), lambda *g: (l, 0, n_of(*g))),
    ]


def _lru_prompt(a6, lru_w, l, batch, seq):
    blk = (None, None, seq, LRU_BLOCK)
    n_of = lambda b, n: n
    return pl.pallas_call(
        functools.partial(_lru_prompt_kernel, seq=seq),
        grid=(batch, N_LRU_BLOCKS),
        in_specs=[
            pl.BlockSpec(blk, lambda b, n: (4, b, 0, n)),
            pl.BlockSpec(blk, lambda b, n: (5, b, 0, n)),
        ] + _lru_weight_specs(l, n_of),
        out_specs=[
            pl.BlockSpec((None, seq, LRU_BLOCK), lambda b, n: (b, 0, n)),
            pl.BlockSpec((None, 1, LRU_BLOCK), lambda b, n: (b, 0, n)),
            pl.BlockSpec((None, CONV_W - 1, LRU_BLOCK), lambda b, n: (b, 0, n)),
        ],
        out_shape=[
            jax.ShapeDtypeStruct((batch, seq, D_LRU), BF16),
            jax.ShapeDtypeStruct((batch, 1, D_LRU), F32),
            jax.ShapeDtypeStruct((batch, CONV_W - 1, D_LRU), F32),
        ],
        scratch_shapes=[pltpu.VMEM((seq + 8, LRU_BLOCK), F32), pltpu.VMEM((seq, LRU_BLOCK), F32),
                        pltpu.VMEM((seq, LRU_BLOCK), F32)],
        compiler_params=_params(("arbitrary", "arbitrary"), 40),
        name="lru_prompt",
    )(a6, a6, *lru_w)


def _lru_sample_kernel(x_ref, g_ref, sc_ref, h0_ref, wc_ref, bc_ref, wa_ref, ba_ref, wx_ref, bx_ref, lam_ref,
                       u_ref, h_ref, cs_ref, xc_scr, a_scr, b_scr, y_scr, *, batch, t_new):
    n_state = CONV_W - 1

    def ext_row(b, i):
        if i < n_state:
            return sc_ref[b, i:i + 1, :]
        return x_ref[b, i - n_state:i - n_state + 1, :]

    for b in range(batch):
        for t in range(t_new):
            acc = bc_ref[...] + ext_row(b, t) * wc_ref[0:1, :]
            for j in range(1, CONV_W):
                acc = acc + ext_row(b, t + j) * wc_ref[j:j + 1, :]
            xc_scr[b * t_new + t:b * t_new + t + 1, :] = acc
    a, bb = _lru_gates(xc_scr[...], wa_ref, ba_ref, wx_ref, bx_ref, lam_ref)
    a_scr[...] = a
    b_scr[...] = bb
    for b in range(batch):
        h = h0_ref[b:b + 1, :]
        for t in range(t_new):
            r = b * t_new + t
            h = a_scr[r:r + 1, :] * h + b_scr[r:r + 1, :]
            y_scr[r:r + 1, :] = h
        h_ref[b:b + 1, :] = h
        for i in range(n_state):
            cs_ref[b, i:i + 1, :] = ext_row(b, t_new + i)
    for b in range(batch):
        g = g_ref[b]
        u_ref[b] = y_scr[b * t_new:(b + 1) * t_new, :] * (g * jax.nn.sigmoid(g))


def _lru_sample(a6, state_conv, state_h, lru_w, l, batch, t_new):
    blk = (None, batch, t_new, LRU_BLOCK)
    n_of = lambda n: n
    rows = batch * t_new
    return pl.pallas_call(
        functools.partial(_lru_sample_kernel, batch=batch, t_new=t_new),
        grid=(N_LRU_BLOCKS,),
        in_specs=[
            pl.BlockSpec(blk, lambda n: (4, 0, 0, n)),
            pl.BlockSpec(blk, lambda n: (5, 0, 0, n)),
            pl.BlockSpec((None, batch, CONV_W - 1, LRU_BLOCK), lambda n: (l, 0, 0, n)),
            pl.BlockSpec((None, batch, LRU_BLOCK), lambda n: (l, 0, n)),
        ] + _lru_weight_specs(l, n_of),
        out_specs=[
            pl.BlockSpec((batch, t_new, LRU_BLOCK), lambda n: (0, 0, n)),
            pl.BlockSpec((batch, LRU_BLOCK), lambda n: (0, n)),
            pl.BlockSpec((batch, CONV_W - 1, LRU_BLOCK), lambda n: (0, 0, n)),
        ],
        out_shape=[
            jax.ShapeDtypeStruct((batch, t_new, D_LRU), F32),
            jax.ShapeDtypeStruct((batch, D_LRU), F32),
            jax.ShapeDtypeStruct((batch, CONV_W - 1, D_LRU), F32),
        ],
        scratch_shapes=[pltpu.VMEM((rows, LRU_BLOCK), F32)] * 4,
        compiler_params=_params(("arbitrary",), 32),
        name="lru_sample",
    )(a6, a6, state_conv, state_h, *lru_w)


def _outproj_kernel(ua_ref, ul_ref, w_ref, x_ref, gate_ref, gp_ref, y_ref):
    tm = x_ref.shape[0]
    chunk = min(tm, NORM_CHUNK)
    per_row = gate_ref.shape[0] == tm
    y_ref[...] = (jnp.dot(ua_ref[...].astype(BF16), w_ref[0:D_ATT, :], preferred_element_type=F32)
                  + jnp.dot(ul_ref[...].astype(BF16), w_ref[D_ATT:D_MODEL, :], preferred_element_type=F32))

    def finish(c, _):
        rows = pl.ds(pl.multiple_of(c * chunk, chunk), chunk)
        y = y_ref[rows, :]
        y = y * lax.rsqrt(jnp.mean(y * y, axis=-1, keepdims=True) + EPS)
        y_ref[rows, :] = x_ref[rows, :] + gate_ref[rows if per_row else slice(None), :] * (y * gp_ref[...])
        return 0

    lax.fori_loop(0, tm // chunk, finish, 0)


def _outproj(u_att, u_lru, w_out, x2, gate, mod_idx, g_post, l, *, tm, rows_per_mod, vmem_mib):
    m_rows = x2.shape[0]
    return pl.pallas_call(
        _outproj_kernel,
        grid=(m_rows // tm,),
        in_specs=[
            pl.BlockSpec((tm, D_ATT), lambda i: (i, 0)),
            pl.BlockSpec((tm, D_LRU), lambda i: (i, 0)),
            pl.BlockSpec((None, D_MODEL, D_MODEL), lambda i: (l, 0, 0), pipeline_mode=pl.Buffered(1)),
            pl.BlockSpec((tm, D_MODEL), lambda i: (i, 0)),
            _mod_specs(mod_idx, 2, tm, rows_per_mod),
            pl.BlockSpec((None, 1, D_MODEL), lambda i: (l, 0, 0)),
        ],
        out_specs=pl.BlockSpec((tm, D_MODEL), lambda i: (i, 0)),
        out_shape=jax.ShapeDtypeStruct((m_rows, D_MODEL), F32),
        compiler_params=_params(("arbitrary",), vmem_mib),
        name="outproj",
    )(u_att, u_lru, w_out, x2, gate, g_post)


def kernel(x_prompt, x_sample, cache_k, cache_v, state_h, state_conv, c_prompt, c_sample, rel_table, w_ada, b_ada,
           g_pre, w_in, w_conv, b_conv, w_a, b_a, w_x, b_x, lam, w_out, g_post):
    depth = w_in.shape[0]
    bp, seq, _ = x_prompt.shape
    bs, t_new, _ = x_sample.shape
    c_len = cache_k.shape[2]
    assert bp + bs <= MOD_ROWS and c_len + t_new <= SAMPLE_KEYS and t_new <= 8
    assert seq % (L_BAND * DILATED_GROUPS[-1][1]) == 0 and c_len >= WINDOW_MAX

    c_all = jnp.concatenate([c_prompt, c_sample, jnp.zeros((MOD_ROWS - bp - bs, D_MODEL), F32)], axis=0)
    mod = _ada(c_all, w_ada, b_ada)
    mod_tab = mod.reshape(depth * MOD_ROWS * 3, 1, D_MODEL)
    pbias, sbias = _bias_tables(rel_table, c_len, t_new)

    w_in_b = w_in.astype(BF16)
    w_out_b = w_out.astype(BF16)
    g_pre3 = g_pre.reshape(depth, 1, D_MODEL)
    g_post3 = g_post.reshape(depth, 1, D_MODEL)

    xp = x_prompt.reshape(bp * seq, D_MODEL)
    xs = x_sample.reshape(bs * t_new, D_MODEL)
    rows_s = bs * t_new
    lru_w = (w_conv, b_conv.reshape(depth, 1, D_LRU), w_a, b_a.reshape(depth, 1, D_LRU),
             w_x, b_x.reshape(depth, 1, D_LRU), lam.reshape(depth, 1, D_LRU))
    kv_p = kv_s = None
    hp_l, cp_l, hs_l, cs_l = [], [], [], []
    tm_norm, tm_in, tm_out = 512, 1024, 256
    for l in range(depth):
        p_idx = lambda b, which, l=l: (l * MOD_ROWS + b) * 3 + which
        hp = _prenorm(xp, mod_tab, mod_tab, p_idx, g_pre3, l, tm=tm_norm, rows_per_mod=seq // tm_norm)
        a6, kp, vp = _inproj(hp, w_in_b, l, kv_p, tm=tm_in, vmem_mib=56)
        kv_p = (kp, vp)
        a6 = a6.reshape(N_SEG, bp, seq, D_ATT)
        u_att = _attn_prompt(a6, pbias, bp, seq)
        u_lru, h_last, conv = _lru_prompt(a6, lru_w, l, bp, seq)
        xp = _outproj(u_att.reshape(bp * seq, D_ATT), u_lru.reshape(bp * seq, D_LRU), w_out_b, xp, mod_tab, p_idx,
                      g_post3, l, tm=tm_out, rows_per_mod=seq // tm_out, vmem_mib=58)
        hp_l.append(h_last.reshape(bp, D_LRU))
        cp_l.append(conv)
        mod_s = jnp.repeat(mod[l, bp:bp + bs], t_new, axis=0)
        hs = _prenorm(xs, mod_s[:, :D_MODEL], mod_s[:, D_MODEL:2 * D_MODEL], None, g_pre3, l,
                      tm=rows_s, rows_per_mod=1)
        a6, ks, vs = _inproj(hs, w_in_b, l, kv_s, tm=rows_s, vmem_mib=40)
        kv_s = (ks, vs)
        a6 = a6.reshape(N_SEG, bs, t_new, D_ATT)
        u_att = _attn_sample(a6, cache_k, cache_v, sbias, l, bs, t_new, c_len)
        u_lru, h_last, conv = _lru_sample(a6, state_conv, state_h, lru_w, l, bs, t_new)
        xs = _outproj(u_att.reshape(rows_s, D_ATT), u_lru.reshape(rows_s, D_LRU), w_out_b, xs,
                      mod_s[:, 2 * D_MODEL:], None, g_post3, l, tm=rows_s, rows_per_mod=1, vmem_mib=48)
        hs_l.append(h_last)
        cs_l.append(conv)

    kp, vp = kv_p
    ks, vs = kv_s
    return (xp.reshape(bp, seq, D_MODEL), xs.reshape(bs, t_new, D_MODEL),
            kp.reshape(depth, bp, seq, N_HEADS, HEAD_DIM), vp.reshape(depth, bp, seq, N_HEADS, HEAD_DIM),
            jnp.stack(hp_l), jnp.stack(cp_l),
            ks.reshape(depth, bs, t_new, N_HEADS, HEAD_DIM), vs.reshape(depth, bs, t_new, N_HEADS, HEAD_DIM),
            jnp.stack(hs_l), jnp.stack(cs_l))
```

```python
import functools
import math

import numpy as np
import jax
import jax.numpy as jnp
from jax import lax
from jax.experimental import pallas as pl
from jax.experimental.pallas import tpu as pltpu

D_MODEL = 4096
D_ATT = 2048
D_LRU = 2048
HEAD_DIM = 128
N_HEADS = 16
N_LRU_BLOCKS = 16
LRU_BLOCK = 128
CONV_W = 4
LRU_C = 8.0
DILATED_GROUPS = ((128, 1), (512, 4), (2048, 16))
N_GROUPS = len(DILATED_GROUPS)
WINDOW_MAX = 2048
N_BUCKETS = 32
MAX_EXACT = N_BUCKETS // 2
MAX_DISTANCE = WINDOW_MAX
EPS = 1e-6
ATT_SCALE = HEAD_DIM ** -0.5
D_IN = 4 * D_ATT + 2 * D_LRU
N_SEG = D_IN // D_ATT
SEG_K, SEG_V = 1, 2
L_BAND = 128
MASKED = -1e30
MOD_ROWS = 16
SAMPLE_KEYS = 2048 + 128
MIB = 1024 * 1024
NORM_CHUNK = 64
EPILOGUE_CHUNK = 128
KV_HEADS = 8

F32 = jnp.float32
BF16 = jnp.bfloat16


def _params(semantics, vmem_mib):
    return pltpu.CompilerParams(dimension_semantics=semantics, vmem_limit_bytes=vmem_mib * MIB)


def _rel_bucket_np(dist):
    d = dist.astype(np.float32)
    large = np.float32(MAX_EXACT) + np.log(np.maximum(d, np.float32(1.0)) / np.float32(MAX_EXACT)) / np.float32(
        math.log(MAX_DISTANCE / MAX_EXACT)) * np.float32(N_BUCKETS - MAX_EXACT)
    large = np.minimum(large.astype(np.int32), N_BUCKETS - 1)
    return np.where(dist < MAX_EXACT, dist, large).astype(np.int32)


def _prompt_bucket_index():
    qi = np.arange(L_BAND)[:, None]
    kj = np.arange(2 * L_BAND)[None, :]
    dist = qi + L_BAND - kj
    band = (dist >= 0) & (dist <= L_BAND)
    out = []
    for _, dil in DILATED_GROUPS:
        b = _rel_bucket_np(np.clip(dist, 0, L_BAND) * dil)
        out.append(np.where(band, b, -1))
    return np.stack(out).astype(np.int32)


def _sample_bucket_index(c_len, t_new):
    t = np.arange(8)[:, None]
    idx = np.arange(SAMPLE_KEYS)[None, :]
    delta = c_len + t - idx
    out = []
    for window, dil in DILATED_GROUPS:
        valid = (delta >= 0) & (delta % dil == 0) & (delta <= window) & (idx < c_len + t_new) & (t < t_new)
        b = _rel_bucket_np(np.clip(delta, 0, window))
        b = np.where(valid, b, -1)
        b = np.where(t >= t_new, 0, b)
        out.append(b)
    return np.stack(out).astype(np.int32)


def _ada_kernel(c_ref, w_ref, b_ref, o_ref):
    c = c_ref[...]
    a = (c * jax.nn.sigmoid(c)).astype(BF16)
    o_ref[...] = jnp.dot(a, w_ref[...].astype(BF16), preferred_element_type=F32) + b_ref[...]


def _ada(c_all, w_ada, b_ada):
    depth = w_ada.shape[0]
    tn = 512
    return pl.pallas_call(
        _ada_kernel,
        grid=(depth, 3 * D_MODEL // tn),
        in_specs=[
            pl.BlockSpec((MOD_ROWS, D_MODEL), lambda l, j: (0, 0)),
            pl.BlockSpec((None, D_MODEL, tn), lambda l, j: (l, 0, j)),
            pl.BlockSpec((None, 1, tn), lambda l, j: (l, 0, j)),
        ],
        out_specs=pl.BlockSpec((None, MOD_ROWS, tn), lambda l, j: (l, 0, j)),
        out_shape=jax.ShapeDtypeStruct((depth, MOD_ROWS, 3 * D_MODEL), F32),
        compiler_params=_params(("arbitrary", "arbitrary"), 40),
        name="ada",
    )(c_all, w_ada, b_ada.reshape(depth, 1, 3 * D_MODEL))


def _bias_kernel(tab_ref, pidx_ref, sidx_ref, pb_ref, sb_ref):
    h = pl.program_id(0)
    for idx_ref, out_ref in ((pidx_ref, pb_ref), (sidx_ref, sb_ref)):
        for g in range(N_GROUPS):
            idx = idx_ref[g]
            acc = jnp.full(idx.shape, MASKED, F32)
            for b in range(N_BUCKETS):
                acc = jnp.where(idx == b, tab_ref[b, h], acc)
            out_ref[g] = acc


def _bias_tables(rel_table, c_len, t_new):
    pidx = jnp.asarray(_prompt_bucket_index())
    sidx = jnp.asarray(_sample_bucket_index(c_len, t_new))
    return pl.pallas_call(
        _bias_kernel,
        grid=(N_HEADS,),
        in_specs=[
            pl.BlockSpec(memory_space=pltpu.SMEM),
            pl.BlockSpec((N_GROUPS, L_BAND, 2 * L_BAND), lambda h: (0, 0, 0)),
            pl.BlockSpec((N_GROUPS, 8, SAMPLE_KEYS), lambda h: (0, 0, 0)),
        ],
        out_specs=[
            pl.BlockSpec((N_GROUPS, None, L_BAND, 2 * L_BAND), lambda h: (0, h, 0, 0)),
            pl.BlockSpec((N_GROUPS, None, 8, SAMPLE_KEYS), lambda h: (0, h, 0, 0)),
        ],
        out_shape=[
            jax.ShapeDtypeStruct((N_GROUPS, N_HEADS, L_BAND, 2 * L_BAND), F32),
            jax.ShapeDtypeStruct((N_GROUPS, N_HEADS, 8, SAMPLE_KEYS), F32),
        ],
        compiler_params=_params(("arbitrary",), 32),
        name="bias_tables",
    )(rel_table, pidx, sidx)


def _mod_specs(mod_idx, which, tm, rows_per_mod, tile_of=lambda i: i):
    if mod_idx is None:
        return pl.BlockSpec((tm, D_MODEL), lambda i, *_: (tile_of(i), 0))
    return pl.BlockSpec((None, 1, D_MODEL), lambda i, *_: (mod_idx(tile_of(i) // rows_per_mod, which), 0, 0))


def _prenorm_kernel(x_ref, shift_ref, scale_ref, g_ref, h_ref):
    tm = x_ref.shape[0]
    chunk = min(tm, NORM_CHUNK)
    per_row = shift_ref.shape[0] == tm

    def norm(c, _):
        rows = pl.ds(pl.multiple_of(c * chunk, chunk), chunk)
        mrows = rows if per_row else slice(None)
        x = x_ref[rows, :]
        y = x * lax.rsqrt(jnp.mean(x * x, axis=-1, keepdims=True) + EPS)
        y = y * g_ref[...]
        h_ref[rows, :] = (y * (1.0 + scale_ref[mrows, :]) + shift_ref[mrows, :]).astype(h_ref.dtype)
        return 0

    lax.fori_loop(0, tm // chunk, norm, 0)


def _prenorm(x2, shift, scale, mod_idx, g_pre, l, *, tm, rows_per_mod):
    m_rows = x2.shape[0]
    return pl.pallas_call(
        _prenorm_kernel,
        grid=(m_rows // tm,),
        in_specs=[
            pl.BlockSpec((tm, D_MODEL), lambda i: (i, 0)),
            _mod_specs(mod_idx, 0, tm, rows_per_mod),
            _mod_specs(mod_idx, 1, tm, rows_per_mod),
            pl.BlockSpec((None, 1, D_MODEL), lambda i: (l, 0, 0)),
        ],
        out_specs=pl.BlockSpec((tm, D_MODEL), lambda i: (i, 0)),
        out_shape=jax.ShapeDtypeStruct((m_rows, D_MODEL), BF16),
        compiler_params=_params(("arbitrary",), 40),
        name="prenorm",
    )(x2, shift, scale, g_pre)


def _inproj_kernel(h_ref, w_ref, *rest, nj_seg, layer):
    a_ref, k_hbm, v_hbm, stage, sem = rest[-5:]
    i, j = pl.program_id(0), pl.program_id(1)
    tm = h_ref.shape[0]
    seg, part = j // nj_seg, j % nj_seg
    first_kv, last_kv = SEG_K * nj_seg, (SEG_V + 1) * nj_seg - 1

    def head_copy(dst_hbm, group):
        rows = pl.ds(i * tm, tm)
        heads = pl.ds(group * KV_HEADS, KV_HEADS)
        return pltpu.make_async_copy(stage, dst_hbm.at[layer, rows, heads, :], sem.at[0])

    acc = jnp.dot(h_ref[...], w_ref[...], preferred_element_type=F32)
    a_ref[...] = acc

    @pl.when((j > first_kv) & (j <= last_kv + 1))
    def _():
        head_copy(k_hbm, 0).wait()

    def stage_and_send(dst_hbm):
        flat = stage.reshape(tm * KV_HEADS, HEAD_DIM)
        for hh in range(KV_HEADS):
            flat[pl.ds(hh, tm, stride=KV_HEADS), :] = acc[:, hh * HEAD_DIM:(hh + 1) * HEAD_DIM]
        head_copy(dst_hbm, part).start()

    @pl.when(seg == SEG_K)
    def _():
        stage_and_send(k_hbm)

    @pl.when(seg == SEG_V)
    def _():
        stage_and_send(v_hbm)


def _inproj(h2, w_in, l, depth, kv_prev, *, tm, vmem_mib):
    m_rows = h2.shape[0]
    tn = KV_HEADS * HEAD_DIM
    nj_seg = D_ATT // tn
    in_specs = [
        pl.BlockSpec((tm, D_MODEL), lambda i, j: (i, 0)),
        pl.BlockSpec((D_MODEL, tn), lambda i, j: (0, j)),
    ]
    args = [h2, w_in]
    aliases = {}
    if kv_prev is not None:
        in_specs += [pl.BlockSpec(memory_space=pl.ANY)] * 2
        args += list(kv_prev)
        aliases = {2: 1, 3: 2}
    kv_shape = jax.ShapeDtypeStruct((depth, m_rows, N_HEADS, HEAD_DIM), F32)
    return pl.pallas_call(
        functools.partial(_inproj_kernel, nj_seg=nj_seg, layer=l),
        grid=(m_rows // tm, N_SEG * nj_seg),
        in_specs=in_specs,
        out_specs=[
            pl.BlockSpec((None, tm, tn), lambda i, j: (j // nj_seg, i, j % nj_seg)),
            pl.BlockSpec(memory_space=pl.ANY),
            pl.BlockSpec(memory_space=pl.ANY),
        ],
        out_shape=[jax.ShapeDtypeStruct((N_SEG, m_rows, D_ATT), F32), kv_shape, kv_shape],
        scratch_shapes=[pltpu.VMEM((tm, KV_HEADS, HEAD_DIM), F32), pltpu.SemaphoreType.DMA((1,))],
        input_output_aliases=aliases,
        compiler_params=_params(("arbitrary", "arbitrary"), vmem_mib),
        name="inproj",
    )(*args)


def _attn_prompt_kernel(q_ref, k_ref, v_ref, g_ref, bias_ref, wsrc_ref, u_ref, wdst_ref,
                        qa, ka, va, qb, kb, vb, o_scr, e_scr, *, seq):
    wdst_ref[...] = wsrc_ref[...].astype(wdst_ref.dtype)
    n_blk = seq // L_BAND

    def regroup(dst, src, stride):
        sub = seq // stride
        for r in range(stride):
            dst[r * sub:(r + 1) * sub, :] = src[pl.ds(r, sub, stride=stride), :]

    def branch(gi, dil, q_src, k_src, v_src):
        per_res = n_blk // dil
        q3 = q_src[...].astype(BF16).reshape(n_blk, L_BAND, HEAD_DIM)
        k3 = k_src[...].astype(BF16).reshape(n_blk, L_BAND, HEAD_DIM)
        v3 = v_src[...].astype(BF16).reshape(n_blk, L_BAND, HEAD_DIM)
        if per_res > 1:
            kk = jnp.concatenate([jnp.concatenate([k3[:1], k3[:-1]], axis=0), k3], axis=1)
            vv = jnp.concatenate([jnp.concatenate([v3[:1], v3[:-1]], axis=0), v3], axis=1)
            blk = lax.broadcasted_iota(jnp.int32, (n_blk, 1, 2 * L_BAND), 0)
            col = lax.broadcasted_iota(jnp.int32, (n_blk, 1, 2 * L_BAND), 2)
            first = jnp.where((blk % per_res == 0) & (col < L_BAND), MASKED, 0.0)
            bias = bias_ref[gi][None] + first
        else:
            kk, vv = k3, v3
            bias = bias_ref[gi, :, L_BAND:][None]
        s = jnp.einsum('bqd,bkd->bqk', q3, kk, preferred_element_type=F32) * ATT_SCALE + bias
        mx = jnp.max(s, axis=-1, keepdims=True)
        p = jnp.exp(s - mx)
        den = jnp.sum(p, axis=-1, keepdims=True)
        o = jnp.einsum('bqk,bkd->bqd', p.astype(BF16), vv, preferred_element_type=F32) / den
        lse = jnp.broadcast_to(mx + jnp.log(den), (n_blk, L_BAND, HEAD_DIM))
        o = o.reshape(seq, HEAD_DIM)
        lse = lse.reshape(seq, HEAD_DIM)
        if dil == 1:
            o_scr[gi] = o
            e_scr[gi] = lse
        else:
            sub = seq // dil
            for r in range(dil):
                o_scr[gi, pl.ds(r, sub, stride=dil), :] = o[r * sub:(r + 1) * sub, :]
                e_scr[gi, pl.ds(r, sub, stride=dil), :] = lse[r * sub:(r + 1) * sub, :]

    (_, d0), (_, d1), (_, d2) = DILATED_GROUPS
    assert d0 == 1 and d2 == d1 * d1
    branch(0, d0, q_ref, k_ref, v_ref)
    for dst, src in ((qa, q_ref), (ka, k_ref), (va, v_ref)):
        regroup(dst, src, d1)
    branch(1, d1, qa, ka, va)
    for dst, src in ((qb, qa), (kb, ka), (vb, va)):
        regroup(dst, src, d1)
    branch(2, d2, qb, kb, vb)

    chunk = 256

    def combine(c, _):
        rows = pl.ds(pl.multiple_of(c * chunk, chunk), chunk)
        es = [e_scr[gi, rows, :] for gi in range(N_GROUPS)]
        top = functools.reduce(jnp.maximum, es)
        ws = [jnp.exp(e - top) for e in es]
        num = sum(w * o_scr[gi, rows, :] for gi, w in enumerate(ws))
        den = sum(ws)
        g = g_ref[rows, :]
        u_ref[rows, :] = ((num / den) * (g * jax.nn.sigmoid(g))).astype(u_ref.dtype)
        return 0

    lax.fori_loop(0, seq // chunk, combine, 0)


def _attn_prompt(a6, pbias, w_out, l, batch, seq):
    blk = (None, None, seq, HEAD_DIM)
    w_rows = D_MODEL // (batch * N_HEADS)
    return pl.pallas_call(
        functools.partial(_attn_prompt_kernel, seq=seq),
        grid=(batch, N_HEADS),
        in_specs=[pl.BlockSpec(blk, lambda b, h, slot=slot: (slot, b, 0, h)) for slot in range(4)] + [
            pl.BlockSpec((N_GROUPS, None, L_BAND, 2 * L_BAND), lambda b, h: (0, h, 0, 0)),
            pl.BlockSpec((None, w_rows, D_MODEL), lambda b, h: (l, b * N_HEADS + h, 0)),
        ],
        out_specs=[
            pl.BlockSpec((None, seq, HEAD_DIM), lambda b, h: (b, 0, h)),
            pl.BlockSpec((w_rows, D_MODEL), lambda b, h: (b * N_HEADS + h, 0)),
        ],
        out_shape=[
            jax.ShapeDtypeStruct((batch, seq, D_ATT), BF16),
            jax.ShapeDtypeStruct((D_MODEL, D_MODEL), BF16),
        ],
        scratch_shapes=[pltpu.VMEM((seq, HEAD_DIM), F32)] * 6 + [pltpu.VMEM((N_GROUPS, seq, HEAD_DIM), F32)] * 2,
        compiler_params=_params(("arbitrary", "arbitrary"), 48),
        name="attn_prompt",
    )(a6, a6, a6, a6, pbias, w_out)


def _attn_sample_kernel(q_ref, kn_ref, vn_ref, ck_ref, cv_ref, g_ref, bias_ref, u_ref, q_scr, k_scr, v_scr,
                        *, c_len, t_new):
    pad = SAMPLE_KEYS - c_len - t_new
    q_scr[t_new:8, :] = jnp.zeros((8 - t_new, HEAD_DIM), F32)
    for slot in range(2):
        k_scr[slot, c_len + t_new:SAMPLE_KEYS, :] = jnp.zeros((pad, HEAD_DIM), F32)
        v_scr[slot, c_len + t_new:SAMPLE_KEYS, :] = jnp.zeros((pad, HEAD_DIM), F32)
    for hh in range(KV_HEADS):
        lanes = slice(hh * HEAD_DIM, (hh + 1) * HEAD_DIM)
        slot = hh % 2
        q_scr[0:t_new, :] = q_ref[:, lanes]
        for new_ref, cache_ref, scr in ((kn_ref, ck_ref, k_scr), (vn_ref, cv_ref, v_scr)):
            heads = cache_ref.reshape(c_len * KV_HEADS, HEAD_DIM)
            scr[slot, 0:c_len, :] = heads[pl.ds(hh, c_len, stride=KV_HEADS), :]
            scr[slot, c_len:c_len + t_new, :] = new_ref[:, lanes]
        kk = k_scr[slot].astype(BF16)
        vv = v_scr[slot].astype(BF16)
        s = lax.dot_general(q_scr[...].astype(BF16), kk, (((1,), (1,)), ((), ())),
                            preferred_element_type=F32) * ATT_SCALE
        ms, ls, os_ = [], [], []
        for gi in range(N_GROUPS):
            sg = s + bias_ref[gi, hh]
            mx = jnp.max(sg, axis=-1, keepdims=True)
            p = jnp.exp(sg - mx)
            ms.append(mx)
            ls.append(jnp.sum(p, axis=-1, keepdims=True))
            os_.append(jnp.dot(p.astype(BF16), vv, preferred_element_type=F32))
        top = functools.reduce(jnp.maximum, ms)
        ws = [jnp.exp(m - top) for m in ms]
        num = sum(w * o for w, o in zip(ws, os_))
        den = sum(w * d for w, d in zip(ws, ls))
        g = g_ref[:, lanes]
        u_ref[:, lanes] = (num / den)[0:t_new, :] * (g * jax.nn.sigmoid(g))


def _attn_sample(a6, cache_k, cache_v, sbias, l, batch, t_new, c_len):
    new_blk = (None, None, t_new, KV_HEADS * HEAD_DIM)
    cache_blk = (None, None, c_len, KV_HEADS, HEAD_DIM)
    return pl.pallas_call(
        functools.partial(_attn_sample_kernel, c_len=c_len, t_new=t_new),
        grid=(batch, N_HEADS // KV_HEADS),
        in_specs=[
            pl.BlockSpec(new_blk, lambda b, hg: (0, b, 0, hg)),
            pl.BlockSpec(new_blk, lambda b, hg: (1, b, 0, hg)),
            pl.BlockSpec(new_blk, lambda b, hg: (2, b, 0, hg)),
            pl.BlockSpec(cache_blk, lambda b, hg: (l, b, 0, hg, 0)),
            pl.BlockSpec(cache_blk, lambda b, hg: (l, b, 0, hg, 0)),
            pl.BlockSpec(new_blk, lambda b, hg: (3, b, 0, hg)),
            pl.BlockSpec((N_GROUPS, KV_HEADS, 8, SAMPLE_KEYS), lambda b, hg: (0, hg, 0, 0)),
        ],
        out_specs=pl.BlockSpec((None, t_new, KV_HEADS * HEAD_DIM), lambda b, hg: (b, 0, hg)),
        out_shape=jax.ShapeDtypeStruct((batch, t_new, D_ATT), F32),
        scratch_shapes=[pltpu.VMEM((8, HEAD_DIM), F32), pltpu.VMEM((2, SAMPLE_KEYS, HEAD_DIM), F32),
                        pltpu.VMEM((2, SAMPLE_KEYS, HEAD_DIM), F32)],
        compiler_params=_params(("arbitrary", "arbitrary"), 48),
        name="attn_sample",
    )(a6, a6, a6, cache_k, cache_v, a6, sbias)


def _lru_gates(xc, wa_ref, ba_ref, wx_ref, bx_ref, lam_ref):
    xcb = xc.astype(BF16)
    r = jax.nn.sigmoid(jnp.dot(xcb, wa_ref[...].astype(BF16), preferred_element_type=F32) + ba_ref[...])
    i = jax.nn.sigmoid(jnp.dot(xcb, wx_ref[...].astype(BF16), preferred_element_type=F32) + bx_ref[...])
    nl = -lam_ref[...]
    softplus = jnp.maximum(nl, 0.0) + jnp.log1p(jnp.exp(-jnp.abs(nl)))
    log_a = -LRU_C * r * softplus
    a = jnp.exp(log_a)
    t = jnp.tanh(log_a)
    b = jnp.sqrt(-2.0 * t / (1.0 - t)) * (i * xc)
    return a, b


def _lru_prompt_kernel(x_ref, g_ref, wc_ref, bc_ref, wa_ref, ba_ref, wx_ref, bx_ref, lam_ref, *rest, seq, cast):
    if cast:
        wsrc_ref, u_ref, h_ref, cs_ref, wdst_ref, xpad, a_scr, b_scr = rest
        wdst_ref[...] = wsrc_ref[...].astype(wdst_ref.dtype)
    else:
        u_ref, h_ref, cs_ref, xpad, a_scr, b_scr = rest
    x = x_ref[...]
    xpad[0:8, :] = jnp.zeros((8, LRU_BLOCK), F32)
    xpad[8:8 + seq, :] = x
    xc = bc_ref[...] + xpad[pl.ds(8 - 3, seq), :] * wc_ref[0:1, :]
    xc = xc + xpad[pl.ds(8 - 2, seq), :] * wc_ref[1:2, :]
    xc = xc + xpad[pl.ds(8 - 1, seq), :] * wc_ref[2:3, :]
    xc = xc + x * wc_ref[3:4, :]
    a, b = _lru_gates(xc, wa_ref, ba_ref, wx_ref, bx_ref, lam_ref)
    a_scr[...] = a
    b_scr[...] = b

    row = lax.broadcasted_iota(jnp.int32, (8, LRU_BLOCK), 0)

    def body(c, h_prev):
        rows = pl.ds(pl.multiple_of(c * 8, 8), 8)
        ca = a_scr[rows, :]
        cb = b_scr[rows, :]
        for k in (1, 2, 4):
            a_sh = jnp.where(row >= k, pltpu.roll(ca, k, 0), 1.0)
            b_sh = jnp.where(row >= k, pltpu.roll(cb, k, 0), 0.0)
            cb = ca * b_sh + cb
            ca = ca * a_sh
        h = ca * h_prev + cb
        b_scr[rows, :] = h
        return jnp.broadcast_to(h[7:8, :], (8, LRU_BLOCK))

    lax.fori_loop(0, seq // 8, body, jnp.zeros((8, LRU_BLOCK), F32), unroll=8)
    y = b_scr[...]
    g = g_ref[...]
    u_ref[...] = (y * (g * jax.nn.sigmoid(g))).astype(u_ref.dtype)
    h_ref[...] = b_scr[seq - 1:seq, :]
    cs_ref[...] = x[seq - (CONV_W - 1):seq, :]


def _lru_weight_specs(l, n_of):
    return [
        pl.BlockSpec((None, CONV_W, LRU_BLOCK), lambda *g: (l, 0, n_of(*g))),
        pl.BlockSpec((None, 1, LRU_BLOCK), lambda *g: (l, 0, n_of(*g))),
        pl.BlockSpec((None, None, LRU_BLOCK, LRU_BLOCK), lambda *g: (l, n_of(*g), 0, 0)),
        pl.BlockSpec((None, 1, LRU_BLOCK), lambda *g: (l, 0, n_of(*g))),
        pl.BlockSpec((None, None, LRU_BLOCK, LRU_BLOCK), lambda *g: (l, n_of(*g), 0, 0)),
        pl.BlockSpec((None, 1, LRU_BLOCK), lambda *g: (l, 0, n_of(*g))),
        pl.BlockSpec((None, 1, LRU_BLOCK), lambda *g: (l, 0, n_of(*g))),
    ]


def _lru_prompt(a6, lru_w, l, batch, seq, w_in=None):
    blk = (None, None, seq, LRU_BLOCK)
    n_of = lambda b, n: n
    cast = w_in is not None
    w_rows = D_MODEL // (batch * N_LRU_BLOCKS)
    in_specs = [
        pl.BlockSpec(blk, lambda b, n: (4, b, 0, n)),
        pl.BlockSpec(blk, lambda b, n: (5, b, 0, n)),
    ] + _lru_weight_specs(l, n_of)
    out_specs = [
        pl.BlockSpec((None, seq, LRU_BLOCK), lambda b, n: (b, 0, n)),
        pl.BlockSpec((None, 1, LRU_BLOCK), lambda b, n: (b, 0, n)),
        pl.BlockSpec((None, CONV_W - 1, LRU_BLOCK), lambda b, n: (b, 0, n)),
    ]
    out_shape = [
        jax.ShapeDtypeStruct((batch, seq, D_LRU), BF16),
        jax.ShapeDtypeStruct((batch, 1, D_LRU), F32),
        jax.ShapeDtypeStruct((batch, CONV_W - 1, D_LRU), F32),
    ]
    args = [a6, a6, *lru_w]
    if cast:
        in_specs.append(pl.BlockSpec((None, w_rows, D_IN), lambda b, n: (l + 1, b * N_LRU_BLOCKS + n, 0)))
        out_specs.append(pl.BlockSpec((w_rows, D_IN), lambda b, n: (b * N_LRU_BLOCKS + n, 0)))
        out_shape.append(jax.ShapeDtypeStruct((D_MODEL, D_IN), BF16))
        args.append(w_in)
    res = pl.pallas_call(
        functools.partial(_lru_prompt_kernel, seq=seq, cast=cast),
        grid=(batch, N_LRU_BLOCKS),
        in_specs=in_specs,
        out_specs=out_specs,
        out_shape=out_shape,
        scratch_shapes=[pltpu.VMEM((seq + 8, LRU_BLOCK), F32), pltpu.VMEM((seq, LRU_BLOCK), F32),
                        pltpu.VMEM((seq, LRU_BLOCK), F32)],
        compiler_params=_params(("arbitrary", "arbitrary"), 40),
        name="lru_prompt",
    )(*args)
    return res if cast else (*res, None)


def _lru_sample_kernel(x_ref, g_ref, sc_ref, h0_ref, wc_ref, bc_ref, wa_ref, ba_ref, wx_ref, bx_ref, lam_ref,
                       u_ref, h_ref, cs_ref, xc_scr, a_scr, b_scr, y_scr, *, batch, t_new):
    n_state = CONV_W - 1

    def ext_row(b, i):
        if i < n_state:
            return sc_ref[b, i:i + 1, :]
        return x_ref[b, i - n_state:i - n_state + 1, :]

    for b in range(batch):
        for t in range(t_new):
            acc = bc_ref[...] + ext_row(b, t) * wc_ref[0:1, :]
            for j in range(1, CONV_W):
                acc = acc + ext_row(b, t + j) * wc_ref[j:j + 1, :]
            xc_scr[b * t_new + t:b * t_new + t + 1, :] = acc
    a, bb = _lru_gates(xc_scr[...], wa_ref, ba_ref, wx_ref, bx_ref, lam_ref)
    a_scr[...] = a
    b_scr[...] = bb
    for b in range(batch):
        h = h0_ref[b:b + 1, :]
        for t in range(t_new):
            r = b * t_new + t
            h = a_scr[r:r + 1, :] * h + b_scr[r:r + 1, :]
            y_scr[r:r + 1, :] = h
        h_ref[b:b + 1, :] = h
        for i in range(n_state):
            cs_ref[b, i:i + 1, :] = ext_row(b, t_new + i)
    for b in range(batch):
        g = g_ref[b]
        u_ref[b] = y_scr[b * t_new:(b + 1) * t_new, :] * (g * jax.nn.sigmoid(g))


def _lru_sample(a6, state_conv, state_h, lru_w, l, batch, t_new):
    blk = (None, batch, t_new, LRU_BLOCK)
    n_of = lambda n: n
    rows = batch * t_new
    return pl.pallas_call(
        functools.partial(_lru_sample_kernel, batch=batch, t_new=t_new),
        grid=(N_LRU_BLOCKS,),
        in_specs=[
            pl.BlockSpec(blk, lambda n: (4, 0, 0, n)),
            pl.BlockSpec(blk, lambda n: (5, 0, 0, n)),
            pl.BlockSpec((None, batch, CONV_W - 1, LRU_BLOCK), lambda n: (l, 0, 0, n)),
            pl.BlockSpec((None, batch, LRU_BLOCK), lambda n: (l, 0, n)),
        ] + _lru_weight_specs(l, n_of),
        out_specs=[
            pl.BlockSpec((batch, t_new, LRU_BLOCK), lambda n: (0, 0, n)),
            pl.BlockSpec((batch, LRU_BLOCK), lambda n: (0, n)),
            pl.BlockSpec((batch, CONV_W - 1, LRU_BLOCK), lambda n: (0, 0, n)),
        ],
        out_shape=[
            jax.ShapeDtypeStruct((batch, t_new, D_LRU), F32),
            jax.ShapeDtypeStruct((batch, D_LRU), F32),
            jax.ShapeDtypeStruct((batch, CONV_W - 1, D_LRU), F32),
        ],
        scratch_shapes=[pltpu.VMEM((rows, LRU_BLOCK), F32)] * 4,
        compiler_params=_params(("arbitrary",), 32),
        name="lru_sample",
    )(a6, a6, state_conv, state_h, *lru_w)


def _outproj_kernel(ua_ref, ul_ref, w_ref, x_ref, gate_ref, gp_ref, *rest, n_tiles, has_next):
    if has_next:
        shift_ref, scale_ref, gn_ref, y_ref, h_ref, acc_a, acc_b = rest
    else:
        y_ref, acc_a, acc_b = rest
    i = pl.program_id(0)
    tm = x_ref.shape[0]
    chunk = min(tm, EPILOGUE_CHUNK)
    per_row = gate_ref.shape[0] == tm

    def matmul(acc):
        acc[...] = (jnp.dot(ua_ref[...].astype(BF16), w_ref[0:D_ATT, :], preferred_element_type=F32)
                    + jnp.dot(ul_ref[...].astype(BF16), w_ref[D_ATT:D_MODEL, :], preferred_element_type=F32))

    def epilogue(acc):
        for c in range(tm // chunk):
            rows = slice(c * chunk, (c + 1) * chunk)
            mrows = rows if per_row else slice(None)
            y = acc[rows, :]
            y = y * lax.rsqrt(jnp.mean(y * y, axis=-1, keepdims=True) + EPS)
            out = x_ref[rows, :] + gate_ref[mrows, :] * (y * gp_ref[...])
            y_ref[rows, :] = out
            if has_next:
                hn = out * lax.rsqrt(jnp.mean(out * out, axis=-1, keepdims=True) + EPS)
                hn = hn * gn_ref[...]
                h_ref[rows, :] = (hn * (1.0 + scale_ref[mrows, :]) + shift_ref[mrows, :]).astype(h_ref.dtype)

    accs = (acc_a, acc_b)

    @pl.when(i == 0)
    def _():
        matmul(accs[0])

    for parity in range(2):
        @pl.when((i > 0) & (i < n_tiles) & (i % 2 == parity))
        def _(parity=parity):
            epilogue(accs[1 - parity])
            matmul(accs[parity])

    @pl.when(i == n_tiles)
    def _():
        epilogue(accs[(n_tiles - 1) % 2])


def _outproj(u_att, u_lru, w_out, x2, gate, mod_idx, g_post, l, nxt, *, tm, rows_per_mod, vmem_mib):
    m_rows = x2.shape[0]
    n_tiles = m_rows // tm
    mm_tile = lambda i: jnp.minimum(i, n_tiles - 1)
    ep_tile = lambda i: jnp.maximum(i - 1, 0)
    row_spec = pl.BlockSpec((tm, D_MODEL), lambda i: (ep_tile(i), 0))
    in_specs = [
        pl.BlockSpec((tm, D_ATT), lambda i: (mm_tile(i), 0)),
        pl.BlockSpec((tm, D_LRU), lambda i: (mm_tile(i), 0)),
        pl.BlockSpec((D_MODEL, D_MODEL), lambda i: (0, 0), pipeline_mode=pl.Buffered(1)),
        row_spec,
        _mod_specs(mod_idx, 2, tm, rows_per_mod, ep_tile),
        pl.BlockSpec((None, 1, D_MODEL), lambda i: (l, 0, 0)),
    ]
    args = [u_att, u_lru, w_out, x2, gate, g_post]
    out_specs = [row_spec]
    out_shape = [jax.ShapeDtypeStruct((m_rows, D_MODEL), F32)]
    if nxt is not None:
        shift_n, scale_n, idx_n, g_pre, l_next = nxt
        in_specs += [
            _mod_specs(idx_n, 0, tm, rows_per_mod, ep_tile),
            _mod_specs(idx_n, 1, tm, rows_per_mod, ep_tile),
            pl.BlockSpec((None, 1, D_MODEL), lambda i: (l_next, 0, 0)),
        ]
        args += [shift_n, scale_n, g_pre]
        out_specs.append(row_spec)
        out_shape.append(jax.ShapeDtypeStruct((m_rows, D_MODEL), BF16))
    res = pl.pallas_call(
        functools.partial(_outproj_kernel, n_tiles=n_tiles, has_next=nxt is not None),
        grid=(n_tiles + 1,),
        in_specs=in_specs,
        out_specs=out_specs,
        out_shape=out_shape,
        scratch_shapes=[pltpu.VMEM((tm, D_MODEL), F32)] * 2,
        compiler_params=_params(("arbitrary",), vmem_mib),
        name="outproj",
    )(*args)
    return res if nxt is not None else (res[0], None)


def kernel(x_prompt, x_sample, cache_k, cache_v, state_h, state_conv, c_prompt, c_sample, rel_table, w_ada, b_ada,
           g_pre, w_in, w_conv, b_conv, w_a, b_a, w_x, b_x, lam, w_out, g_post):
    depth = w_in.shape[0]
    bp, seq, _ = x_prompt.shape
    bs, t_new, _ = x_sample.shape
    c_len = cache_k.shape[2]
    assert bp + bs <= MOD_ROWS and c_len + t_new <= SAMPLE_KEYS and t_new <= 8
    assert seq % (L_BAND * DILATED_GROUPS[-1][1]) == 0 and c_len >= WINDOW_MAX

    c_all = jnp.concatenate([c_prompt, c_sample, jnp.zeros((MOD_ROWS - bp - bs, D_MODEL), F32)], axis=0)
    mod = _ada(c_all, w_ada, b_ada)
    mod_tab = mod.reshape(depth * MOD_ROWS * 3, 1, D_MODEL)
    pbias, sbias = _bias_tables(rel_table, c_len, t_new)

    w_in_l = w_in[0].astype(BF16)
    g_pre3 = g_pre.reshape(depth, 1, D_MODEL)
    g_post3 = g_post.reshape(depth, 1, D_MODEL)

    xp = x_prompt.reshape(bp * seq, D_MODEL)
    xs = x_sample.reshape(bs * t_new, D_MODEL)
    rows_s = bs * t_new
    lru_w = (w_conv, b_conv.reshape(depth, 1, D_LRU), w_a, b_a.reshape(depth, 1, D_LRU),
             w_x, b_x.reshape(depth, 1, D_LRU), lam.reshape(depth, 1, D_LRU))
    kv_p = kv_s = None
    hp_l, cp_l, hs_l, cs_l = [], [], [], []
    tm_norm, tm_in, tm_out = 512, 1024, 128
    p_idx = lambda l: (lambda b, which: (l * MOD_ROWS + b) * 3 + which)
    mod_s = [jnp.repeat(mod[l, bp:bp + bs], t_new, axis=0) for l in range(depth)]
    shift_s = [m[:, :D_MODEL] for m in mod_s]
    scale_s = [m[:, D_MODEL:2 * D_MODEL] for m in mod_s]
    gate_s = [m[:, 2 * D_MODEL:] for m in mod_s]
    hp = _prenorm(xp, mod_tab, mod_tab, p_idx(0), g_pre3, 0, tm=tm_norm, rows_per_mod=seq // tm_norm)
    hs = _prenorm(xs, shift_s[0], scale_s[0], None, g_pre3, 0, tm=rows_s, rows_per_mod=1)
    for l in range(depth):
        more = l + 1 < depth
        a6, kp, vp = _inproj(hp, w_in_l, l, depth, kv_p, tm=tm_in, vmem_mib=56)
        kv_p = (kp, vp)
        a6 = a6.reshape(N_SEG, bp, seq, D_ATT)
        u_att, w_out_l = _attn_prompt(a6, pbias, w_out, l, bp, seq)
        u_lru, h_last, conv, w_in_next = _lru_prompt(a6, lru_w, l, bp, seq, w_in if more else None)
        nxt = (mod_tab, mod_tab, p_idx(l + 1), g_pre3, l + 1) if more else None
        xp, hp = _outproj(u_att.reshape(bp * seq, D_ATT), u_lru.reshape(bp * seq, D_LRU), w_out_l, xp, mod_tab,
                          p_idx(l), g_post3, l, nxt, tm=tm_out, rows_per_mod=seq // tm_out, vmem_mib=56)
        hp_l.append(h_last.reshape(bp, D_LRU))
        cp_l.append(conv)
        a6, ks, vs = _inproj(hs, w_in_l, l, depth, kv_s, tm=rows_s, vmem_mib=40)
        kv_s = (ks, vs)
        a6 = a6.reshape(N_SEG, bs, t_new, D_ATT)
        u_att = _attn_sample(a6, cache_k, cache_v, sbias, l, bs, t_new, c_len)
        u_lru, h_last, conv = _lru_sample(a6, state_conv, state_h, lru_w, l, bs, t_new)
        nxt = (shift_s[l + 1], scale_s[l + 1], None, g_pre3, l + 1) if more else None
        xs, hs = _outproj(u_att.reshape(rows_s, D_ATT), u_lru.reshape(rows_s, D_LRU), w_out_l, xs, gate_s[l], None,
                          g_post3, l, nxt, tm=rows_s, rows_per_mod=1, vmem_mib=48)
        hs_l.append(h_last)
        cs_l.append(conv)
        w_in_l = w_in_next

    kp, vp = kv_p
    ks, vs = kv_s
    return (xp.reshape(bp, seq, D_MODEL), xs.reshape(bs, t_new, D_MODEL),
            kp.reshape(depth, bp, seq, N_HEADS, HEAD_DIM), vp.reshape(depth, bp, seq, N_HEADS, HEAD_DIM),
            jnp.stack(hp_l), jnp.stack(cp_l),
            ks.reshape(depth, bs, t_new, N_HEADS, HEAD_DIM), vs.reshape(depth, bs, t_new, N_HEADS, HEAD_DIM),
            jnp.stack(hs_l), jnp.stack(cs_l))
```

```python
import functools
import math

import numpy as np
import jax
import jax.numpy as jnp
from jax import lax
from jax.experimental import pallas as pl
from jax.experimental.pallas import tpu as pltpu

D_MODEL = 4096
D_ATT = 2048
D_LRU = 2048
HEAD_DIM = 128
N_HEADS = 16
N_LRU_BLOCKS = 16
LRU_BLOCK = 128
CONV_W = 4
LRU_C = 8.0
DILATED_GROUPS = ((128, 1), (512, 4), (2048, 16))
N_GROUPS = len(DILATED_GROUPS)
WINDOW_MAX = 2048
N_BUCKETS = 32
MAX_EXACT = N_BUCKETS // 2
MAX_DISTANCE = WINDOW_MAX
EPS = 1e-6
ATT_SCALE = HEAD_DIM ** -0.5
D_IN = 4 * D_ATT + 2 * D_LRU
N_SEG = D_IN // D_ATT
SEG_K, SEG_V = 1, 2
L_BAND = 128
MASKED = -1e30
MOD_ROWS = 16
SAMPLE_KEYS = 2048 + 128
MIB = 1024 * 1024
NORM_CHUNK = 64
EPILOGUE_CHUNK = 128
SCAN_SEGS = 8
KV_HEADS = 8

F32 = jnp.float32
BF16 = jnp.bfloat16


def _params(semantics, vmem_mib):
    return pltpu.CompilerParams(dimension_semantics=semantics, vmem_limit_bytes=vmem_mib * MIB)


def _silu(x):
    half = 0.5 * x
    return half + half * jnp.tanh(half)


def _rel_bucket_np(dist):
    d = dist.astype(np.float32)
    large = np.float32(MAX_EXACT) + np.log(np.maximum(d, np.float32(1.0)) / np.float32(MAX_EXACT)) / np.float32(
        math.log(MAX_DISTANCE / MAX_EXACT)) * np.float32(N_BUCKETS - MAX_EXACT)
    large = np.minimum(large.astype(np.int32), N_BUCKETS - 1)
    return np.where(dist < MAX_EXACT, dist, large).astype(np.int32)


def _prompt_bucket_index():
    qi = np.arange(L_BAND)[:, None]
    kj = np.arange(2 * L_BAND)[None, :]
    dist = qi + L_BAND - kj
    band = (dist >= 0) & (dist <= L_BAND)
    out = []
    for _, dil in DILATED_GROUPS:
        b = _rel_bucket_np(np.clip(dist, 0, L_BAND) * dil)
        out.append(np.where(band, b, -1))
    return np.stack(out).astype(np.int32)


def _sample_bucket_index(c_len, t_new):
    t = np.arange(8)[:, None]
    idx = np.arange(SAMPLE_KEYS)[None, :]
    delta = c_len + t - idx
    out = []
    for window, dil in DILATED_GROUPS:
        valid = (delta >= 0) & (delta % dil == 0) & (delta <= window) & (idx < c_len + t_new) & (t < t_new)
        b = _rel_bucket_np(np.clip(delta, 0, window))
        b = np.where(valid, b, -1)
        b = np.where(t >= t_new, 0, b)
        out.append(b)
    return np.stack(out).astype(np.int32)


def _ada_kernel(c_ref, w_ref, b_ref, o_ref):
    c = c_ref[...]
    a = _silu(c).astype(BF16)
    o_ref[...] = jnp.dot(a, w_ref[...].astype(BF16), preferred_element_type=F32) + b_ref[...]


def _ada(c_all, w_ada, b_ada):
    depth = w_ada.shape[0]
    tn = 512
    return pl.pallas_call(
        _ada_kernel,
        grid=(depth, 3 * D_MODEL // tn),
        in_specs=[
            pl.BlockSpec((MOD_ROWS, D_MODEL), lambda l, j: (0, 0)),
            pl.BlockSpec((None, D_MODEL, tn), lambda l, j: (l, 0, j)),
            pl.BlockSpec((None, 1, tn), lambda l, j: (l, 0, j)),
        ],
        out_specs=pl.BlockSpec((None, MOD_ROWS, tn), lambda l, j: (l, 0, j)),
        out_shape=jax.ShapeDtypeStruct((depth, MOD_ROWS, 3 * D_MODEL), F32),
        compiler_params=_params(("arbitrary", "arbitrary"), 40),
        name="ada",
    )(c_all, w_ada, b_ada.reshape(depth, 1, 3 * D_MODEL))


def _bias_kernel(tab_ref, pidx_ref, sidx_ref, pb_ref, sb_ref):
    h = pl.program_id(0)
    for idx_ref, out_ref in ((pidx_ref, pb_ref), (sidx_ref, sb_ref)):
        for g in range(N_GROUPS):
            idx = idx_ref[g]
            acc = jnp.full(idx.shape, MASKED, F32)
            for b in range(N_BUCKETS):
                acc = jnp.where(idx == b, tab_ref[b, h], acc)
            out_ref[g] = acc


def _bias_tables(rel_table, c_len, t_new):
    pidx = jnp.asarray(_prompt_bucket_index())
    sidx = jnp.asarray(_sample_bucket_index(c_len, t_new))
    return pl.pallas_call(
        _bias_kernel,
        grid=(N_HEADS,),
        in_specs=[
            pl.BlockSpec(memory_space=pltpu.SMEM),
            pl.BlockSpec((N_GROUPS, L_BAND, 2 * L_BAND), lambda h: (0, 0, 0)),
            pl.BlockSpec((N_GROUPS, 8, SAMPLE_KEYS), lambda h: (0, 0, 0)),
        ],
        out_specs=[
            pl.BlockSpec((N_GROUPS, None, L_BAND, 2 * L_BAND), lambda h: (0, h, 0, 0)),
            pl.BlockSpec((N_GROUPS, None, 8, SAMPLE_KEYS), lambda h: (0, h, 0, 0)),
        ],
        out_shape=[
            jax.ShapeDtypeStruct((N_GROUPS, N_HEADS, L_BAND, 2 * L_BAND), F32),
            jax.ShapeDtypeStruct((N_GROUPS, N_HEADS, 8, SAMPLE_KEYS), F32),
        ],
        compiler_params=_params(("arbitrary",), 32),
        name="bias_tables",
    )(rel_table, pidx, sidx)


def _mod_specs(mod_idx, which, tm, rows_per_mod, tile_of=lambda i: i):
    if mod_idx is None:
        return pl.BlockSpec((tm, D_MODEL), lambda i, *_: (tile_of(i), 0))
    return pl.BlockSpec((None, 1, D_MODEL), lambda i, *_: (mod_idx(tile_of(i) // rows_per_mod, which), 0, 0))


def _prenorm_kernel(x_ref, shift_ref, scale_ref, g_ref, h_ref):
    tm = x_ref.shape[0]
    chunk = min(tm, NORM_CHUNK)
    per_row = shift_ref.shape[0] == tm

    def norm(c, _):
        rows = pl.ds(pl.multiple_of(c * chunk, chunk), chunk)
        mrows = rows if per_row else slice(None)
        x = x_ref[rows, :]
        y = x * lax.rsqrt(jnp.mean(x * x, axis=-1, keepdims=True) + EPS)
        y = y * g_ref[...]
        h_ref[rows, :] = (y * (1.0 + scale_ref[mrows, :]) + shift_ref[mrows, :]).astype(h_ref.dtype)
        return 0

    lax.fori_loop(0, tm // chunk, norm, 0)


def _prenorm(x2, shift, scale, mod_idx, g_pre, l, *, tm, rows_per_mod):
    m_rows = x2.shape[0]
    return pl.pallas_call(
        _prenorm_kernel,
        grid=(m_rows // tm,),
        in_specs=[
            pl.BlockSpec((tm, D_MODEL), lambda i: (i, 0)),
            _mod_specs(mod_idx, 0, tm, rows_per_mod),
            _mod_specs(mod_idx, 1, tm, rows_per_mod),
            pl.BlockSpec((None, 1, D_MODEL), lambda i: (l, 0, 0)),
        ],
        out_specs=pl.BlockSpec((tm, D_MODEL), lambda i: (i, 0)),
        out_shape=jax.ShapeDtypeStruct((m_rows, D_MODEL), BF16),
        compiler_params=_params(("arbitrary",), 40),
        name="prenorm",
    )(x2, shift, scale, g_pre)


def _inproj_kernel(h_ref, w_ref, *rest, nj_seg, layer):
    a_ref, k_hbm, v_hbm, stage, sem = rest[-5:]
    i, j = pl.program_id(0), pl.program_id(1)
    tm = h_ref.shape[0]
    seg, part = j // nj_seg, j % nj_seg
    first_kv, last_kv = SEG_K * nj_seg, (SEG_V + 1) * nj_seg - 1

    def head_copy(dst_hbm, group):
        rows = pl.ds(i * tm, tm)
        heads = pl.ds(group * KV_HEADS, KV_HEADS)
        return pltpu.make_async_copy(stage, dst_hbm.at[layer, rows, heads, :], sem.at[0])

    acc = jnp.dot(h_ref[...], w_ref[...], preferred_element_type=F32)
    a_ref[...] = acc

    @pl.when((j > first_kv) & (j <= last_kv + 1))
    def _():
        head_copy(k_hbm, 0).wait()

    def stage_and_send(dst_hbm):
        flat = stage.reshape(tm * KV_HEADS, HEAD_DIM)
        for hh in range(KV_HEADS):
            flat[pl.ds(hh, tm, stride=KV_HEADS), :] = acc[:, hh * HEAD_DIM:(hh + 1) * HEAD_DIM]
        head_copy(dst_hbm, part).start()

    @pl.when(seg == SEG_K)
    def _():
        stage_and_send(k_hbm)

    @pl.when(seg == SEG_V)
    def _():
        stage_and_send(v_hbm)


def _inproj(h2, w_in, l, depth, kv_prev, *, tm, vmem_mib):
    m_rows = h2.shape[0]
    tn = KV_HEADS * HEAD_DIM
    nj_seg = D_ATT // tn
    in_specs = [
        pl.BlockSpec((tm, D_MODEL), lambda i, j: (i, 0)),
        pl.BlockSpec((D_MODEL, tn), lambda i, j: (0, j)),
    ]
    args = [h2, w_in]
    aliases = {}
    if kv_prev is not None:
        in_specs += [pl.BlockSpec(memory_space=pl.ANY)] * 2
        args += list(kv_prev)
        aliases = {2: 1, 3: 2}
    kv_shape = jax.ShapeDtypeStruct((depth, m_rows, N_HEADS, HEAD_DIM), F32)
    return pl.pallas_call(
        functools.partial(_inproj_kernel, nj_seg=nj_seg, layer=l),
        grid=(m_rows // tm, N_SEG * nj_seg),
        in_specs=in_specs,
        out_specs=[
            pl.BlockSpec((None, tm, tn), lambda i, j: (j // nj_seg, i, j % nj_seg)),
            pl.BlockSpec(memory_space=pl.ANY),
            pl.BlockSpec(memory_space=pl.ANY),
        ],
        out_shape=[jax.ShapeDtypeStruct((N_SEG, m_rows, D_ATT), F32), kv_shape, kv_shape],
        scratch_shapes=[pltpu.VMEM((tm, KV_HEADS, HEAD_DIM), F32), pltpu.SemaphoreType.DMA((1,))],
        input_output_aliases=aliases,
        compiler_params=_params(("arbitrary", "arbitrary"), vmem_mib),
        name="inproj",
    )(*args)


def _attn_prompt_kernel(q_ref, k_ref, v_ref, g_ref, bias_ref, wsrc_ref, u_ref, wdst_ref,
                        qa, ka, va, qb, kb, vb, o_scr, e_scr, *, seq):
    wdst_ref[...] = wsrc_ref[...].astype(wdst_ref.dtype)
    n_blk = seq // L_BAND

    def regroup(dst, src, stride):
        sub = seq // stride
        for r in range(stride):
            dst[r * sub:(r + 1) * sub, :] = src[pl.ds(r, sub, stride=stride), :]

    def branch(gi, dil, q_src, k_src, v_src):
        per_res = n_blk // dil
        q3 = q_src[...].astype(BF16).reshape(n_blk, L_BAND, HEAD_DIM)
        k3 = k_src[...].astype(BF16).reshape(n_blk, L_BAND, HEAD_DIM)
        v3 = v_src[...].astype(BF16).reshape(n_blk, L_BAND, HEAD_DIM)
        if per_res > 1:
            kk = jnp.concatenate([jnp.concatenate([k3[:1], k3[:-1]], axis=0), k3], axis=1)
            vv = jnp.concatenate([jnp.concatenate([v3[:1], v3[:-1]], axis=0), v3], axis=1)
            blk = lax.broadcasted_iota(jnp.int32, (n_blk, 1, 2 * L_BAND), 0)
            col = lax.broadcasted_iota(jnp.int32, (n_blk, 1, 2 * L_BAND), 2)
            first = jnp.where((blk % per_res == 0) & (col < L_BAND), MASKED, 0.0)
            bias = bias_ref[gi][None] + first
        else:
            kk, vv = k3, v3
            bias = bias_ref[gi, :, L_BAND:][None]
        s = jnp.einsum('bqd,bkd->bqk', q3, kk, preferred_element_type=F32) * ATT_SCALE + bias
        mx = jnp.max(s, axis=-1, keepdims=True)
        p = jnp.exp(s - mx)
        den = jnp.sum(p, axis=-1, keepdims=True)
        o = jnp.einsum('bqk,bkd->bqd', p.astype(BF16), vv, preferred_element_type=F32) / den
        lse = jnp.broadcast_to(mx + jnp.log(den), (n_blk, L_BAND, HEAD_DIM))
        o = o.reshape(seq, HEAD_DIM)
        lse = lse.reshape(seq, HEAD_DIM)
        if dil == 1:
            o_scr[gi] = o
            e_scr[gi] = lse
        else:
            sub = seq // dil
            for r in range(dil):
                o_scr[gi, pl.ds(r, sub, stride=dil), :] = o[r * sub:(r + 1) * sub, :]
                e_scr[gi, pl.ds(r, sub, stride=dil), :] = lse[r * sub:(r + 1) * sub, :]

    (_, d0), (_, d1), (_, d2) = DILATED_GROUPS
    assert d0 == 1 and d2 == d1 * d1
    branch(0, d0, q_ref, k_ref, v_ref)
    for dst, src in ((qa, q_ref), (ka, k_ref), (va, v_ref)):
        regroup(dst, src, d1)
    branch(1, d1, qa, ka, va)
    for dst, src in ((qb, qa), (kb, ka), (vb, va)):
        regroup(dst, src, d1)
    branch(2, d2, qb, kb, vb)

    chunk = 256

    def combine(c, _):
        rows = pl.ds(pl.multiple_of(c * chunk, chunk), chunk)
        es = [e_scr[gi, rows, :] for gi in range(N_GROUPS)]
        top = functools.reduce(jnp.maximum, es)
        ws = [jnp.exp(e - top) for e in es]
        num = sum(w * o_scr[gi, rows, :] for gi, w in enumerate(ws))
        den = sum(ws)
        g = g_ref[rows, :]
        u_ref[rows, :] = ((num / den) * _silu(g)).astype(u_ref.dtype)
        return 0

    lax.fori_loop(0, seq // chunk, combine, 0)


def _attn_prompt(a6, pbias, w_out, l, batch, seq):
    blk = (None, None, seq, HEAD_DIM)
    w_rows = D_MODEL // (batch * N_HEADS)
    return pl.pallas_call(
        functools.partial(_attn_prompt_kernel, seq=seq),
        grid=(batch, N_HEADS),
        in_specs=[pl.BlockSpec(blk, lambda b, h, slot=slot: (slot, b, 0, h)) for slot in range(4)] + [
            pl.BlockSpec((N_GROUPS, None, L_BAND, 2 * L_BAND), lambda b, h: (0, h, 0, 0)),
            pl.BlockSpec((None, w_rows, D_MODEL), lambda b, h: (l, b * N_HEADS + h, 0)),
        ],
        out_specs=[
            pl.BlockSpec((None, seq, HEAD_DIM), lambda b, h: (b, 0, h)),
            pl.BlockSpec((w_rows, D_MODEL), lambda b, h: (b * N_HEADS + h, 0)),
        ],
        out_shape=[
            jax.ShapeDtypeStruct((batch, seq, D_ATT), BF16),
            jax.ShapeDtypeStruct((D_MODEL, D_MODEL), BF16),
        ],
        scratch_shapes=[pltpu.VMEM((seq, HEAD_DIM), F32)] * 6 + [pltpu.VMEM((N_GROUPS, seq, HEAD_DIM), F32)] * 2,
        compiler_params=_params(("arbitrary", "arbitrary"), 48),
        name="attn_prompt",
    )(a6, a6, a6, a6, pbias, w_out)


def _attn_sample_kernel(q_ref, kn_ref, vn_ref, ck_ref, cv_ref, g_ref, bias_ref, u_ref, q_scr, k_scr, v_scr,
                        *, c_len, t_new):
    pad = SAMPLE_KEYS - c_len - t_new
    q_scr[t_new:8, :] = jnp.zeros((8 - t_new, HEAD_DIM), F32)
    for slot in range(2):
        k_scr[slot, c_len + t_new:SAMPLE_KEYS, :] = jnp.zeros((pad, HEAD_DIM), F32)
        v_scr[slot, c_len + t_new:SAMPLE_KEYS, :] = jnp.zeros((pad, HEAD_DIM), F32)
    for hh in range(KV_HEADS):
        lanes = slice(hh * HEAD_DIM, (hh + 1) * HEAD_DIM)
        slot = hh % 2
        q_scr[0:t_new, :] = q_ref[:, lanes]
        for new_ref, cache_ref, scr in ((kn_ref, ck_ref, k_scr), (vn_ref, cv_ref, v_scr)):
            heads = cache_ref.reshape(c_len * KV_HEADS, HEAD_DIM)
            scr[slot, 0:c_len, :] = heads[pl.ds(hh, c_len, stride=KV_HEADS), :]
            scr[slot, c_len:c_len + t_new, :] = new_ref[:, lanes]
        kk = k_scr[slot].astype(BF16)
        vv = v_scr[slot].astype(BF16)
        s = lax.dot_general(q_scr[...].astype(BF16), kk, (((1,), (1,)), ((), ())),
                            preferred_element_type=F32) * ATT_SCALE
        ms, ls, ps = [], [], []
        for gi in range(N_GROUPS):
            sg = s + bias_ref[gi, hh]
            mx = jnp.max(sg, axis=-1, keepdims=True)
            p = jnp.exp(sg - mx)
            ms.append(mx)
            ls.append(jnp.sum(p, axis=-1, keepdims=True))
            ps.append(p)
        o_all = jnp.dot(jnp.concatenate(ps, axis=0).astype(BF16), vv, preferred_element_type=F32)
        os_ = [o_all[8 * gi:8 * (gi + 1), :] for gi in range(N_GROUPS)]
        top = functools.reduce(jnp.maximum, ms)
        ws = [jnp.exp(m - top) for m in ms]
        num = sum(w * o for w, o in zip(ws, os_))
        den = sum(w * d for w, d in zip(ws, ls))
        g = g_ref[:, lanes]
        u_ref[:, lanes] = (num / den)[0:t_new, :] * _silu(g)


def _attn_sample(a6, cache_k, cache_v, sbias, l, batch, t_new, c_len):
    new_blk = (None, None, t_new, KV_HEADS * HEAD_DIM)
    cache_blk = (None, None, c_len, KV_HEADS, HEAD_DIM)
    return pl.pallas_call(
        functools.partial(_attn_sample_kernel, c_len=c_len, t_new=t_new),
        grid=(batch, N_HEADS // KV_HEADS),
        in_specs=[
            pl.BlockSpec(new_blk, lambda b, hg: (0, b, 0, hg)),
            pl.BlockSpec(new_blk, lambda b, hg: (1, b, 0, hg)),
            pl.BlockSpec(new_blk, lambda b, hg: (2, b, 0, hg)),
            pl.BlockSpec(cache_blk, lambda b, hg: (l, b, 0, hg, 0)),
            pl.BlockSpec(cache_blk, lambda b, hg: (l, b, 0, hg, 0)),
            pl.BlockSpec(new_blk, lambda b, hg: (3, b, 0, hg)),
            pl.BlockSpec((N_GROUPS, KV_HEADS, 8, SAMPLE_KEYS), lambda b, hg: (0, hg, 0, 0)),
        ],
        out_specs=pl.BlockSpec((None, t_new, KV_HEADS * HEAD_DIM), lambda b, hg: (b, 0, hg)),
        out_shape=jax.ShapeDtypeStruct((batch, t_new, D_ATT), F32),
        scratch_shapes=[pltpu.VMEM((8, HEAD_DIM), F32), pltpu.VMEM((2, SAMPLE_KEYS, HEAD_DIM), F32),
                        pltpu.VMEM((2, SAMPLE_KEYS, HEAD_DIM), F32)],
        compiler_params=_params(("arbitrary", "arbitrary"), 48),
        name="attn_sample",
    )(a6, a6, a6, cache_k, cache_v, a6, sbias)


def _lru_gates(xc, wa_ref, ba_ref, wx_ref, bx_ref, lam_ref):
    xcb = xc.astype(BF16)
    th_r = jnp.tanh(jnp.dot(xcb, (0.5 * wa_ref[...]).astype(BF16), preferred_element_type=F32) + 0.5 * ba_ref[...])
    th_i = jnp.tanh(jnp.dot(xcb, (0.5 * wx_ref[...]).astype(BF16), preferred_element_type=F32) + 0.5 * bx_ref[...])
    nl = -lam_ref[...]
    softplus = jnp.maximum(nl, 0.0) + jnp.log1p(jnp.exp(-jnp.abs(nl)))
    half_log_a = (-0.25 * LRU_C * softplus) * (1.0 + th_r)
    t = jnp.tanh(half_log_a)
    qn = t / (t - 1.0)
    a = 1.0 - 2.0 * qn
    b = jnp.sqrt(qn * (1.0 - qn)) * ((1.0 + th_i) * xc)
    return a, b


def _lru_prompt_kernel(x_ref, g_ref, wc_ref, bc_ref, wa_ref, ba_ref, wx_ref, bx_ref, lam_ref, *rest, seq, cast):
    if cast:
        wsrc_ref, u_ref, h_ref, cs_ref, wdst_ref, xpad, a_scr, b_scr = rest
        wdst_ref[...] = wsrc_ref[...].astype(wdst_ref.dtype)
    else:
        u_ref, h_ref, cs_ref, xpad, a_scr, b_scr = rest
    seg_len = seq // SCAN_SEGS
    lead = (CONV_W - 1) * SCAN_SEGS
    row = lax.broadcasted_iota(jnp.int32, (SCAN_SEGS, LRU_BLOCK), 0)
    for i in range(SCAN_SEGS):
        xpad[pl.ds(lead + i, seg_len, stride=SCAN_SEGS), :] = x_ref[i * seg_len:(i + 1) * seg_len, :]
    for k in range(1, CONV_W):
        tail = xpad[lead + (seg_len - k) * SCAN_SEGS:lead + (seg_len - k + 1) * SCAN_SEGS, :]
        xpad[lead - k * SCAN_SEGS:lead - (k - 1) * SCAN_SEGS, :] = jnp.where(row == 0, 0.0, pltpu.roll(tail, 1, 0))
    xc = bc_ref[...]
    for j in range(CONV_W):
        xc = xc + xpad[j * SCAN_SEGS:j * SCAN_SEGS + seq, :] * wc_ref[j:j + 1, :]
    a, b = _lru_gates(xc, wa_ref, ba_ref, wx_ref, bx_ref, lam_ref)
    a_scr[...] = a
    b_scr[...] = b

    def body(c, carry):
        h_loc, a_cum = carry
        rows = pl.ds(pl.multiple_of(c * SCAN_SEGS, SCAN_SEGS), SCAN_SEGS)
        ca = a_scr[rows, :]
        h_loc = ca * h_loc + b_scr[rows, :]
        a_cum = ca * a_cum
        b_scr[rows, :] = h_loc
        a_scr[rows, :] = a_cum
        return h_loc, a_cum

    init = (jnp.zeros((SCAN_SEGS, LRU_BLOCK), F32), jnp.ones((SCAN_SEGS, LRU_BLOCK), F32))
    h_end, a_end = lax.fori_loop(0, seg_len, body, init, unroll=8)
    carry = jnp.zeros((1, LRU_BLOCK), F32)
    carries = jnp.zeros((SCAN_SEGS, LRU_BLOCK), F32)
    for i in range(1, SCAN_SEGS):
        carry = h_end[i - 1:i, :] + a_end[i - 1:i, :] * carry
        carries = jnp.where(row == i, carry, carries)
    shape3 = (seg_len, SCAN_SEGS, LRU_BLOCK)
    y = b_scr[...].reshape(shape3) + a_scr[...].reshape(shape3) * carries[None]
    b_scr[...] = y.reshape(seq, LRU_BLOCK)
    for i in range(SCAN_SEGS):
        rows = slice(i * seg_len, (i + 1) * seg_len)
        g = g_ref[rows, :]
        u_ref[rows, :] = (b_scr[pl.ds(i, seg_len, stride=SCAN_SEGS), :] * _silu(g)).astype(u_ref.dtype)
    h_ref[...] = b_scr[seq - 1:seq, :]
    cs_ref[...] = x_ref[seq - (CONV_W - 1):seq, :]


def _lru_weight_specs(l, n_of):
    return [
        pl.BlockSpec((None, CONV_W, LRU_BLOCK), lambda *g: (l, 0, n_of(*g))),
        pl.BlockSpec((None, 1, LRU_BLOCK), lambda *g: (l, 0, n_of(*g))),
        pl.BlockSpec((None, None, LRU_BLOCK, LRU_BLOCK), lambda *g: (l, n_of(*g), 0, 0)),
        pl.BlockSpec((None, 1, LRU_BLOCK), lambda *g: (l, 0, n_of(*g))),
        pl.BlockSpec((None, None, LRU_BLOCK, LRU_BLOCK), lambda *g: (l, n_of(*g), 0, 0)),
        pl.BlockSpec((None, 1, LRU_BLOCK), lambda *g: (l, 0, n_of(*g))),
        pl.BlockSpec((None, 1, LRU_BLOCK), lambda *g: (l, 0, n_of(*g))),
    ]


def _lru_prompt(a6, lru_w, l, batch, seq, w_in=None):
    blk = (None, None, seq, LRU_BLOCK)
    n_of = lambda b, n: n
    cast = w_in is not None
    w_rows = D_MODEL // (batch * N_LRU_BLOCKS)
    in_specs = [
        pl.BlockSpec(blk, lambda b, n: (4, b, 0, n)),
        pl.BlockSpec(blk, lambda b, n: (5, b, 0, n)),
    ] + _lru_weight_specs(l, n_of)
    out_specs = [
        pl.BlockSpec((None, seq, LRU_BLOCK), lambda b, n: (b, 0, n)),
        pl.BlockSpec((None, 1, LRU_BLOCK), lambda b, n: (b, 0, n)),
        pl.BlockSpec((None, CONV_W - 1, LRU_BLOCK), lambda b, n: (b, 0, n)),
    ]
    out_shape = [
        jax.ShapeDtypeStruct((batch, seq, D_LRU), BF16),
        jax.ShapeDtypeStruct((batch, 1, D_LRU), F32),
        jax.ShapeDtypeStruct((batch, CONV_W - 1, D_LRU), F32),
    ]
    args = [a6, a6, *lru_w]
    if cast:
        in_specs.append(pl.BlockSpec((None, w_rows, D_IN), lambda b, n: (l + 1, b * N_LRU_BLOCKS + n, 0)))
        out_specs.append(pl.BlockSpec((w_rows, D_IN), lambda b, n: (b * N_LRU_BLOCKS + n, 0)))
        out_shape.append(jax.ShapeDtypeStruct((D_MODEL, D_IN), BF16))
        args.append(w_in)
    res = pl.pallas_call(
        functools.partial(_lru_prompt_kernel, seq=seq, cast=cast),
        grid=(batch, N_LRU_BLOCKS),
        in_specs=in_specs,
        out_specs=out_specs,
        out_shape=out_shape,
        scratch_shapes=[pltpu.VMEM((seq + (CONV_W - 1) * SCAN_SEGS, LRU_BLOCK), F32),
                        pltpu.VMEM((seq, LRU_BLOCK), F32), pltpu.VMEM((seq, LRU_BLOCK), F32)],
        compiler_params=_params(("arbitrary", "arbitrary"), 40),
        name="lru_prompt",
    )(*args)
    return res if cast else (*res, None)


def _lru_sample_kernel(x_ref, g_ref, sc_ref, h0_ref, wc_ref, bc_ref, wa_ref, ba_ref, wx_ref, bx_ref, lam_ref,
                       u_ref, h_ref, cs_ref, xc_scr, a_scr, b_scr, y_scr, *, batch, t_new):
    n_state = CONV_W - 1

    def ext_row(b, i):
        if i < n_state:
            return sc_ref[b, i:i + 1, :]
        return x_ref[b, i - n_state:i - n_state + 1, :]

    for b in range(batch):
        for t in range(t_new):
            acc = bc_ref[...] + ext_row(b, t) * wc_ref[0:1, :]
            for j in range(1, CONV_W):
                acc = acc + ext_row(b, t + j) * wc_ref[j:j + 1, :]
            xc_scr[b * t_new + t:b * t_new + t + 1, :] = acc
    a, bb = _lru_gates(xc_scr[...], wa_ref, ba_ref, wx_ref, bx_ref, lam_ref)
    a_scr[...] = a
    b_scr[...] = bb
    for b in range(batch):
        h = h0_ref[b:b + 1, :]
        for t in range(t_new):
            r = b * t_new + t
            h = a_scr[r:r + 1, :] * h + b_scr[r:r + 1, :]
            y_scr[r:r + 1, :] = h
        h_ref[b:b + 1, :] = h
        for i in range(n_state):
            cs_ref[b, i:i + 1, :] = ext_row(b, t_new + i)
    for b in range(batch):
        g = g_ref[b]
        u_ref[b] = y_scr[b * t_new:(b + 1) * t_new, :] * _silu(g)


def _lru_sample(a6, state_conv, state_h, lru_w, l, batch, t_new):
    blk = (None, batch, t_new, LRU_BLOCK)
    n_of = lambda n: n
    rows = batch * t_new
    return pl.pallas_call(
        functools.partial(_lru_sample_kernel, batch=batch, t_new=t_new),
        grid=(N_LRU_BLOCKS,),
        in_specs=[
            pl.BlockSpec(blk, lambda n: (4, 0, 0, n)),
            pl.BlockSpec(blk, lambda n: (5, 0, 0, n)),
            pl.BlockSpec((None, batch, CONV_W - 1, LRU_BLOCK), lambda n: (l, 0, 0, n)),
            pl.BlockSpec((None, batch, LRU_BLOCK), lambda n: (l, 0, n)),
        ] + _lru_weight_specs(l, n_of),
        out_specs=[
            pl.BlockSpec((batch, t_new, LRU_BLOCK), lambda n: (0, 0, n)),
            pl.BlockSpec((batch, LRU_BLOCK), lambda n: (0, n)),
            pl.BlockSpec((batch, CONV_W - 1, LRU_BLOCK), lambda n: (0, 0, n)),
        ],
        out_shape=[
            jax.ShapeDtypeStruct((batch, t_new, D_LRU), F32),
            jax.ShapeDtypeStruct((batch, D_LRU), F32),
            jax.ShapeDtypeStruct((batch, CONV_W - 1, D_LRU), F32),
        ],
        scratch_shapes=[pltpu.VMEM((rows, LRU_BLOCK), F32)] * 4,
        compiler_params=_params(("arbitrary",), 32),
        name="lru_sample",
    )(a6, a6, state_conv, state_h, *lru_w)


def _outproj_kernel(ua_ref, ul_ref, w_ref, x_ref, gate_ref, gp_ref, *rest, n_tiles, has_next):
    if has_next:
        shift_ref, scale_ref, gn_ref, y_ref, h_ref, acc_a, acc_b = rest
    else:
        y_ref, acc_a, acc_b = rest
    i = pl.program_id(0)
    tm = x_ref.shape[0]
    chunk = min(tm, EPILOGUE_CHUNK)
    per_row = gate_ref.shape[0] == tm

    def matmul(acc):
        acc[...] = (jnp.dot(ua_ref[...].astype(BF16), w_ref[0:D_ATT, :], preferred_element_type=F32)
                    + jnp.dot(ul_ref[...].astype(BF16), w_ref[D_ATT:D_MODEL, :], preferred_element_type=F32))

    def epilogue(acc):
        for c in range(tm // chunk):
            rows = slice(c * chunk, (c + 1) * chunk)
            mrows = rows if per_row else slice(None)
            y = acc[rows, :]
            y = y * lax.rsqrt(jnp.mean(y * y, axis=-1, keepdims=True) + EPS)
            out = x_ref[rows, :] + gate_ref[mrows, :] * (y * gp_ref[...])
            y_ref[rows, :] = out
            if has_next:
                hn = out * lax.rsqrt(jnp.mean(out * out, axis=-1, keepdims=True) + EPS)
                hn = hn * gn_ref[...]
                h_ref[rows, :] = (hn * (1.0 + scale_ref[mrows, :]) + shift_ref[mrows, :]).astype(h_ref.dtype)

    accs = (acc_a, acc_b)

    @pl.when(i == 0)
    def _():
        matmul(accs[0])

    for parity in range(2):
        @pl.when((i > 0) & (i < n_tiles) & (i % 2 == parity))
        def _(parity=parity):
            epilogue(accs[1 - parity])
            matmul(accs[parity])

    @pl.when(i == n_tiles)
    def _():
        epilogue(accs[(n_tiles - 1) % 2])


def _outproj(u_att, u_lru, w_out, x2, gate, mod_idx, g_post, l, nxt, *, tm, rows_per_mod, vmem_mib):
    m_rows = x2.shape[0]
    n_tiles = m_rows // tm
    mm_tile = lambda i: jnp.minimum(i, n_tiles - 1)
    ep_tile = lambda i: jnp.maximum(i - 1, 0)
    row_spec = pl.BlockSpec((tm, D_MODEL), lambda i: (ep_tile(i), 0))
    in_specs = [
        pl.BlockSpec((tm, D_ATT), lambda i: (mm_tile(i), 0)),
        pl.BlockSpec((tm, D_LRU), lambda i: (mm_tile(i), 0)),
        pl.BlockSpec((D_MODEL, D_MODEL), lambda i: (0, 0), pipeline_mode=pl.Buffered(1)),
        row_spec,
        _mod_specs(mod_idx, 2, tm, rows_per_mod, ep_tile),
        pl.BlockSpec((None, 1, D_MODEL), lambda i: (l, 0, 0)),
    ]
    args = [u_att, u_lru, w_out, x2, gate, g_post]
    out_specs = [row_spec]
    out_shape = [jax.ShapeDtypeStruct((m_rows, D_MODEL), F32)]
    if nxt is not None:
        shift_n, scale_n, idx_n, g_pre, l_next = nxt
        in_specs += [
            _mod_specs(idx_n, 0, tm, rows_per_mod, ep_tile),
            _mod_specs(idx_n, 1, tm, rows_per_mod, ep_tile),
            pl.BlockSpec((None, 1, D_MODEL), lambda i: (l_next, 0, 0)),
        ]
        args += [shift_n, scale_n, g_pre]
        out_specs.append(row_spec)
        out_shape.append(jax.ShapeDtypeStruct((m_rows, D_MODEL), BF16))
    res = pl.pallas_call(
        functools.partial(_outproj_kernel, n_tiles=n_tiles, has_next=nxt is not None),
        grid=(n_tiles + 1,),
        in_specs=in_specs,
        out_specs=out_specs,
        out_shape=out_shape,
        scratch_shapes=[pltpu.VMEM((tm, D_MODEL), F32)] * 2,
        compiler_params=_params(("arbitrary",), vmem_mib),
        name="outproj",
    )(*args)
    return res if nxt is not None else (res[0], None)


def kernel(x_prompt, x_sample, cache_k, cache_v, state_h, state_conv, c_prompt, c_sample, rel_table, w_ada, b_ada,
           g_pre, w_in, w_conv, b_conv, w_a, b_a, w_x, b_x, lam, w_out, g_post):
    depth = w_in.shape[0]
    bp, seq, _ = x_prompt.shape
    bs, t_new, _ = x_sample.shape
    c_len = cache_k.shape[2]
    assert bp + bs <= MOD_ROWS and c_len + t_new <= SAMPLE_KEYS and t_new <= 8
    assert seq % (L_BAND * DILATED_GROUPS[-1][1]) == 0 and c_len >= WINDOW_MAX

    c_all = jnp.concatenate([c_prompt, c_sample, jnp.zeros((MOD_ROWS - bp - bs, D_MODEL), F32)], axis=0)
    mod = _ada(c_all, w_ada, b_ada)
    mod_tab = mod.reshape(depth * MOD_ROWS * 3, 1, D_MODEL)
    pbias, sbias = _bias_tables(rel_table, c_len, t_new)

    w_in_l = w_in[0].astype(BF16)
    g_pre3 = g_pre.reshape(depth, 1, D_MODEL)
    g_post3 = g_post.reshape(depth, 1, D_MODEL)

    xp = x_prompt.reshape(bp * seq, D_MODEL)
    xs = x_sample.reshape(bs * t_new, D_MODEL)
    rows_s = bs * t_new
    lru_w = (w_conv, b_conv.reshape(depth, 1, D_LRU), w_a, b_a.reshape(depth, 1, D_LRU),
             w_x, b_x.reshape(depth, 1, D_LRU), lam.reshape(depth, 1, D_LRU))
    kv_p = kv_s = None
    hp_l, cp_l, hs_l, cs_l = [], [], [], []
    tm_norm, tm_in, tm_out = 512, 1024, 128
    p_idx = lambda l: (lambda b, which: (l * MOD_ROWS + b) * 3 + which)
    mod_s = [jnp.repeat(mod[l, bp:bp + bs], t_new, axis=0) for l in range(depth)]
    shift_s = [m[:, :D_MODEL] for m in mod_s]
    scale_s = [m[:, D_MODEL:2 * D_MODEL] for m in mod_s]
    gate_s = [m[:, 2 * D_MODEL:] for m in mod_s]
    hp = _prenorm(xp, mod_tab, mod_tab, p_idx(0), g_pre3, 0, tm=tm_norm, rows_per_mod=seq // tm_norm)
    hs = _prenorm(xs, shift_s[0], scale_s[0], None, g_pre3, 0, tm=rows_s, rows_per_mod=1)
    for l in range(depth):
        more = l + 1 < depth
        a6, kp, vp = _inproj(hp, w_in_l, l, depth, kv_p, tm=tm_in, vmem_mib=56)
        kv_p = (kp, vp)
        a6 = a6.reshape(N_SEG, bp, seq, D_ATT)
        u_att, w_out_l = _attn_prompt(a6, pbias, w_out, l, bp, seq)
        u_lru, h_last, conv, w_in_next = _lru_prompt(a6, lru_w, l, bp, seq, w_in if more else None)
        nxt = (mod_tab, mod_tab, p_idx(l + 1), g_pre3, l + 1) if more else None
        xp, hp = _outproj(u_att.reshape(bp * seq, D_ATT), u_lru.reshape(bp * seq, D_LRU), w_out_l, xp, mod_tab,
                          p_idx(l), g_post3, l, nxt, tm=tm_out, rows_per_mod=seq // tm_out, vmem_mib=56)
        hp_l.append(h_last.reshape(bp, D_LRU))
        cp_l.append(conv)
        a6, ks, vs = _inproj(hs, w_in_l, l, depth, kv_s, tm=rows_s, vmem_mib=40)
        kv_s = (ks, vs)
        a6 = a6.reshape(N_SEG, bs, t_new, D_ATT)
        u_att = _attn_sample(a6, cache_k, cache_v, sbias, l, bs, t_new, c_len)
        u_lru, h_last, conv = _lru_sample(a6, state_conv, state_h, lru_w, l, bs, t_new)
        nxt = (shift_s[l + 1], scale_s[l + 1], None, g_pre3, l + 1) if more else None
        xs, hs = _outproj(u_att.reshape(rows_s, D_ATT), u_lru.reshape(rows_s, D_LRU), w_out_l, xs, gate_s[l], None,
                          g_post3, l, nxt, tm=rows_s, rows_per_mod=1, vmem_mib=48)
        hs_l.append(h_last)
        cs_l.append(conv)
        w_in_l = w_in_next

    kp, vp = kv_p
    ks, vs = kv_s
    return (xp.reshape(bp, seq, D_MODEL), xs.reshape(bs, t_new, D_MODEL),
            kp.reshape(depth, bp, seq, N_HEADS, HEAD_DIM), vp.reshape(depth, bp, seq, N_HEADS, HEAD_DIM),
            jnp.stack(hp_l), jnp.stack(cp_l),
            ks.reshape(depth, bs, t_new, N_HEADS, HEAD_DIM), vs.reshape(depth, bs, t_new, N_HEADS, HEAD_DIM),
            jnp.stack(hs_l), jnp.stack(cs_l))
```

```python
import functools
import math

import numpy as np
import jax
import jax.numpy as jnp
from jax import lax
from jax.experimental import pallas as pl
from jax.experimental.pallas import tpu as pltpu

D_MODEL = 4096
D_ATT = 2048
D_LRU = 2048
HEAD_DIM = 128
N_HEADS = 16
N_LRU_BLOCKS = 16
LRU_BLOCK = 128
CONV_W = 4
LRU_C = 8.0
DILATED_GROUPS = ((128, 1), (512, 4), (2048, 16))
N_GROUPS = len(DILATED_GROUPS)
WINDOW_MAX = 2048
N_BUCKETS = 32
MAX_EXACT = N_BUCKETS // 2
MAX_DISTANCE = WINDOW_MAX
EPS = 1e-6
ATT_SCALE = HEAD_DIM ** -0.5
D_IN = 4 * D_ATT + 2 * D_LRU
N_SEG = D_IN // D_ATT
SEG_K, SEG_V = 1, 2
L_BAND = 128
MASKED = -1e30
MOD_ROWS = 16
SAMPLE_KEYS = 2048 + 128
MIB = 1024 * 1024
NORM_CHUNK = 64
EPILOGUE_CHUNK = 128
ATTN_GROUP = 8
SCAN_SEGS = 8
KV_HEADS = 8

F32 = jnp.float32
BF16 = jnp.bfloat16


def _params(semantics, vmem_mib):
    return pltpu.CompilerParams(dimension_semantics=semantics, vmem_limit_bytes=vmem_mib * MIB)


def _silu(x):
    half = 0.5 * x
    return half + half * jnp.tanh(half)


def _rel_bucket_np(dist):
    d = dist.astype(np.float32)
    large = np.float32(MAX_EXACT) + np.log(np.maximum(d, np.float32(1.0)) / np.float32(MAX_EXACT)) / np.float32(
        math.log(MAX_DISTANCE / MAX_EXACT)) * np.float32(N_BUCKETS - MAX_EXACT)
    large = np.minimum(large.astype(np.int32), N_BUCKETS - 1)
    return np.where(dist < MAX_EXACT, dist, large).astype(np.int32)


def _prompt_bucket_index():
    qi = np.arange(L_BAND)[:, None]
    kj = np.arange(2 * L_BAND)[None, :]
    dist = qi + L_BAND - kj
    band = (dist >= 0) & (dist <= L_BAND)
    out = []
    for _, dil in DILATED_GROUPS:
        b = _rel_bucket_np(np.clip(dist, 0, L_BAND) * dil)
        out.append(np.where(band, b, -1))
    return np.stack(out).astype(np.int32)


def _sample_bucket_index(c_len, t_new):
    t = np.arange(8)[:, None]
    idx = np.arange(SAMPLE_KEYS)[None, :]
    delta = c_len + t - idx
    out = []
    for window, dil in DILATED_GROUPS:
        valid = (delta >= 0) & (delta % dil == 0) & (delta <= window) & (idx < c_len + t_new) & (t < t_new)
        b = _rel_bucket_np(np.clip(delta, 0, window))
        b = np.where(valid, b, -1)
        b = np.where(t >= t_new, 0, b)
        out.append(b)
    return np.stack(out).astype(np.int32)


def _ada_kernel(c_ref, w_ref, b_ref, o_ref):
    c = c_ref[...]
    a = _silu(c).astype(BF16)
    o_ref[...] = jnp.dot(a, w_ref[...].astype(BF16), preferred_element_type=F32) + b_ref[...]


def _ada(c_all, w_ada, b_ada):
    depth = w_ada.shape[0]
    tn = 512
    return pl.pallas_call(
        _ada_kernel,
        grid=(depth, 3 * D_MODEL // tn),
        in_specs=[
            pl.BlockSpec((MOD_ROWS, D_MODEL), lambda l, j: (0, 0)),
            pl.BlockSpec((None, D_MODEL, tn), lambda l, j: (l, 0, j)),
            pl.BlockSpec((None, 1, tn), lambda l, j: (l, 0, j)),
        ],
        out_specs=pl.BlockSpec((None, MOD_ROWS, tn), lambda l, j: (l, 0, j)),
        out_shape=jax.ShapeDtypeStruct((depth, MOD_ROWS, 3 * D_MODEL), F32),
        compiler_params=_params(("arbitrary", "arbitrary"), 40),
        name="ada",
    )(c_all, w_ada, b_ada.reshape(depth, 1, 3 * D_MODEL))


def _bias_kernel(tab_ref, pidx_ref, sidx_ref, pb_ref, sb_ref):
    h = pl.program_id(0)
    for idx_ref, out_ref in ((pidx_ref, pb_ref), (sidx_ref, sb_ref)):
        for g in range(N_GROUPS):
            idx = idx_ref[g]
            acc = jnp.full(idx.shape, MASKED, F32)
            for b in range(N_BUCKETS):
                acc = jnp.where(idx == b, tab_ref[b, h], acc)
            out_ref[g] = acc


def _bias_tables(rel_table, c_len, t_new):
    pidx = jnp.asarray(_prompt_bucket_index())
    sidx = jnp.asarray(_sample_bucket_index(c_len, t_new))
    return pl.pallas_call(
        _bias_kernel,
        grid=(N_HEADS,),
        in_specs=[
            pl.BlockSpec(memory_space=pltpu.SMEM),
            pl.BlockSpec((N_GROUPS, L_BAND, 2 * L_BAND), lambda h: (0, 0, 0)),
            pl.BlockSpec((N_GROUPS, 8, SAMPLE_KEYS), lambda h: (0, 0, 0)),
        ],
        out_specs=[
            pl.BlockSpec((N_GROUPS, None, L_BAND, 2 * L_BAND), lambda h: (0, h, 0, 0)),
            pl.BlockSpec((N_GROUPS, None, 8, SAMPLE_KEYS), lambda h: (0, h, 0, 0)),
        ],
        out_shape=[
            jax.ShapeDtypeStruct((N_GROUPS, N_HEADS, L_BAND, 2 * L_BAND), F32),
            jax.ShapeDtypeStruct((N_GROUPS, N_HEADS, 8, SAMPLE_KEYS), F32),
        ],
        compiler_params=_params(("arbitrary",), 32),
        name="bias_tables",
    )(rel_table, pidx, sidx)


def _mod_specs(mod_idx, which, tm, rows_per_mod, tile_of=lambda i: i):
    if mod_idx is None:
        return pl.BlockSpec((tm, D_MODEL), lambda i, *_: (tile_of(i), 0))
    return pl.BlockSpec((None, 1, D_MODEL), lambda i, *_: (mod_idx(tile_of(i) // rows_per_mod, which), 0, 0))


def _prenorm_kernel(x_ref, shift_ref, scale_ref, g_ref, h_ref):
    tm = x_ref.shape[0]
    chunk = min(tm, NORM_CHUNK)
    per_row = shift_ref.shape[0] == tm

    def norm(c, _):
        rows = pl.ds(pl.multiple_of(c * chunk, chunk), chunk)
        mrows = rows if per_row else slice(None)
        x = x_ref[rows, :]
        y = x * lax.rsqrt(jnp.mean(x * x, axis=-1, keepdims=True) + EPS)
        y = y * g_ref[...]
        h_ref[rows, :] = (y * (1.0 + scale_ref[mrows, :]) + shift_ref[mrows, :]).astype(h_ref.dtype)
        return 0

    lax.fori_loop(0, tm // chunk, norm, 0)


def _prenorm(x2, shift, scale, mod_idx, g_pre, l, *, tm, rows_per_mod):
    m_rows = x2.shape[0]
    return pl.pallas_call(
        _prenorm_kernel,
        grid=(m_rows // tm,),
        in_specs=[
            pl.BlockSpec((tm, D_MODEL), lambda i: (i, 0)),
            _mod_specs(mod_idx, 0, tm, rows_per_mod),
            _mod_specs(mod_idx, 1, tm, rows_per_mod),
            pl.BlockSpec((None, 1, D_MODEL), lambda i: (l, 0, 0)),
        ],
        out_specs=pl.BlockSpec((tm, D_MODEL), lambda i: (i, 0)),
        out_shape=jax.ShapeDtypeStruct((m_rows, D_MODEL), BF16),
        compiler_params=_params(("arbitrary",), 40),
        name="prenorm",
    )(x2, shift, scale, g_pre)


def _inproj_kernel(h_ref, w_ref, *rest, nj_seg, layer):
    a_ref, k_hbm, v_hbm, stage, sem = rest[-5:]
    i, j = pl.program_id(0), pl.program_id(1)
    tm = h_ref.shape[0]
    seg, part = j // nj_seg, j % nj_seg
    first_kv, last_kv = SEG_K * nj_seg, (SEG_V + 1) * nj_seg - 1

    def head_copy(dst_hbm, group):
        rows = pl.ds(i * tm, tm)
        heads = pl.ds(group * KV_HEADS, KV_HEADS)
        return pltpu.make_async_copy(stage, dst_hbm.at[layer, rows, heads, :], sem.at[0])

    acc = jnp.dot(h_ref[...], w_ref[...], preferred_element_type=F32)
    a_ref[...] = acc

    @pl.when((j > first_kv) & (j <= last_kv + 1))
    def _():
        head_copy(k_hbm, 0).wait()

    def stage_and_send(dst_hbm):
        flat = stage.reshape(tm * KV_HEADS, HEAD_DIM)
        for hh in range(KV_HEADS):
            flat[pl.ds(hh, tm, stride=KV_HEADS), :] = acc[:, hh * HEAD_DIM:(hh + 1) * HEAD_DIM]
        head_copy(dst_hbm, part).start()

    @pl.when(seg == SEG_K)
    def _():
        stage_and_send(k_hbm)

    @pl.when(seg == SEG_V)
    def _():
        stage_and_send(v_hbm)


def _inproj(h2, w_in, l, depth, kv_prev, *, tm, vmem_mib):
    m_rows = h2.shape[0]
    tn = KV_HEADS * HEAD_DIM
    nj_seg = D_ATT // tn
    in_specs = [
        pl.BlockSpec((tm, D_MODEL), lambda i, j: (i, 0)),
        pl.BlockSpec((D_MODEL, tn), lambda i, j: (0, j)),
    ]
    args = [h2, w_in]
    aliases = {}
    if kv_prev is not None:
        in_specs += [pl.BlockSpec(memory_space=pl.ANY)] * 2
        args += list(kv_prev)
        aliases = {2: 1, 3: 2}
    kv_shape = jax.ShapeDtypeStruct((depth, m_rows, N_HEADS, HEAD_DIM), F32)
    return pl.pallas_call(
        functools.partial(_inproj_kernel, nj_seg=nj_seg, layer=l),
        grid=(m_rows // tm, N_SEG * nj_seg),
        in_specs=in_specs,
        out_specs=[
            pl.BlockSpec((None, tm, tn), lambda i, j: (j // nj_seg, i, j % nj_seg)),
            pl.BlockSpec(memory_space=pl.ANY),
            pl.BlockSpec(memory_space=pl.ANY),
        ],
        out_shape=[jax.ShapeDtypeStruct((N_SEG, m_rows, D_ATT), F32), kv_shape, kv_shape],
        scratch_shapes=[pltpu.VMEM((tm, KV_HEADS, HEAD_DIM), F32), pltpu.SemaphoreType.DMA((1,))],
        input_output_aliases=aliases,
        compiler_params=_params(("arbitrary", "arbitrary"), vmem_mib),
        name="inproj",
    )(*args)


def _attn_prompt_kernel(q_ref, k_ref, v_ref, g_ref, bias_ref, wsrc_ref, *rest, seq, cast_next):
    if cast_next:
        wsrc2_ref, u_ref, wdst_ref, wdst2_ref, qa, ka, va, qb, kb, vb, o_scr, e_scr = rest
        wdst2_ref[...] = wsrc2_ref[...].astype(wdst2_ref.dtype)
    else:
        u_ref, wdst_ref, qa, ka, va, qb, kb, vb, o_scr, e_scr = rest
    wdst_ref[...] = wsrc_ref[...].astype(wdst_ref.dtype)
    n_blk = seq // L_BAND

    def regroup(dst, src, stride):
        sub = seq // stride
        for r in range(stride):
            dst[r * sub:(r + 1) * sub, :] = src[pl.ds(r, sub, stride=stride), :]

    def branch(gi, dil, q_src, k_src, v_src):
        per_res = n_blk // dil
        grp = ATTN_GROUP
        assert n_blk % grp == 0 and (grp % per_res == 0 or per_res % grp == 0)

        def blocks(src, lo, hi):
            return src[lo * L_BAND:hi * L_BAND, :].astype(BF16).reshape(hi - lo, L_BAND, HEAD_DIM)

        for g0 in range(0, n_blk, grp):
            q3 = blocks(q_src, g0, g0 + grp)
            k3 = blocks(k_src, g0, g0 + grp)
            v3 = blocks(v_src, g0, g0 + grp)
            if per_res > 1:
                if g0 == 0:
                    kp = jnp.concatenate([k3[:1], k3[:-1]], axis=0)
                    vp = jnp.concatenate([v3[:1], v3[:-1]], axis=0)
                else:
                    kp = blocks(k_src, g0 - 1, g0 + grp - 1)
                    vp = blocks(v_src, g0 - 1, g0 + grp - 1)
                kk = jnp.concatenate([kp, k3], axis=1)
                vv = jnp.concatenate([vp, v3], axis=1)
                blk = lax.broadcasted_iota(jnp.int32, (grp, 1, 2 * L_BAND), 0) + g0
                col = lax.broadcasted_iota(jnp.int32, (grp, 1, 2 * L_BAND), 2)
                first = jnp.where((blk % per_res == 0) & (col < L_BAND), MASKED, 0.0)
                bias = bias_ref[gi][None] + first
            else:
                kk, vv = k3, v3
                bias = bias_ref[gi, :, L_BAND:][None]
            s = jnp.einsum('bqd,bkd->bqk', q3, kk, preferred_element_type=F32) * ATT_SCALE + bias
            mx = jnp.max(s, axis=-1, keepdims=True)
            p = jnp.exp(s - mx)
            den = jnp.sum(p, axis=-1, keepdims=True)
            o = jnp.einsum('bqk,bkd->bqd', p.astype(BF16), vv, preferred_element_type=F32) / den
            lse = jnp.broadcast_to(mx + jnp.log(den), (grp, L_BAND, HEAD_DIM))
            o = o.reshape(grp * L_BAND, HEAD_DIM)
            lse = lse.reshape(grp * L_BAND, HEAD_DIM)
            if dil == 1:
                o_scr[gi, g0 * L_BAND:(g0 + grp) * L_BAND, :] = o
                e_scr[gi, g0 * L_BAND:(g0 + grp) * L_BAND, :] = lse
            else:
                sub = seq // dil
                for r in range(g0 // per_res, (g0 + grp) // per_res):
                    part = slice((r * per_res - g0) * L_BAND, ((r + 1) * per_res - g0) * L_BAND)
                    o_scr[gi, pl.ds(r, sub, stride=dil), :] = o[part, :]
                    e_scr[gi, pl.ds(r, sub, stride=dil), :] = lse[part, :]

    (_, d0), (_, d1), (_, d2) = DILATED_GROUPS
    assert d0 == 1 and d2 == d1 * d1
    branch(0, d0, q_ref, k_ref, v_ref)
    for dst, src in ((qa, q_ref), (ka, k_ref), (va, v_ref)):
        regroup(dst, src, d1)
    branch(1, d1, qa, ka, va)
    for dst, src in ((qb, qa), (kb, ka), (vb, va)):
        regroup(dst, src, d1)
    branch(2, d2, qb, kb, vb)

    chunk = 256

    def combine(c, _):
        rows = pl.ds(pl.multiple_of(c * chunk, chunk), chunk)
        es = [e_scr[gi, rows, :] for gi in range(N_GROUPS)]
        top = functools.reduce(jnp.maximum, es)
        ws = [jnp.exp(e - top) for e in es]
        num = sum(w * o_scr[gi, rows, :] for gi, w in enumerate(ws))
        den = sum(ws)
        g = g_ref[rows, :]
        u_ref[rows, :] = ((num / den) * _silu(g)).astype(u_ref.dtype)
        return 0

    lax.fori_loop(0, seq // chunk, combine, 0)


def _attn_prompt(a6, pbias, w_out, l, batch, seq, w_in=None):
    blk = (None, None, seq, HEAD_DIM)
    steps = batch * N_HEADS
    w_rows = D_MODEL // steps
    half_rows = D_MODEL // (2 * steps)
    step = lambda b, h: b * N_HEADS + h
    in_specs = [pl.BlockSpec(blk, lambda b, h, slot=slot: (slot, b, 0, h)) for slot in range(4)] + [
        pl.BlockSpec((N_GROUPS, None, L_BAND, 2 * L_BAND), lambda b, h: (0, h, 0, 0)),
        pl.BlockSpec((None, w_rows, D_MODEL), lambda b, h: (l, step(b, h), 0)),
    ]
    out_specs = [
        pl.BlockSpec((None, seq, HEAD_DIM), lambda b, h: (b, 0, h)),
        pl.BlockSpec((w_rows, D_MODEL), lambda b, h: (step(b, h), 0)),
    ]
    out_shape = [
        jax.ShapeDtypeStruct((batch, seq, D_ATT), BF16),
        jax.ShapeDtypeStruct((D_MODEL, D_MODEL), BF16),
    ]
    args = [a6, a6, a6, a6, pbias, w_out]
    if w_in is not None:
        in_specs.append(pl.BlockSpec((None, half_rows, D_IN), lambda b, h: (l + 1, step(b, h), 0)))
        out_specs.append(pl.BlockSpec((half_rows, D_IN), lambda b, h: (step(b, h), 0)))
        out_shape.append(jax.ShapeDtypeStruct((D_MODEL, D_IN), BF16))
        args.append(w_in)
    res = pl.pallas_call(
        functools.partial(_attn_prompt_kernel, seq=seq, cast_next=w_in is not None),
        grid=(batch, N_HEADS),
        in_specs=in_specs,
        out_specs=out_specs,
        out_shape=out_shape,
        scratch_shapes=[pltpu.VMEM((seq, HEAD_DIM), F32)] * 6 + [pltpu.VMEM((N_GROUPS, seq, HEAD_DIM), F32)] * 2,
        compiler_params=_params(("arbitrary", "arbitrary"), 48),
        name="attn_prompt",
    )(*args)
    return res if w_in is not None else (*res, None)


def _attn_sample_kernel(q_ref, kn_ref, vn_ref, ck_ref, cv_ref, g_ref, bias_ref, u_ref, q_scr, k_scr, v_scr,
                        *, c_len, t_new):
    tail = SAMPLE_KEYS - c_len
    q_scr[t_new:8, :] = jnp.zeros((8 - t_new, HEAD_DIM), F32)
    k_scr[t_new:tail, :] = jnp.zeros((tail - t_new, HEAD_DIM), F32)
    v_scr[t_new:tail, :] = jnp.zeros((tail - t_new, HEAD_DIM), F32)
    for hh in range(KV_HEADS):
        lanes = slice(hh * HEAD_DIM, (hh + 1) * HEAD_DIM)
        q_scr[0:t_new, :] = q_ref[:, lanes]
        keys = []
        for new_ref, cache_ref, scr in ((kn_ref, ck_ref, k_scr), (vn_ref, cv_ref, v_scr)):
            heads = cache_ref.reshape(c_len * KV_HEADS, HEAD_DIM)
            scr[0:t_new, :] = new_ref[:, lanes]
            keys.append(jnp.concatenate([heads[pl.ds(hh, c_len, stride=KV_HEADS), :].astype(BF16),
                                         scr[...].astype(BF16)], axis=0))
        kk, vv = keys
        s = lax.dot_general(q_scr[...].astype(BF16), kk, (((1,), (1,)), ((), ())),
                            preferred_element_type=F32) * ATT_SCALE
        ms, ls, ps = [], [], []
        for gi in range(N_GROUPS):
            sg = s + bias_ref[gi, hh]
            mx = jnp.max(sg, axis=-1, keepdims=True)
            p = jnp.exp(sg - mx)
            ms.append(mx)
            ls.append(jnp.sum(p, axis=-1, keepdims=True))
            ps.append(p)
        o_all = jnp.dot(jnp.concatenate(ps, axis=0).astype(BF16), vv, preferred_element_type=F32)
        os_ = [o_all[8 * gi:8 * (gi + 1), :] for gi in range(N_GROUPS)]
        top = functools.reduce(jnp.maximum, ms)
        ws = [jnp.exp(m - top) for m in ms]
        num = sum(w * o for w, o in zip(ws, os_))
        den = sum(w * d for w, d in zip(ws, ls))
        g = g_ref[:, lanes]
        u_ref[:, lanes] = (num / den)[0:t_new, :] * _silu(g)


def _attn_sample(a6, cache_k, cache_v, sbias, l, batch, t_new, c_len):
    new_blk = (None, None, t_new, KV_HEADS * HEAD_DIM)
    cache_blk = (None, None, c_len, KV_HEADS, HEAD_DIM)
    return pl.pallas_call(
        functools.partial(_attn_sample_kernel, c_len=c_len, t_new=t_new),
        grid=(batch, N_HEADS // KV_HEADS),
        in_specs=[
            pl.BlockSpec(new_blk, lambda b, hg: (0, b, 0, hg)),
            pl.BlockSpec(new_blk, lambda b, hg: (1, b, 0, hg)),
            pl.BlockSpec(new_blk, lambda b, hg: (2, b, 0, hg)),
            pl.BlockSpec(cache_blk, lambda b, hg: (l, b, 0, hg, 0)),
            pl.BlockSpec(cache_blk, lambda b, hg: (l, b, 0, hg, 0)),
            pl.BlockSpec(new_blk, lambda b, hg: (3, b, 0, hg)),
            pl.BlockSpec((N_GROUPS, KV_HEADS, 8, SAMPLE_KEYS), lambda b, hg: (0, hg, 0, 0)),
        ],
        out_specs=pl.BlockSpec((None, t_new, KV_HEADS * HEAD_DIM), lambda b, hg: (b, 0, hg)),
        out_shape=jax.ShapeDtypeStruct((batch, t_new, D_ATT), F32),
        scratch_shapes=[pltpu.VMEM((8, HEAD_DIM), F32), pltpu.VMEM((SAMPLE_KEYS - c_len, HEAD_DIM), F32),
                        pltpu.VMEM((SAMPLE_KEYS - c_len, HEAD_DIM), F32)],
        compiler_params=_params(("arbitrary", "arbitrary"), 48),
        name="attn_sample",
    )(a6, a6, a6, cache_k, cache_v, a6, sbias)


def _lru_gates(xc, wa_ref, ba_ref, wx_ref, bx_ref, lam_ref):
    xcb = xc.astype(BF16)
    th_r = jnp.tanh(jnp.dot(xcb, (0.5 * wa_ref[...]).astype(BF16), preferred_element_type=F32) + 0.5 * ba_ref[...])
    th_i = jnp.tanh(jnp.dot(xcb, (0.5 * wx_ref[...]).astype(BF16), preferred_element_type=F32) + 0.5 * bx_ref[...])
    nl = -lam_ref[...]
    softplus = jnp.maximum(nl, 0.0) + jnp.log1p(jnp.exp(-jnp.abs(nl)))
    half_log_a = (-0.25 * LRU_C * softplus) * (1.0 + th_r)
    t = jnp.tanh(half_log_a)
    qn = t / (t - 1.0)
    a = 1.0 - 2.0 * qn
    w = qn * (1.0 - qn)
    root = jnp.where(w > 0.0, w * lax.rsqrt(w), 0.0)
    b = root * ((1.0 + th_i) * xc)
    return a, b


def _lru_prompt_kernel(x_ref, g_ref, wc_ref, bc_ref, wa_ref, ba_ref, wx_ref, bx_ref, lam_ref, *rest, seq, cast):
    if cast:
        wsrc_ref, _, u_ref, h_ref, cs_ref, wdst_ref, xpad, a_scr, b_scr = rest
        wdst_ref[...] = wsrc_ref[...].astype(wdst_ref.dtype)
    else:
        u_ref, h_ref, cs_ref, xpad, a_scr, b_scr = rest
    seg_len = seq // SCAN_SEGS
    lead = (CONV_W - 1) * SCAN_SEGS
    row = lax.broadcasted_iota(jnp.int32, (SCAN_SEGS, LRU_BLOCK), 0)
    for i in range(SCAN_SEGS):
        xpad[pl.ds(lead + i, seg_len, stride=SCAN_SEGS), :] = x_ref[i * seg_len:(i + 1) * seg_len, :]
    for k in range(1, CONV_W):
        tail = xpad[lead + (seg_len - k) * SCAN_SEGS:lead + (seg_len - k + 1) * SCAN_SEGS, :]
        xpad[lead - k * SCAN_SEGS:lead - (k - 1) * SCAN_SEGS, :] = jnp.where(row == 0, 0.0, pltpu.roll(tail, 1, 0))
    xc = bc_ref[...]
    for j in range(CONV_W):
        xc = xc + xpad[j * SCAN_SEGS:j * SCAN_SEGS + seq, :] * wc_ref[j:j + 1, :]
    a, b = _lru_gates(xc, wa_ref, ba_ref, wx_ref, bx_ref, lam_ref)
    a_scr[...] = a
    b_scr[...] = b

    def body(c, carry):
        h_loc, a_cum = carry
        rows = pl.ds(pl.multiple_of(c * SCAN_SEGS, SCAN_SEGS), SCAN_SEGS)
        ca = a_scr[rows, :]
        h_loc = ca * h_loc + b_scr[rows, :]
        a_cum = ca * a_cum
        b_scr[rows, :] = h_loc
        a_scr[rows, :] = a_cum
        return h_loc, a_cum

    init = (jnp.zeros((SCAN_SEGS, LRU_BLOCK), F32), jnp.ones((SCAN_SEGS, LRU_BLOCK), F32))
    h_end, a_end = lax.fori_loop(0, seg_len, body, init, unroll=8)
    carry = jnp.zeros((1, LRU_BLOCK), F32)
    carries = jnp.zeros((SCAN_SEGS, LRU_BLOCK), F32)
    for i in range(1, SCAN_SEGS):
        carry = h_end[i - 1:i, :] + a_end[i - 1:i, :] * carry
        carries = jnp.where(row == i, carry, carries)
    shape3 = (seg_len, SCAN_SEGS, LRU_BLOCK)
    y = b_scr[...].reshape(shape3) + a_scr[...].reshape(shape3) * carries[None]
    b_scr[...] = y.reshape(seq, LRU_BLOCK)
    for i in range(SCAN_SEGS):
        rows = slice(i * seg_len, (i + 1) * seg_len)
        g = g_ref[rows, :]
        u_ref[rows, :] = (b_scr[pl.ds(i, seg_len, stride=SCAN_SEGS), :] * _silu(g)).astype(u_ref.dtype)
    h_ref[...] = b_scr[seq - 1:seq, :]
    cs_ref[...] = x_ref[seq - (CONV_W - 1):seq, :]


def _lru_weight_specs(l, n_of):
    return [
        pl.BlockSpec((None, CONV_W, LRU_BLOCK), lambda *g: (l, 0, n_of(*g))),
        pl.BlockSpec((None, 1, LRU_BLOCK), lambda *g: (l, 0, n_of(*g))),
        pl.BlockSpec((None, None, LRU_BLOCK, LRU_BLOCK), lambda *g: (l, n_of(*g), 0, 0)),
        pl.BlockSpec((None, 1, LRU_BLOCK), lambda *g: (l, 0, n_of(*g))),
        pl.BlockSpec((None, None, LRU_BLOCK, LRU_BLOCK), lambda *g: (l, n_of(*g), 0, 0)),
        pl.BlockSpec((None, 1, LRU_BLOCK), lambda *g: (l, 0, n_of(*g))),
        pl.BlockSpec((None, 1, LRU_BLOCK), lambda *g: (l, 0, n_of(*g))),
    ]


def _lru_prompt(a6, lru_w, l, batch, seq, w_in=None, w_half=None):
    blk = (None, None, seq, LRU_BLOCK)
    n_of = lambda b, n: n
    cast = w_in is not None
    steps = batch * N_LRU_BLOCKS
    w_rows = D_MODEL // (2 * steps)
    w_block = lambda b, n: steps + b * N_LRU_BLOCKS + n
    in_specs = [
        pl.BlockSpec(blk, lambda b, n: (4, b, 0, n)),
        pl.BlockSpec(blk, lambda b, n: (5, b, 0, n)),
    ] + _lru_weight_specs(l, n_of)
    out_specs = [
        pl.BlockSpec((None, seq, LRU_BLOCK), lambda b, n: (b, 0, n)),
        pl.BlockSpec((None, 1, LRU_BLOCK), lambda b, n: (b, 0, n)),
        pl.BlockSpec((None, CONV_W - 1, LRU_BLOCK), lambda b, n: (b, 0, n)),
    ]
    out_shape = [
        jax.ShapeDtypeStruct((batch, seq, D_LRU), BF16),
        jax.ShapeDtypeStruct((batch, 1, D_LRU), F32),
        jax.ShapeDtypeStruct((batch, CONV_W - 1, D_LRU), F32),
    ]
    args = [a6, a6, *lru_w]
    if cast:
        in_specs.append(pl.BlockSpec((None, w_rows, D_IN), lambda b, n: (l + 1, w_block(b, n), 0)))
        in_specs.append(pl.BlockSpec(memory_space=pl.ANY))
        out_specs.append(pl.BlockSpec((w_rows, D_IN), lambda b, n: (w_block(b, n), 0)))
        out_shape.append(jax.ShapeDtypeStruct((D_MODEL, D_IN), BF16))
        args += [w_in, w_half]
    res = pl.pallas_call(
        functools.partial(_lru_prompt_kernel, seq=seq, cast=cast),
        grid=(batch, N_LRU_BLOCKS),
        in_specs=in_specs,
        out_specs=out_specs,
        out_shape=out_shape,
        scratch_shapes=[pltpu.VMEM((seq + (CONV_W - 1) * SCAN_SEGS, LRU_BLOCK), F32),
                        pltpu.VMEM((seq, LRU_BLOCK), F32), pltpu.VMEM((seq, LRU_BLOCK), F32)],
        input_output_aliases={len(args) - 1: len(out_shape) - 1} if cast else {},
        compiler_params=_params(("arbitrary", "arbitrary"), 40),
        name="lru_prompt",
    )(*args)
    return res if cast else (*res, None)


def _lru_sample_kernel(x_ref, g_ref, sc_ref, h0_ref, wc_ref, bc_ref, wa_ref, ba_ref, wx_ref, bx_ref, lam_ref,
                       u_ref, h_ref, cs_ref, xc_scr, a_scr, b_scr, y_scr, *, batch, t_new):
    n_state = CONV_W - 1

    def ext_row(b, i):
        if i < n_state:
            return sc_ref[b, i:i + 1, :]
        return x_ref[b, i - n_state:i - n_state + 1, :]

    for b in range(batch):
        for t in range(t_new):
            acc = bc_ref[...] + ext_row(b, t) * wc_ref[0:1, :]
            for j in range(1, CONV_W):
                acc = acc + ext_row(b, t + j) * wc_ref[j:j + 1, :]
            xc_scr[b * t_new + t:b * t_new + t + 1, :] = acc
    a, bb = _lru_gates(xc_scr[...], wa_ref, ba_ref, wx_ref, bx_ref, lam_ref)
    a_scr[...] = a
    b_scr[...] = bb
    for b in range(batch):
        h = h0_ref[b:b + 1, :]
        for t in range(t_new):
            r = b * t_new + t
            h = a_scr[r:r + 1, :] * h + b_scr[r:r + 1, :]
            y_scr[r:r + 1, :] = h
        h_ref[b:b + 1, :] = h
        for i in range(n_state):
            cs_ref[b, i:i + 1, :] = ext_row(b, t_new + i)
    for b in range(batch):
        g = g_ref[b]
        u_ref[b] = y_scr[b * t_new:(b + 1) * t_new, :] * _silu(g)


def _lru_sample(a6, state_conv, state_h, lru_w, l, batch, t_new):
    blk = (None, batch, t_new, LRU_BLOCK)
    n_of = lambda n: n
    rows = batch * t_new
    return pl.pallas_call(
        functools.partial(_lru_sample_kernel, batch=batch, t_new=t_new),
        grid=(N_LRU_BLOCKS,),
        in_specs=[
            pl.BlockSpec(blk, lambda n: (4, 0, 0, n)),
            pl.BlockSpec(blk, lambda n: (5, 0, 0, n)),
            pl.BlockSpec((None, batch, CONV_W - 1, LRU_BLOCK), lambda n: (l, 0, 0, n)),
            pl.BlockSpec((None, batch, LRU_BLOCK), lambda n: (l, 0, n)),
        ] + _lru_weight_specs(l, n_of),
        out_specs=[
            pl.BlockSpec((batch, t_new, LRU_BLOCK), lambda n: (0, 0, n)),
            pl.BlockSpec((batch, LRU_BLOCK), lambda n: (0, n)),
            pl.BlockSpec((batch, CONV_W - 1, LRU_BLOCK), lambda n: (0, 0, n)),
        ],
        out_shape=[
            jax.ShapeDtypeStruct((batch, t_new, D_LRU), F32),
            jax.ShapeDtypeStruct((batch, D_LRU), F32),
            jax.ShapeDtypeStruct((batch, CONV_W - 1, D_LRU), F32),
        ],
        scratch_shapes=[pltpu.VMEM((rows, LRU_BLOCK), F32)] * 4,
        compiler_params=_params(("arbitrary",), 32),
        name="lru_sample",
    )(a6, a6, state_conv, state_h, *lru_w)


def _outproj_kernel(ua_ref, ul_ref, w_ref, x_ref, gate_ref, gp_ref, *rest, n_tiles, has_next):
    if has_next:
        shift_ref, scale_ref, gn_ref, y_ref, h_ref, acc_a, acc_b = rest
    else:
        y_ref, acc_a, acc_b = rest
    i = pl.program_id(0)
    tm = x_ref.shape[0]
    chunk = min(tm, EPILOGUE_CHUNK)
    per_row = gate_ref.shape[0] == tm

    def matmul(acc):
        acc[...] = (jnp.dot(ua_ref[...].astype(BF16), w_ref[0:D_ATT, :], preferred_element_type=F32)
                    + jnp.dot(ul_ref[...].astype(BF16), w_ref[D_ATT:D_MODEL, :], preferred_element_type=F32))

    def epilogue(acc):
        for c in range(tm // chunk):
            rows = slice(c * chunk, (c + 1) * chunk)
            mrows = rows if per_row else slice(None)
            y = acc[rows, :]
            y = y * lax.rsqrt(jnp.mean(y * y, axis=-1, keepdims=True) + EPS)
            out = x_ref[rows, :] + gate_ref[mrows, :] * (y * gp_ref[...])
            y_ref[rows, :] = out
            if has_next:
                hn = out * lax.rsqrt(jnp.mean(out * out, axis=-1, keepdims=True) + EPS)
                hn = hn * gn_ref[...]
                h_ref[rows, :] = (hn * (1.0 + scale_ref[mrows, :]) + shift_ref[mrows, :]).astype(h_ref.dtype)

    accs = (acc_a, acc_b)

    @pl.when(i == 0)
    def _():
        matmul(accs[0])

    for parity in range(2):
        @pl.when((i > 0) & (i < n_tiles) & (i % 2 == parity))
        def _(parity=parity):
            epilogue(accs[1 - parity])
            matmul(accs[parity])

    @pl.when(i == n_tiles)
    def _():
        epilogue(accs[(n_tiles - 1) % 2])


def _outproj(u_att, u_lru, w_out, x2, gate, mod_idx, g_post, l, nxt, *, tm, rows_per_mod, vmem_mib):
    m_rows = x2.shape[0]
    n_tiles = m_rows // tm
    mm_tile = lambda i: jnp.minimum(i, n_tiles - 1)
    ep_tile = lambda i: jnp.maximum(i - 1, 0)
    row_spec = pl.BlockSpec((tm, D_MODEL), lambda i: (ep_tile(i), 0))
    in_specs = [
        pl.BlockSpec((tm, D_ATT), lambda i: (mm_tile(i), 0)),
        pl.BlockSpec((tm, D_LRU), lambda i: (mm_tile(i), 0)),
        pl.BlockSpec((D_MODEL, D_MODEL), lambda i: (0, 0), pipeline_mode=pl.Buffered(1)),
        row_spec,
        _mod_specs(mod_idx, 2, tm, rows_per_mod, ep_tile),
        pl.BlockSpec((None, 1, D_MODEL), lambda i: (l, 0, 0)),
    ]
    args = [u_att, u_lru, w_out, x2, gate, g_post]
    out_specs = [row_spec]
    out_shape = [jax.ShapeDtypeStruct((m_rows, D_MODEL), F32)]
    if nxt is not None:
        shift_n, scale_n, idx_n, g_pre, l_next = nxt
        in_specs += [
            _mod_specs(idx_n, 0, tm, rows_per_mod, ep_tile),
            _mod_specs(idx_n, 1, tm, rows_per_mod, ep_tile),
            pl.BlockSpec((None, 1, D_MODEL), lambda i: (l_next, 0, 0)),
        ]
        args += [shift_n, scale_n, g_pre]
        out_specs.append(row_spec)
        out_shape.append(jax.ShapeDtypeStruct((m_rows, D_MODEL), BF16))
    res = pl.pallas_call(
        functools.partial(_outproj_kernel, n_tiles=n_tiles, has_next=nxt is not None),
        grid=(n_tiles + 1,),
        in_specs=in_specs,
        out_specs=out_specs,
        out_shape=out_shape,
        scratch_shapes=[pltpu.VMEM((tm, D_MODEL), F32)] * 2,
        compiler_params=_params(("arbitrary",), vmem_mib),
        name="outproj",
    )(*args)
    return res if nxt is not None else (res[0], None)


def kernel(x_prompt, x_sample, cache_k, cache_v, state_h, state_conv, c_prompt, c_sample, rel_table, w_ada, b_ada,
           g_pre, w_in, w_conv, b_conv, w_a, b_a, w_x, b_x, lam, w_out, g_post):
    depth = w_in.shape[0]
    bp, seq, _ = x_prompt.shape
    bs, t_new, _ = x_sample.shape
    c_len = cache_k.shape[2]
    assert bp + bs <= MOD_ROWS and c_len + t_new <= SAMPLE_KEYS and t_new <= 8
    assert seq % (L_BAND * DILATED_GROUPS[-1][1]) == 0 and c_len >= WINDOW_MAX

    c_all = jnp.concatenate([c_prompt, c_sample, jnp.zeros((MOD_ROWS - bp - bs, D_MODEL), F32)], axis=0)
    mod = _ada(c_all, w_ada, b_ada)
    mod_tab = mod.reshape(depth * MOD_ROWS * 3, 1, D_MODEL)
    pbias, sbias = _bias_tables(rel_table, c_len, t_new)

    w_in_l = w_in[0].astype(BF16)
    g_pre3 = g_pre.reshape(depth, 1, D_MODEL)
    g_post3 = g_post.reshape(depth, 1, D_MODEL)

    xp = x_prompt.reshape(bp * seq, D_MODEL)
    xs = x_sample.reshape(bs * t_new, D_MODEL)
    rows_s = bs * t_new
    lru_w = (w_conv, b_conv.reshape(depth, 1, D_LRU), w_a, b_a.reshape(depth, 1, D_LRU),
             w_x, b_x.reshape(depth, 1, D_LRU), lam.reshape(depth, 1, D_LRU))
    kv_p = kv_s = None
    hp_l, cp_l, hs_l, cs_l = [], [], [], []
    tm_norm, tm_in, tm_out = 512, 1024, 128
    p_idx = lambda l: (lambda b, which: (l * MOD_ROWS + b) * 3 + which)
    mod_s = [jnp.repeat(mod[l, bp:bp + bs], t_new, axis=0) for l in range(depth)]
    shift_s = [m[:, :D_MODEL] for m in mod_s]
    scale_s = [m[:, D_MODEL:2 * D_MODEL] for m in mod_s]
    gate_s = [m[:, 2 * D_MODEL:] for m in mod_s]
    hp = _prenorm(xp, mod_tab, mod_tab, p_idx(0), g_pre3, 0, tm=tm_norm, rows_per_mod=seq // tm_norm)
    hs = _prenorm(xs, shift_s[0], scale_s[0], None, g_pre3, 0, tm=rows_s, rows_per_mod=1)
    for l in range(depth):
        more = l + 1 < depth
        a6, kp, vp = _inproj(hp, w_in_l, l, depth, kv_p, tm=tm_in, vmem_mib=56)
        kv_p = (kp, vp)
        a6 = a6.reshape(N_SEG, bp, seq, D_ATT)
        u_att, w_out_l, w_half = _attn_prompt(a6, pbias, w_out, l, bp, seq, w_in if more else None)
        u_lru, h_last, conv, w_in_next = _lru_prompt(a6, lru_w, l, bp, seq, w_in if more else None, w_half)
        nxt = (mod_tab, mod_tab, p_idx(l + 1), g_pre3, l + 1) if more else None
        xp, hp = _outproj(u_att.reshape(bp * seq, D_ATT), u_lru.reshape(bp * seq, D_LRU), w_out_l, xp, mod_tab,
                          p_idx(l), g_post3, l, nxt, tm=tm_out, rows_per_mod=seq // tm_out, vmem_mib=56)
        hp_l.append(h_last.reshape(bp, D_LRU))
        cp_l.append(conv)
        a6, ks, vs = _inproj(hs, w_in_l, l, depth, kv_s, tm=rows_s, vmem_mib=40)
        kv_s = (ks, vs)
        a6 = a6.reshape(N_SEG, bs, t_new, D_ATT)
        u_att = _attn_sample(a6, cache_k, cache_v, sbias, l, bs, t_new, c_len)
        u_lru, h_last, conv = _lru_sample(a6, state_conv, state_h, lru_w, l, bs, t_new)
        nxt = (shift_s[l + 1], scale_s[l + 1], None, g_pre3, l + 1) if more else None
        xs, hs = _outproj(u_att.reshape(rows_s, D_ATT), u_lru.reshape(rows_s, D_LRU), w_out_l, xs, gate_s[l], None,
                          g_post3, l, nxt, tm=rows_s, rows_per_mod=1, vmem_mib=48)
        hs_l.append(h_last)
        cs_l.append(conv)
        w_in_l = w_in_next

    kp, vp = kv_p
    ks, vs = kv_s
    return (xp.reshape(bp, seq, D_MODEL), xs.reshape(bs, t_new, D_MODEL),
            kp.reshape(depth, bp, seq, N_HEADS, HEAD_DIM), vp.reshape(depth, bp, seq, N_HEADS, HEAD_DIM),
            jnp.stack(hp_l), jnp.stack(cp_l),
            ks.reshape(depth, bs, t_new, N_HEADS, HEAD_DIM), vs.reshape(depth, bs, t_new, N_HEADS, HEAD_DIM),
            jnp.stack(hs_l), jnp.stack(cs_l))
```

```python
import functools
import math

import numpy as np
import jax
import jax.numpy as jnp
from jax import lax
from jax.experimental import pallas as pl
from jax.experimental.pallas import tpu as pltpu

D_MODEL = 4096
D_ATT = 2048
D_LRU = 2048
HEAD_DIM = 128
N_HEADS = 16
N_LRU_BLOCKS = 16
LRU_BLOCK = 128
CONV_W = 4
LRU_C = 8.0
DILATED_GROUPS = ((128, 1), (512, 4), (2048, 16))
N_GROUPS = len(DILATED_GROUPS)
WINDOW_MAX = 2048
N_BUCKETS = 32
MAX_EXACT = N_BUCKETS // 2
MAX_DISTANCE = WINDOW_MAX
EPS = 1e-6
ATT_SCALE = HEAD_DIM ** -0.5
D_IN = 4 * D_ATT + 2 * D_LRU
N_SEG = D_IN // D_ATT
SEG_K, SEG_V = 1, 2
L_BAND = 128
MASKED = -1e30
MOD_ROWS = 16
SAMPLE_KEYS = 2048 + 128
MIB = 1024 * 1024
NORM_CHUNK = 64
EPILOGUE_CHUNK = 128
ATTN_GROUP = 8
SCAN_SEGS = 8
KV_HEADS = 8

F32 = jnp.float32
BF16 = jnp.bfloat16


def _params(semantics, vmem_mib):
    return pltpu.CompilerParams(dimension_semantics=semantics, vmem_limit_bytes=vmem_mib * MIB)


def _silu(x):
    half = 0.5 * x
    return half + half * jnp.tanh(half)


def _rel_bucket_np(dist):
    d = dist.astype(np.float32)
    large = np.float32(MAX_EXACT) + np.log(np.maximum(d, np.float32(1.0)) / np.float32(MAX_EXACT)) / np.float32(
        math.log(MAX_DISTANCE / MAX_EXACT)) * np.float32(N_BUCKETS - MAX_EXACT)
    large = np.minimum(large.astype(np.int32), N_BUCKETS - 1)
    return np.where(dist < MAX_EXACT, dist, large).astype(np.int32)


def _prompt_bucket_index():
    qi = np.arange(L_BAND)[:, None]
    kj = np.arange(2 * L_BAND)[None, :]
    dist = qi + L_BAND - kj
    band = (dist >= 0) & (dist <= L_BAND)
    out = []
    for _, dil in DILATED_GROUPS:
        b = _rel_bucket_np(np.clip(dist, 0, L_BAND) * dil)
        out.append(np.where(band, b, -1))
    return np.stack(out).astype(np.int32)


def _sample_bucket_index(c_len, t_new):
    t = np.arange(8)[:, None]
    idx = np.arange(SAMPLE_KEYS)[None, :]
    delta = c_len + t - idx
    out = []
    for window, dil in DILATED_GROUPS:
        valid = (delta >= 0) & (delta % dil == 0) & (delta <= window) & (idx < c_len + t_new) & (t < t_new)
        b = _rel_bucket_np(np.clip(delta, 0, window))
        b = np.where(valid, b, -1)
        b = np.where(t >= t_new, 0, b)
        out.append(b)
    return np.stack(out).astype(np.int32)


def _ada_kernel(c_ref, w_ref, b_ref, o_ref):
    c = c_ref[...]
    a = _silu(c).astype(BF16)
    o_ref[...] = jnp.dot(a, w_ref[...].astype(BF16), preferred_element_type=F32) + b_ref[...]


def _ada(c_all, w_ada, b_ada):
    depth = w_ada.shape[0]
    tn = 512
    return pl.pallas_call(
        _ada_kernel,
        grid=(depth, 3 * D_MODEL // tn),
        in_specs=[
            pl.BlockSpec((MOD_ROWS, D_MODEL), lambda l, j: (0, 0)),
            pl.BlockSpec((None, D_MODEL, tn), lambda l, j: (l, 0, j)),
            pl.BlockSpec((None, 1, tn), lambda l, j: (l, 0, j)),
        ],
        out_specs=pl.BlockSpec((None, MOD_ROWS, tn), lambda l, j: (l, 0, j)),
        out_shape=jax.ShapeDtypeStruct((depth, MOD_ROWS, 3 * D_MODEL), F32),
        compiler_params=_params(("arbitrary", "arbitrary"), 40),
        name="ada",
    )(c_all, w_ada, b_ada.reshape(depth, 1, 3 * D_MODEL))


def _bias_kernel(tab_ref, pidx_ref, sidx_ref, pb_ref, sb_ref):
    h = pl.program_id(0)
    for idx_ref, out_ref in ((pidx_ref, pb_ref), (sidx_ref, sb_ref)):
        for g in range(N_GROUPS):
            idx = idx_ref[g]
            acc = jnp.full(idx.shape, MASKED, F32)
            for b in range(N_BUCKETS):
                acc = jnp.where(idx == b, tab_ref[b, h], acc)
            out_ref[g] = acc


def _bias_tables(rel_table, c_len, t_new):
    pidx = jnp.asarray(_prompt_bucket_index())
    sidx = jnp.asarray(_sample_bucket_index(c_len, t_new))
    return pl.pallas_call(
        _bias_kernel,
        grid=(N_HEADS,),
        in_specs=[
            pl.BlockSpec(memory_space=pltpu.SMEM),
            pl.BlockSpec((N_GROUPS, L_BAND, 2 * L_BAND), lambda h: (0, 0, 0)),
            pl.BlockSpec((N_GROUPS, 8, SAMPLE_KEYS), lambda h: (0, 0, 0)),
        ],
        out_specs=[
            pl.BlockSpec((N_GROUPS, None, L_BAND, 2 * L_BAND), lambda h: (0, h, 0, 0)),
            pl.BlockSpec((N_GROUPS, None, 8, SAMPLE_KEYS), lambda h: (0, h, 0, 0)),
        ],
        out_shape=[
            jax.ShapeDtypeStruct((N_GROUPS, N_HEADS, L_BAND, 2 * L_BAND), F32),
            jax.ShapeDtypeStruct((N_GROUPS, N_HEADS, 8, SAMPLE_KEYS), F32),
        ],
        compiler_params=_params(("arbitrary",), 32),
        name="bias_tables",
    )(rel_table, pidx, sidx)


def _mod_specs(mod_idx, which, tm, rows_per_mod, tile_of=lambda i: i):
    if mod_idx is None:
        return pl.BlockSpec((tm, D_MODEL), lambda i, *_: (tile_of(i), 0))
    return pl.BlockSpec((None, 1, D_MODEL), lambda i, *_: (mod_idx(tile_of(i) // rows_per_mod, which), 0, 0))


def _prenorm_kernel(x_ref, shift_ref, scale_ref, g_ref, h_ref):
    tm = x_ref.shape[0]
    chunk = min(tm, NORM_CHUNK)
    per_row = shift_ref.shape[0] == tm

    def norm(c, _):
        rows = pl.ds(pl.multiple_of(c * chunk, chunk), chunk)
        mrows = rows if per_row else slice(None)
        x = x_ref[rows, :]
        y = x * lax.rsqrt(jnp.mean(x * x, axis=-1, keepdims=True) + EPS)
        y = y * g_ref[...]
        h_ref[rows, :] = (y * (1.0 + scale_ref[mrows, :]) + shift_ref[mrows, :]).astype(h_ref.dtype)
        return 0

    lax.fori_loop(0, tm // chunk, norm, 0)


def _prenorm(x2, shift, scale, mod_idx, g_pre, l, *, tm, rows_per_mod):
    m_rows = x2.shape[0]
    return pl.pallas_call(
        _prenorm_kernel,
        grid=(m_rows // tm,),
        in_specs=[
            pl.BlockSpec((tm, D_MODEL), lambda i: (i, 0)),
            _mod_specs(mod_idx, 0, tm, rows_per_mod),
            _mod_specs(mod_idx, 1, tm, rows_per_mod),
            pl.BlockSpec((None, 1, D_MODEL), lambda i: (l, 0, 0)),
        ],
        out_specs=pl.BlockSpec((tm, D_MODEL), lambda i: (i, 0)),
        out_shape=jax.ShapeDtypeStruct((m_rows, D_MODEL), BF16),
        compiler_params=_params(("arbitrary",), 40),
        name="prenorm",
    )(x2, shift, scale, g_pre)


def _inproj_kernel(h_ref, w_ref, *rest, nj_seg, layer):
    a_ref, k_hbm, v_hbm, stage, sem = rest[-5:]
    i, j = pl.program_id(0), pl.program_id(1)
    tm = h_ref.shape[0]
    seg, part = j // nj_seg, j % nj_seg
    first_kv, last_kv = SEG_K * nj_seg, (SEG_V + 1) * nj_seg - 1

    def head_copy(dst_hbm, group):
        rows = pl.ds(i * tm, tm)
        heads = pl.ds(group * KV_HEADS, KV_HEADS)
        return pltpu.make_async_copy(stage, dst_hbm.at[layer, rows, heads, :], sem.at[0])

    acc = jnp.dot(h_ref[...], w_ref[...], preferred_element_type=F32)
    for hh in range(KV_HEADS):
        a_ref[hh] = acc[:, hh * HEAD_DIM:(hh + 1) * HEAD_DIM]

    @pl.when((j > first_kv) & (j <= last_kv + 1))
    def _():
        head_copy(k_hbm, 0).wait()

    def stage_and_send(dst_hbm):
        flat = stage.reshape(tm * KV_HEADS, HEAD_DIM)
        for hh in range(KV_HEADS):
            flat[pl.ds(hh, tm, stride=KV_HEADS), :] = acc[:, hh * HEAD_DIM:(hh + 1) * HEAD_DIM]
        head_copy(dst_hbm, part).start()

    @pl.when(seg == SEG_K)
    def _():
        stage_and_send(k_hbm)

    @pl.when(seg == SEG_V)
    def _():
        stage_and_send(v_hbm)


def _inproj(h2, w_in, l, depth, kv_prev, *, tm, vmem_mib):
    m_rows = h2.shape[0]
    tn = KV_HEADS * HEAD_DIM
    nj_seg = D_ATT // tn
    in_specs = [
        pl.BlockSpec((tm, D_MODEL), lambda i, j: (i, 0)),
        pl.BlockSpec((D_MODEL, tn), lambda i, j: (0, j)),
    ]
    args = [h2, w_in]
    aliases = {}
    if kv_prev is not None:
        in_specs += [pl.BlockSpec(memory_space=pl.ANY)] * 2
        args += list(kv_prev)
        aliases = {2: 1, 3: 2}
    kv_shape = jax.ShapeDtypeStruct((depth, m_rows, N_HEADS, HEAD_DIM), F32)
    return pl.pallas_call(
        functools.partial(_inproj_kernel, nj_seg=nj_seg, layer=l),
        grid=(m_rows // tm, N_SEG * nj_seg),
        in_specs=in_specs,
        out_specs=[
            pl.BlockSpec((None, KV_HEADS, tm, HEAD_DIM), lambda i, j: (j // nj_seg, j % nj_seg, i, 0)),
            pl.BlockSpec(memory_space=pl.ANY),
            pl.BlockSpec(memory_space=pl.ANY),
        ],
        out_shape=[jax.ShapeDtypeStruct((N_SEG, N_HEADS, m_rows, HEAD_DIM), F32), kv_shape, kv_shape],
        scratch_shapes=[pltpu.VMEM((tm, KV_HEADS, HEAD_DIM), F32), pltpu.SemaphoreType.DMA((1,))],
        input_output_aliases=aliases,
        compiler_params=_params(("arbitrary", "arbitrary"), vmem_mib),
        name="inproj",
    )(*args)


def _attn_prompt_kernel(q_ref, k_ref, v_ref, g_ref, bias_ref, wsrc_ref, *rest, seq, cast_next):
    if cast_next:
        wsrc2_ref, u_ref, wdst_ref, wdst2_ref, qa, ka, va, qb, kb, vb, o_scr, e_scr = rest
        wdst2_ref[...] = wsrc2_ref[...].astype(wdst2_ref.dtype)
    else:
        u_ref, wdst_ref, qa, ka, va, qb, kb, vb, o_scr, e_scr = rest
    wdst_ref[...] = wsrc_ref[...].astype(wdst_ref.dtype)
    n_blk = seq // L_BAND

    def regroup(dst, src, stride):
        sub = seq // stride
        for r in range(stride):
            dst[r * sub:(r + 1) * sub, :] = src[pl.ds(r, sub, stride=stride), :]

    def branch(gi, dil, q_src, k_src, v_src):
        per_res = n_blk // dil
        grp = ATTN_GROUP
        assert n_blk % grp == 0 and (grp % per_res == 0 or per_res % grp == 0)

        def blocks(src, lo, hi):
            return src[lo * L_BAND:hi * L_BAND, :].astype(BF16).reshape(hi - lo, L_BAND, HEAD_DIM)

        for g0 in range(0, n_blk, grp):
            q3 = blocks(q_src, g0, g0 + grp)
            k3 = blocks(k_src, g0, g0 + grp)
            v3 = blocks(v_src, g0, g0 + grp)
            if per_res > 1:
                if g0 == 0:
                    kp = jnp.concatenate([k3[:1], k3[:-1]], axis=0)
                    vp = jnp.concatenate([v3[:1], v3[:-1]], axis=0)
                else:
                    kp = blocks(k_src, g0 - 1, g0 + grp - 1)
                    vp = blocks(v_src, g0 - 1, g0 + grp - 1)
                kk = jnp.concatenate([kp, k3], axis=1)
                vv = jnp.concatenate([vp, v3], axis=1)
                blk = lax.broadcasted_iota(jnp.int32, (grp, 1, 2 * L_BAND), 0) + g0
                col = lax.broadcasted_iota(jnp.int32, (grp, 1, 2 * L_BAND), 2)
                first = jnp.where((blk % per_res == 0) & (col < L_BAND), MASKED, 0.0)
                bias = bias_ref[gi][None] + first
            else:
                kk, vv = k3, v3
                bias = bias_ref[gi, :, L_BAND:][None]
            s = jnp.einsum('bqd,bkd->bqk', q3, kk, preferred_element_type=F32) * ATT_SCALE + bias
            mx = jnp.max(s, axis=-1, keepdims=True)
            p = jnp.exp(s - mx)
            den = jnp.sum(p, axis=-1, keepdims=True)
            o = jnp.einsum('bqk,bkd->bqd', p.astype(BF16), vv, preferred_element_type=F32) / den
            lse = jnp.broadcast_to(mx + jnp.log(den), (grp, L_BAND, HEAD_DIM))
            o = o.reshape(grp * L_BAND, HEAD_DIM)
            lse = lse.reshape(grp * L_BAND, HEAD_DIM)
            if dil == 1:
                o_scr[gi, g0 * L_BAND:(g0 + grp) * L_BAND, :] = o
                e_scr[gi, g0 * L_BAND:(g0 + grp) * L_BAND, :] = lse
            else:
                sub = seq // dil
                for r in range(g0 // per_res, (g0 + grp) // per_res):
                    part = slice((r * per_res - g0) * L_BAND, ((r + 1) * per_res - g0) * L_BAND)
                    o_scr[gi, pl.ds(r, sub, stride=dil), :] = o[part, :]
                    e_scr[gi, pl.ds(r, sub, stride=dil), :] = lse[part, :]

    (_, d0), (_, d1), (_, d2) = DILATED_GROUPS
    assert d0 == 1 and d2 == d1 * d1
    branch(0, d0, q_ref, k_ref, v_ref)
    for dst, src in ((qa, q_ref), (ka, k_ref), (va, v_ref)):
        regroup(dst, src, d1)
    branch(1, d1, qa, ka, va)
    for dst, src in ((qb, qa), (kb, ka), (vb, va)):
        regroup(dst, src, d1)
    branch(2, d2, qb, kb, vb)

    chunk = 256

    def combine(c, _):
        rows = pl.ds(pl.multiple_of(c * chunk, chunk), chunk)
        es = [e_scr[gi, rows, :] for gi in range(N_GROUPS)]
        top = functools.reduce(jnp.maximum, es)
        ws = [jnp.exp(e - top) for e in es]
        num = sum(w * o_scr[gi, rows, :] for gi, w in enumerate(ws))
        den = sum(ws)
        g = g_ref[rows, :]
        u_ref[rows, :] = ((num / den) * _silu(g)).astype(u_ref.dtype)
        return 0

    lax.fori_loop(0, seq // chunk, combine, 0)


def _attn_prompt(a6, pbias, w_out, l, batch, seq, w_in=None):
    blk = (None, None, None, seq, HEAD_DIM)
    steps = batch * N_HEADS
    w_rows = D_MODEL // steps
    half_rows = D_MODEL // (2 * steps)
    step = lambda b, h: b * N_HEADS + h
    in_specs = [pl.BlockSpec(blk, lambda b, h, slot=slot: (slot, h, b, 0, 0)) for slot in range(4)] + [
        pl.BlockSpec((N_GROUPS, None, L_BAND, 2 * L_BAND), lambda b, h: (0, h, 0, 0)),
        pl.BlockSpec((None, w_rows, D_MODEL), lambda b, h: (l, step(b, h), 0)),
    ]
    out_specs = [
        pl.BlockSpec((None, seq, HEAD_DIM), lambda b, h: (b, 0, h)),
        pl.BlockSpec((w_rows, D_MODEL), lambda b, h: (step(b, h), 0)),
    ]
    out_shape = [
        jax.ShapeDtypeStruct((batch, seq, D_ATT), BF16),
        jax.ShapeDtypeStruct((D_MODEL, D_MODEL), BF16),
    ]
    args = [a6, a6, a6, a6, pbias, w_out]
    if w_in is not None:
        in_specs.append(pl.BlockSpec((None, half_rows, D_IN), lambda b, h: (l + 1, step(b, h), 0)))
        out_specs.append(pl.BlockSpec((half_rows, D_IN), lambda b, h: (step(b, h), 0)))
        out_shape.append(jax.ShapeDtypeStruct((D_MODEL, D_IN), BF16))
        args.append(w_in)
    res = pl.pallas_call(
        functools.partial(_attn_prompt_kernel, seq=seq, cast_next=w_in is not None),
        grid=(batch, N_HEADS),
        in_specs=in_specs,
        out_specs=out_specs,
        out_shape=out_shape,
        scratch_shapes=[pltpu.VMEM((seq, HEAD_DIM), F32)] * 6 + [pltpu.VMEM((N_GROUPS, seq, HEAD_DIM), F32)] * 2,
        compiler_params=_params(("arbitrary", "arbitrary"), 48),
        name="attn_prompt",
    )(*args)
    return res if w_in is not None else (*res, None)


def _attn_sample_kernel(q_ref, kn_ref, vn_ref, ck_ref, cv_ref, g_ref, bias_ref, u_ref, q_scr, k_scr, v_scr,
                        *, c_len, t_new):
    tail = SAMPLE_KEYS - c_len
    rows = pl.ds(pl.program_id(0) * t_new, t_new)
    q_scr[t_new:8, :] = jnp.zeros((8 - t_new, HEAD_DIM), F32)
    k_scr[t_new:tail, :] = jnp.zeros((tail - t_new, HEAD_DIM), F32)
    v_scr[t_new:tail, :] = jnp.zeros((tail - t_new, HEAD_DIM), F32)
    for hh in range(KV_HEADS):
        lanes = slice(hh * HEAD_DIM, (hh + 1) * HEAD_DIM)
        q_scr[0:t_new, :] = q_ref[hh, rows, :]
        keys = []
        for new_ref, cache_ref, scr in ((kn_ref, ck_ref, k_scr), (vn_ref, cv_ref, v_scr)):
            heads = cache_ref.reshape(c_len * KV_HEADS, HEAD_DIM)
            scr[0:t_new, :] = new_ref[hh, rows, :]
            keys.append(jnp.concatenate([heads[pl.ds(hh, c_len, stride=KV_HEADS), :].astype(BF16),
                                         scr[...].astype(BF16)], axis=0))
        kk, vv = keys
        s = lax.dot_general(q_scr[...].astype(BF16), kk, (((1,), (1,)), ((), ())),
                            preferred_element_type=F32) * ATT_SCALE
        ms, ls, ps = [], [], []
        for gi in range(N_GROUPS):
            sg = s + bias_ref[gi, hh]
            mx = jnp.max(sg, axis=-1, keepdims=True)
            p = jnp.exp(sg - mx)
            ms.append(mx)
            ls.append(jnp.sum(p, axis=-1, keepdims=True))
            ps.append(p)
        o_all = jnp.dot(jnp.concatenate(ps, axis=0).astype(BF16), vv, preferred_element_type=F32)
        os_ = [o_all[8 * gi:8 * (gi + 1), :] for gi in range(N_GROUPS)]
        top = functools.reduce(jnp.maximum, ms)
        ws = [jnp.exp(m - top) for m in ms]
        num = sum(w * o for w, o in zip(ws, os_))
        den = sum(w * d for w, d in zip(ws, ls))
        g = g_ref[hh, rows, :]
        u_ref[:, lanes] = (num / den)[0:t_new, :] * _silu(g)


def _attn_sample(a6, cache_k, cache_v, sbias, l, batch, t_new, c_len):
    new_blk = (None, KV_HEADS, batch * t_new, HEAD_DIM)
    cache_blk = (None, None, c_len, KV_HEADS, HEAD_DIM)
    return pl.pallas_call(
        functools.partial(_attn_sample_kernel, c_len=c_len, t_new=t_new),
        grid=(batch, N_HEADS // KV_HEADS),
        in_specs=[
            pl.BlockSpec(new_blk, lambda b, hg: (0, hg, 0, 0)),
            pl.BlockSpec(new_blk, lambda b, hg: (1, hg, 0, 0)),
            pl.BlockSpec(new_blk, lambda b, hg: (2, hg, 0, 0)),
            pl.BlockSpec(cache_blk, lambda b, hg: (l, b, 0, hg, 0)),
            pl.BlockSpec(cache_blk, lambda b, hg: (l, b, 0, hg, 0)),
            pl.BlockSpec(new_blk, lambda b, hg: (3, hg, 0, 0)),
            pl.BlockSpec((N_GROUPS, KV_HEADS, 8, SAMPLE_KEYS), lambda b, hg: (0, hg, 0, 0)),
        ],
        out_specs=pl.BlockSpec((None, t_new, KV_HEADS * HEAD_DIM), lambda b, hg: (b, 0, hg)),
        out_shape=jax.ShapeDtypeStruct((batch, t_new, D_ATT), F32),
        scratch_shapes=[pltpu.VMEM((8, HEAD_DIM), F32), pltpu.VMEM((SAMPLE_KEYS - c_len, HEAD_DIM), F32),
                        pltpu.VMEM((SAMPLE_KEYS - c_len, HEAD_DIM), F32)],
        compiler_params=_params(("arbitrary", "arbitrary"), 48),
        name="attn_sample",
    )(a6, a6, a6, cache_k, cache_v, a6, sbias)


def _lru_gates(xc, wa_ref, ba_ref, wx_ref, bx_ref, lam_ref):
    xcb = xc.astype(BF16)
    th_r = jnp.tanh(jnp.dot(xcb, (0.5 * wa_ref[...]).astype(BF16), preferred_element_type=F32) + 0.5 * ba_ref[...])
    th_i = jnp.tanh(jnp.dot(xcb, (0.5 * wx_ref[...]).astype(BF16), preferred_element_type=F32) + 0.5 * bx_ref[...])
    nl = -lam_ref[...]
    softplus = jnp.maximum(nl, 0.0) + jnp.log1p(jnp.exp(-jnp.abs(nl)))
    half_log_a = (-0.25 * LRU_C * softplus) * (1.0 + th_r)
    t = jnp.tanh(half_log_a)
    qn = t / (t - 1.0)
    a = 1.0 - 2.0 * qn
    w = qn * (1.0 - qn)
    root = jnp.where(w > 0.0, w * lax.rsqrt(w), 0.0)
    b = root * ((1.0 + th_i) * xc)
    return a, b


def _lru_prompt_kernel(x_ref, g_ref, wc_ref, bc_ref, wa_ref, ba_ref, wx_ref, bx_ref, lam_ref, *rest, seq, cast):
    if cast:
        wsrc_ref, _, u_ref, h_ref, cs_ref, wdst_ref, xpad, a_scr, b_scr = rest
        wdst_ref[...] = wsrc_ref[...].astype(wdst_ref.dtype)
    else:
        u_ref, h_ref, cs_ref, xpad, a_scr, b_scr = rest
    seg_len = seq // SCAN_SEGS
    lead = (CONV_W - 1) * SCAN_SEGS
    row = lax.broadcasted_iota(jnp.int32, (SCAN_SEGS, LRU_BLOCK), 0)
    for i in range(SCAN_SEGS):
        xpad[pl.ds(lead + i, seg_len, stride=SCAN_SEGS), :] = x_ref[i * seg_len:(i + 1) * seg_len, :]
    for k in range(1, CONV_W):
        tail = xpad[lead + (seg_len - k) * SCAN_SEGS:lead + (seg_len - k + 1) * SCAN_SEGS, :]
        xpad[lead - k * SCAN_SEGS:lead - (k - 1) * SCAN_SEGS, :] = jnp.where(row == 0, 0.0, pltpu.roll(tail, 1, 0))
    xc = bc_ref[...]
    for j in range(CONV_W):
        xc = xc + xpad[j * SCAN_SEGS:j * SCAN_SEGS + seq, :] * wc_ref[j:j + 1, :]
    a, b = _lru_gates(xc, wa_ref, ba_ref, wx_ref, bx_ref, lam_ref)
    a_scr[...] = a
    b_scr[...] = b

    def body(c, carry):
        h_loc, a_cum = carry
        rows = pl.ds(pl.multiple_of(c * SCAN_SEGS, SCAN_SEGS), SCAN_SEGS)
        ca = a_scr[rows, :]
        h_loc = ca * h_loc + b_scr[rows, :]
        a_cum = ca * a_cum
        b_scr[rows, :] = h_loc
        a_scr[rows, :] = a_cum
        return h_loc, a_cum

    init = (jnp.zeros((SCAN_SEGS, LRU_BLOCK), F32), jnp.ones((SCAN_SEGS, LRU_BLOCK), F32))
    h_end, a_end = lax.fori_loop(0, seg_len, body, init, unroll=8)
    carry = jnp.zeros((1, LRU_BLOCK), F32)
    carries = jnp.zeros((SCAN_SEGS, LRU_BLOCK), F32)
    for i in range(1, SCAN_SEGS):
        carry = h_end[i - 1:i, :] + a_end[i - 1:i, :] * carry
        carries = jnp.where(row == i, carry, carries)
    shape3 = (seg_len, SCAN_SEGS, LRU_BLOCK)
    y = b_scr[...].reshape(shape3) + a_scr[...].reshape(shape3) * carries[None]
    b_scr[...] = y.reshape(seq, LRU_BLOCK)
    for i in range(SCAN_SEGS):
        rows = slice(i * seg_len, (i + 1) * seg_len)
        g = g_ref[rows, :]
        u_ref[rows, :] = (b_scr[pl.ds(i, seg_len, stride=SCAN_SEGS), :] * _silu(g)).astype(u_ref.dtype)
    h_ref[...] = b_scr[seq - 1:seq, :]
    cs_ref[...] = x_ref[seq - (CONV_W - 1):seq, :]


def _lru_weight_specs(l, n_of):
    return [
        pl.BlockSpec((None, CONV_W, LRU_BLOCK), lambda *g: (l, 0, n_of(*g))),
        pl.BlockSpec((None, 1, LRU_BLOCK), lambda *g: (l, 0, n_of(*g))),
        pl.BlockSpec((None, None, LRU_BLOCK, LRU_BLOCK), lambda *g: (l, n_of(*g), 0, 0)),
        pl.BlockSpec((None, 1, LRU_BLOCK), lambda *g: (l, 0, n_of(*g))),
        pl.BlockSpec((None, None, LRU_BLOCK, LRU_BLOCK), lambda *g: (l, n_of(*g), 0, 0)),
        pl.BlockSpec((None, 1, LRU_BLOCK), lambda *g: (l, 0, n_of(*g))),
        pl.BlockSpec((None, 1, LRU_BLOCK), lambda *g: (l, 0, n_of(*g))),
    ]


def _lru_prompt(a6, lru_w, l, batch, seq, w_in=None, w_half=None):
    blk = (None, None, None, seq, LRU_BLOCK)
    n_of = lambda b, n: n
    cast = w_in is not None
    steps = batch * N_LRU_BLOCKS
    w_rows = D_MODEL // (2 * steps)
    w_block = lambda b, n: steps + b * N_LRU_BLOCKS + n
    in_specs = [
        pl.BlockSpec(blk, lambda b, n: (4, n, b, 0, 0)),
        pl.BlockSpec(blk, lambda b, n: (5, n, b, 0, 0)),
    ] + _lru_weight_specs(l, n_of)
    out_specs = [
        pl.BlockSpec((None, seq, LRU_BLOCK), lambda b, n: (b, 0, n)),
        pl.BlockSpec((None, 1, LRU_BLOCK), lambda b, n: (b, 0, n)),
        pl.BlockSpec((None, CONV_W - 1, LRU_BLOCK), lambda b, n: (b, 0, n)),
    ]
    out_shape = [
        jax.ShapeDtypeStruct((batch, seq, D_LRU), BF16),
        jax.ShapeDtypeStruct((batch, 1, D_LRU), F32),
        jax.ShapeDtypeStruct((batch, CONV_W - 1, D_LRU), F32),
    ]
    args = [a6, a6, *lru_w]
    if cast:
        in_specs.append(pl.BlockSpec((None, w_rows, D_IN), lambda b, n: (l + 1, w_block(b, n), 0)))
        in_specs.append(pl.BlockSpec(memory_space=pl.ANY))
        out_specs.append(pl.BlockSpec((w_rows, D_IN), lambda b, n: (w_block(b, n), 0)))
        out_shape.append(jax.ShapeDtypeStruct((D_MODEL, D_IN), BF16))
        args += [w_in, w_half]
    res = pl.pallas_call(
        functools.partial(_lru_prompt_kernel, seq=seq, cast=cast),
        grid=(batch, N_LRU_BLOCKS),
        in_specs=in_specs,
        out_specs=out_specs,
        out_shape=out_shape,
        scratch_shapes=[pltpu.VMEM((seq + (CONV_W - 1) * SCAN_SEGS, LRU_BLOCK), F32),
                        pltpu.VMEM((seq, LRU_BLOCK), F32), pltpu.VMEM((seq, LRU_BLOCK), F32)],
        input_output_aliases={len(args) - 1: len(out_shape) - 1} if cast else {},
        compiler_params=_params(("arbitrary", "arbitrary"), 40),
        name="lru_prompt",
    )(*args)
    return res if cast else (*res, None)


def _lru_sample_kernel(x_ref, g_ref, sc_ref, h0_ref, wc_ref, bc_ref, wa_ref, ba_ref, wx_ref, bx_ref, lam_ref,
                       u_ref, h_ref, cs_ref, xc_scr, a_scr, b_scr, y_scr, *, batch, t_new):
    n_state = CONV_W - 1

    def ext_row(b, i):
        if i < n_state:
            return sc_ref[b, i:i + 1, :]
        r = b * t_new + i - n_state
        return x_ref[r:r + 1, :]

    for b in range(batch):
        for t in range(t_new):
            acc = bc_ref[...] + ext_row(b, t) * wc_ref[0:1, :]
            for j in range(1, CONV_W):
                acc = acc + ext_row(b, t + j) * wc_ref[j:j + 1, :]
            xc_scr[b * t_new + t:b * t_new + t + 1, :] = acc
    a, bb = _lru_gates(xc_scr[...], wa_ref, ba_ref, wx_ref, bx_ref, lam_ref)
    a_scr[...] = a
    b_scr[...] = bb
    for b in range(batch):
        h = h0_ref[b:b + 1, :]
        for t in range(t_new):
            r = b * t_new + t
            h = a_scr[r:r + 1, :] * h + b_scr[r:r + 1, :]
            y_scr[r:r + 1, :] = h
        h_ref[b:b + 1, :] = h
        for i in range(n_state):
            cs_ref[b, i:i + 1, :] = ext_row(b, t_new + i)
    for b in range(batch):
        g = g_ref[b * t_new:(b + 1) * t_new, :]
        u_ref[b] = y_scr[b * t_new:(b + 1) * t_new, :] * _silu(g)


def _lru_sample(a6, state_conv, state_h, lru_w, l, batch, t_new):
    blk = (None, None, batch * t_new, LRU_BLOCK)
    n_of = lambda n: n
    rows = batch * t_new
    return pl.pallas_call(
        functools.partial(_lru_sample_kernel, batch=batch, t_new=t_new),
        grid=(N_LRU_BLOCKS,),
        in_specs=[
            pl.BlockSpec(blk, lambda n: (4, n, 0, 0)),
            pl.BlockSpec(blk, lambda n: (5, n, 0, 0)),
            pl.BlockSpec((None, batch, CONV_W - 1, LRU_BLOCK), lambda n: (l, 0, 0, n)),
            pl.BlockSpec((None, batch, LRU_BLOCK), lambda n: (l, 0, n)),
        ] + _lru_weight_specs(l, n_of),
        out_specs=[
            pl.BlockSpec((batch, t_new, LRU_BLOCK), lambda n: (0, 0, n)),
            pl.BlockSpec((batch, LRU_BLOCK), lambda n: (0, n)),
            pl.BlockSpec((batch, CONV_W - 1, LRU_BLOCK), lambda n: (0, 0, n)),
        ],
        out_shape=[
            jax.ShapeDtypeStruct((batch, t_new, D_LRU), F32),
            jax.ShapeDtypeStruct((batch, D_LRU), F32),
            jax.ShapeDtypeStruct((batch, CONV_W - 1, D_LRU), F32),
        ],
        scratch_shapes=[pltpu.VMEM((rows, LRU_BLOCK), F32)] * 4,
        compiler_params=_params(("arbitrary",), 32),
        name="lru_sample",
    )(a6, a6, state_conv, state_h, *lru_w)


def _outproj_kernel(ua_ref, ul_ref, w_ref, x_ref, gate_ref, gp_ref, *rest, n_tiles, has_next):
    if has_next:
        shift_ref, scale_ref, gn_ref, y_ref, h_ref, acc_a, acc_b = rest
    else:
        y_ref, acc_a, acc_b = rest
    i = pl.program_id(0)
    tm = x_ref.shape[0]
    chunk = min(tm, EPILOGUE_CHUNK)
    per_row = gate_ref.shape[0] == tm

    def matmul(acc):
        acc[...] = (jnp.dot(ua_ref[...].astype(BF16), w_ref[0:D_ATT, :], preferred_element_type=F32)
                    + jnp.dot(ul_ref[...].astype(BF16), w_ref[D_ATT:D_MODEL, :], preferred_element_type=F32))

    def epilogue(acc):
        for c in range(tm // chunk):
            rows = slice(c * chunk, (c + 1) * chunk)
            mrows = rows if per_row else slice(None)
            y = acc[rows, :]
            y = y * lax.rsqrt(jnp.mean(y * y, axis=-1, keepdims=True) + EPS)
            out = x_ref[rows, :] + gate_ref[mrows, :] * (y * gp_ref[...])
            y_ref[rows, :] = out
            if has_next:
                hn = out * lax.rsqrt(jnp.mean(out * out, axis=-1, keepdims=True) + EPS)
                hn = hn * gn_ref[...]
                h_ref[rows, :] = (hn * (1.0 + scale_ref[mrows, :]) + shift_ref[mrows, :]).astype(h_ref.dtype)

    accs = (acc_a, acc_b)

    @pl.when(i == 0)
    def _():
        matmul(accs[0])

    for parity in range(2):
        @pl.when((i > 0) & (i < n_tiles) & (i % 2 == parity))
        def _(parity=parity):
            epilogue(accs[1 - parity])
            matmul(accs[parity])

    @pl.when(i == n_tiles)
    def _():
        epilogue(accs[(n_tiles - 1) % 2])


def _outproj(u_att, u_lru, w_out, x2, gate, mod_idx, g_post, l, nxt, *, tm, rows_per_mod, vmem_mib):
    m_rows = x2.shape[0]
    n_tiles = m_rows // tm
    mm_tile = lambda i: jnp.minimum(i, n_tiles - 1)
    ep_tile = lambda i: jnp.maximum(i - 1, 0)
    row_spec = pl.BlockSpec((tm, D_MODEL), lambda i: (ep_tile(i), 0))
    in_specs = [
        pl.BlockSpec((tm, D_ATT), lambda i: (mm_tile(i), 0)),
        pl.BlockSpec((tm, D_LRU), lambda i: (mm_tile(i), 0)),
        pl.BlockSpec((D_MODEL, D_MODEL), lambda i: (0, 0), pipeline_mode=pl.Buffered(1)),
        row_spec,
        _mod_specs(mod_idx, 2, tm, rows_per_mod, ep_tile),
        pl.BlockSpec((None, 1, D_MODEL), lambda i: (l, 0, 0)),
    ]
    args = [u_att, u_lru, w_out, x2, gate, g_post]
    out_specs = [row_spec]
    out_shape = [jax.ShapeDtypeStruct((m_rows, D_MODEL), F32)]
    if nxt is not None:
        shift_n, scale_n, idx_n, g_pre, l_next = nxt
        in_specs += [
            _mod_specs(idx_n, 0, tm, rows_per_mod, ep_tile),
            _mod_specs(idx_n, 1, tm, rows_per_mod, ep_tile),
            pl.BlockSpec((None, 1, D_MODEL), lambda i: (l_next, 0, 0)),
        ]
        args += [shift_n, scale_n, g_pre]
        out_specs.append(row_spec)
        out_shape.append(jax.ShapeDtypeStruct((m_rows, D_MODEL), BF16))
    res = pl.pallas_call(
        functools.partial(_outproj_kernel, n_tiles=n_tiles, has_next=nxt is not None),
        grid=(n_tiles + 1,),
        in_specs=in_specs,
        out_specs=out_specs,
        out_shape=out_shape,
        scratch_shapes=[pltpu.VMEM((tm, D_MODEL), F32)] * 2,
        compiler_params=_params(("arbitrary",), vmem_mib),
        name="outproj",
    )(*args)
    return res if nxt is not None else (res[0], None)


def kernel(x_prompt, x_sample, cache_k, cache_v, state_h, state_conv, c_prompt, c_sample, rel_table, w_ada, b_ada,
           g_pre, w_in, w_conv, b_conv, w_a, b_a, w_x, b_x, lam, w_out, g_post):
    depth = w_in.shape[0]
    bp, seq, _ = x_prompt.shape
    bs, t_new, _ = x_sample.shape
    c_len = cache_k.shape[2]
    assert bp + bs <= MOD_ROWS and c_len + t_new <= SAMPLE_KEYS and t_new <= 8
    assert seq % (L_BAND * DILATED_GROUPS[-1][1]) == 0 and c_len >= WINDOW_MAX

    c_all = jnp.concatenate([c_prompt, c_sample, jnp.zeros((MOD_ROWS - bp - bs, D_MODEL), F32)], axis=0)
    mod = _ada(c_all, w_ada, b_ada)
    mod_tab = mod.reshape(depth * MOD_ROWS * 3, 1, D_MODEL)
    pbias, sbias = _bias_tables(rel_table, c_len, t_new)

    w_in_l = w_in[0].astype(BF16)
    g_pre3 = g_pre.reshape(depth, 1, D_MODEL)
    g_post3 = g_post.reshape(depth, 1, D_MODEL)

    xp = x_prompt.reshape(bp * seq, D_MODEL)
    xs = x_sample.reshape(bs * t_new, D_MODEL)
    rows_s = bs * t_new
    lru_w = (w_conv, b_conv.reshape(depth, 1, D_LRU), w_a, b_a.reshape(depth, 1, D_LRU),
             w_x, b_x.reshape(depth, 1, D_LRU), lam.reshape(depth, 1, D_LRU))
    kv_p = kv_s = None
    hp_l, cp_l, hs_l, cs_l = [], [], [], []
    tm_norm, tm_in, tm_out = 512, 1024, 128
    p_idx = lambda l: (lambda b, which: (l * MOD_ROWS + b) * 3 + which)
    mod_s = [jnp.repeat(mod[l, bp:bp + bs], t_new, axis=0) for l in range(depth)]
    shift_s = [m[:, :D_MODEL] for m in mod_s]
    scale_s = [m[:, D_MODEL:2 * D_MODEL] for m in mod_s]
    gate_s = [m[:, 2 * D_MODEL:] for m in mod_s]
    hp = _prenorm(xp, mod_tab, mod_tab, p_idx(0), g_pre3, 0, tm=tm_norm, rows_per_mod=seq // tm_norm)
    hs = _prenorm(xs, shift_s[0], scale_s[0], None, g_pre3, 0, tm=rows_s, rows_per_mod=1)
    for l in range(depth):
        more = l + 1 < depth
        a6, kp, vp = _inproj(hp, w_in_l, l, depth, kv_p, tm=tm_in, vmem_mib=56)
        kv_p = (kp, vp)
        a6 = a6.reshape(N_SEG, N_HEADS, bp, seq, HEAD_DIM)
        u_att, w_out_l, w_half = _attn_prompt(a6, pbias, w_out, l, bp, seq, w_in if more else None)
        u_lru, h_last, conv, w_in_next = _lru_prompt(a6, lru_w, l, bp, seq, w_in if more else None, w_half)
        nxt = (mod_tab, mod_tab, p_idx(l + 1), g_pre3, l + 1) if more else None
        xp, hp = _outproj(u_att.reshape(bp * seq, D_ATT), u_lru.reshape(bp * seq, D_LRU), w_out_l, xp, mod_tab,
                          p_idx(l), g_post3, l, nxt, tm=tm_out, rows_per_mod=seq // tm_out, vmem_mib=56)
        hp_l.append(h_last.reshape(bp, D_LRU))
        cp_l.append(conv)
        a6, ks, vs = _inproj(hs, w_in_l, l, depth, kv_s, tm=rows_s, vmem_mib=40)
        kv_s = (ks, vs)
        u_att = _attn_sample(a6, cache_k, cache_v, sbias, l, bs, t_new, c_len)
        u_lru, h_last, conv = _lru_sample(a6, state_conv, state_h, lru_w, l, bs, t_new)
        nxt = (shift_s[l + 1], scale_s[l + 1], None, g_pre3, l + 1) if more else None
        xs, hs = _outproj(u_att.reshape(rows_s, D_ATT), u_lru.reshape(rows_s, D_LRU), w_out_l, xs, gate_s[l], None,
                          g_post3, l, nxt, tm=rows_s, rows_per_mod=1, vmem_mib=48)
        hs_l.append(h_last)
        cs_l.append(conv)
        w_in_l = w_in_next

    kp, vp = kv_p
    ks, vs = kv_s
    return (xp.reshape(bp, seq, D_MODEL), xs.reshape(bs, t_new, D_MODEL),
            kp.reshape(depth, bp, seq, N_HEADS, HEAD_DIM), vp.reshape(depth, bp, seq, N_HEADS, HEAD_DIM),
            jnp.stack(hp_l), jnp.stack(cp_l),
            ks.reshape(depth, bs, t_new, N_HEADS, HEAD_DIM), vs.reshape(depth, bs, t_new, N_HEADS, HEAD_DIM),
            jnp.stack(hs_l), jnp.stack(cs_l))
```

```python
import functools
import math

import numpy as np
import jax
import jax.numpy as jnp
from jax import lax
from jax.experimental import pallas as pl
from jax.experimental.pallas import tpu as pltpu

D_MODEL = 4096
D_ATT = 2048
D_LRU = 2048
HEAD_DIM = 128
N_HEADS = 16
N_LRU_BLOCKS = 16
LRU_BLOCK = 128
CONV_W = 4
LRU_C = 8.0
DILATED_GROUPS = ((128, 1), (512, 4), (2048, 16))
N_GROUPS = len(DILATED_GROUPS)
WINDOW_MAX = 2048
N_BUCKETS = 32
MAX_EXACT = N_BUCKETS // 2
MAX_DISTANCE = WINDOW_MAX
EPS = 1e-6
ATT_SCALE = HEAD_DIM ** -0.5
D_IN = 4 * D_ATT + 2 * D_LRU
N_SEG = D_IN // D_ATT
SEG_K, SEG_V = 1, 2
L_BAND = 128
MASKED = -1e30
MOD_ROWS = 16
SAMPLE_TAIL = 512
SAMPLE_DENSE = SAMPLE_TAIL + 128
SAMPLE_T = 4
SAMPLE_KEYS = SAMPLE_DENSE + SAMPLE_T * (WINDOW_MAX // DILATED_GROUPS[-1][1])
MIB = 1024 * 1024
NORM_CHUNK = 64
EPILOGUE_CHUNK = 128
ATTN_GROUP = 8
SCAN_SEGS = 8
KV_HEADS = 8

F32 = jnp.float32
BF16 = jnp.bfloat16


def _params(semantics, vmem_mib):
    return pltpu.CompilerParams(dimension_semantics=semantics, vmem_limit_bytes=vmem_mib * MIB)


def _silu(x):
    half = 0.5 * x
    return half + half * jnp.tanh(half)


def _rel_bucket_np(dist):
    d = dist.astype(np.float32)
    large = np.float32(MAX_EXACT) + np.log(np.maximum(d, np.float32(1.0)) / np.float32(MAX_EXACT)) / np.float32(
        math.log(MAX_DISTANCE / MAX_EXACT)) * np.float32(N_BUCKETS - MAX_EXACT)
    large = np.minimum(large.astype(np.int32), N_BUCKETS - 1)
    return np.where(dist < MAX_EXACT, dist, large).astype(np.int32)


def _prompt_bucket_index():
    qi = np.arange(L_BAND)[:, None]
    kj = np.arange(2 * L_BAND)[None, :]
    dist = qi + L_BAND - kj
    band = (dist >= 0) & (dist <= L_BAND)
    out = []
    for _, dil in DILATED_GROUPS:
        b = _rel_bucket_np(np.clip(dist, 0, L_BAND) * dil)
        out.append(np.where(band, b, -1))
    return np.stack(out).astype(np.int32)


def _sample_bucket_index(c_len, t_new):
    d_last = DILATED_GROUPS[-1][1]
    grp = c_len // d_last
    t = np.arange(8)[:, None]
    col = np.arange(SAMPLE_KEYS)[None, :]
    dense = col < SAMPLE_DENSE
    owner = np.where(dense, -1, (col - SAMPLE_DENSE) // grp)
    idx = np.where(dense, c_len - SAMPLE_TAIL + col, ((col - SAMPLE_DENSE) % grp) * d_last + owner)
    real = np.where(dense, col < SAMPLE_TAIL + t_new, owner < t_new)
    delta = c_len + t - idx
    out = []
    for gi, (window, dil) in enumerate(DILATED_GROUPS):
        valid = real & (delta >= 0) & (delta % dil == 0) & (delta <= window) & (t < t_new)
        if gi == N_GROUPS - 1:
            valid &= np.where(dense, delta < dil, owner == t)
        else:
            valid &= dense
        b = _rel_bucket_np(np.clip(delta, 0, window))
        b = np.where(valid, b, -1)
        b = np.where(t >= t_new, 0, b)
        out.append(b)
    return np.stack(out).astype(np.int32)


def _ada_kernel(c_ref, w_ref, b_ref, o_ref):
    c = c_ref[...]
    a = _silu(c).astype(BF16)
    o_ref[...] = jnp.dot(a, w_ref[...].astype(BF16), preferred_element_type=F32) + b_ref[...]


def _ada(c_all, w_ada, b_ada):
    depth = w_ada.shape[0]
    tn = 512
    return pl.pallas_call(
        _ada_kernel,
        grid=(depth, 3 * D_MODEL // tn),
        in_specs=[
            pl.BlockSpec((MOD_ROWS, D_MODEL), lambda l, j: (0, 0)),
            pl.BlockSpec((None, D_MODEL, tn), lambda l, j: (l, 0, j)),
            pl.BlockSpec((None, 1, tn), lambda l, j: (l, 0, j)),
        ],
        out_specs=pl.BlockSpec((None, MOD_ROWS, tn), lambda l, j: (l, 0, j)),
        out_shape=jax.ShapeDtypeStruct((depth, MOD_ROWS, 3 * D_MODEL), F32),
        compiler_params=_params(("arbitrary", "arbitrary"), 40),
        name="ada",
    )(c_all, w_ada, b_ada.reshape(depth, 1, 3 * D_MODEL))


def _bias_kernel(tab_ref, pidx_ref, sidx_ref, pb_ref, sb_ref):
    h = pl.program_id(0)
    for idx_ref, out_ref in ((pidx_ref, pb_ref), (sidx_ref, sb_ref)):
        for g in range(N_GROUPS):
            idx = idx_ref[g]
            acc = jnp.full(idx.shape, MASKED, F32)
            for b in range(N_BUCKETS):
                acc = jnp.where(idx == b, tab_ref[b, h], acc)
            out_ref[g] = acc


def _bias_tables(rel_table, c_len, t_new):
    pidx = jnp.asarray(_prompt_bucket_index())
    sidx = jnp.asarray(_sample_bucket_index(c_len, t_new))
    return pl.pallas_call(
        _bias_kernel,
        grid=(N_HEADS,),
        in_specs=[
            pl.BlockSpec(memory_space=pltpu.SMEM),
            pl.BlockSpec((N_GROUPS, L_BAND, 2 * L_BAND), lambda h: (0, 0, 0)),
            pl.BlockSpec((N_GROUPS, 8, SAMPLE_KEYS), lambda h: (0, 0, 0)),
        ],
        out_specs=[
            pl.BlockSpec((N_GROUPS, None, L_BAND, 2 * L_BAND), lambda h: (0, h, 0, 0)),
            pl.BlockSpec((N_GROUPS, None, 8, SAMPLE_KEYS), lambda h: (0, h, 0, 0)),
        ],
        out_shape=[
            jax.ShapeDtypeStruct((N_GROUPS, N_HEADS, L_BAND, 2 * L_BAND), F32),
            jax.ShapeDtypeStruct((N_GROUPS, N_HEADS, 8, SAMPLE_KEYS), F32),
        ],
        compiler_params=_params(("arbitrary",), 32),
        name="bias_tables",
    )(rel_table, pidx, sidx)


def _mod_specs(mod_idx, which, tm, rows_per_mod, tile_of=lambda i: i):
    if mod_idx is None:
        return pl.BlockSpec((tm, D_MODEL), lambda i, *_: (tile_of(i), 0))
    return pl.BlockSpec((None, 1, D_MODEL), lambda i, *_: (mod_idx(tile_of(i) // rows_per_mod, which), 0, 0))


def _prenorm_kernel(x_ref, shift_ref, scale_ref, g_ref, h_ref):
    tm = x_ref.shape[0]
    chunk = min(tm, NORM_CHUNK)
    per_row = shift_ref.shape[0] == tm

    def norm(c, _):
        rows = pl.ds(pl.multiple_of(c * chunk, chunk), chunk)
        mrows = rows if per_row else slice(None)
        x = x_ref[rows, :]
        y = x * lax.rsqrt(jnp.mean(x * x, axis=-1, keepdims=True) + EPS)
        y = y * g_ref[...]
        h_ref[rows, :] = (y * (1.0 + scale_ref[mrows, :]) + shift_ref[mrows, :]).astype(h_ref.dtype)
        return 0

    lax.fori_loop(0, tm // chunk, norm, 0)


def _prenorm(x2, shift, scale, mod_idx, g_pre, l, *, tm, rows_per_mod):
    m_rows = x2.shape[0]
    return pl.pallas_call(
        _prenorm_kernel,
        grid=(m_rows // tm,),
        in_specs=[
            pl.BlockSpec((tm, D_MODEL), lambda i: (i, 0)),
            _mod_specs(mod_idx, 0, tm, rows_per_mod),
            _mod_specs(mod_idx, 1, tm, rows_per_mod),
            pl.BlockSpec((None, 1, D_MODEL), lambda i: (l, 0, 0)),
        ],
        out_specs=pl.BlockSpec((tm, D_MODEL), lambda i: (i, 0)),
        out_shape=jax.ShapeDtypeStruct((m_rows, D_MODEL), BF16),
        compiler_params=_params(("arbitrary",), 40),
        name="prenorm",
    )(x2, shift, scale, g_pre)


def _inproj_kernel(h_ref, w_ref, *rest, nj_seg, layer):
    a_ref, k_hbm, v_hbm, stage, sem = rest[-5:]
    i, j = pl.program_id(0), pl.program_id(1)
    tm = h_ref.shape[0]
    seg, part = j // nj_seg, j % nj_seg
    first_kv, last_kv = SEG_K * nj_seg, (SEG_V + 1) * nj_seg - 1

    def head_copy(dst_hbm, group):
        rows = pl.ds(i * tm, tm)
        heads = pl.ds(group * KV_HEADS, KV_HEADS)
        return pltpu.make_async_copy(stage, dst_hbm.at[layer, rows, heads, :], sem.at[0])

    acc = jnp.dot(h_ref[...], w_ref[...], preferred_element_type=F32)
    for hh in range(KV_HEADS):
        a_ref[hh] = acc[:, hh * HEAD_DIM:(hh + 1) * HEAD_DIM]

    @pl.when((j > first_kv) & (j <= last_kv + 1))
    def _():
        head_copy(k_hbm, 0).wait()

    def stage_and_send(dst_hbm):
        flat = stage.reshape(tm * KV_HEADS, HEAD_DIM)
        for hh in range(KV_HEADS):
            flat[pl.ds(hh, tm, stride=KV_HEADS), :] = acc[:, hh * HEAD_DIM:(hh + 1) * HEAD_DIM]
        head_copy(dst_hbm, part).start()

    @pl.when(seg == SEG_K)
    def _():
        stage_and_send(k_hbm)

    @pl.when(seg == SEG_V)
    def _():
        stage_and_send(v_hbm)


def _inproj(h2, w_in, l, depth, kv_prev, *, tm, vmem_mib):
    m_rows = h2.shape[0]
    tn = KV_HEADS * HEAD_DIM
    nj_seg = D_ATT // tn
    in_specs = [
        pl.BlockSpec((tm, D_MODEL), lambda i, j: (i, 0)),
        pl.BlockSpec((D_MODEL, tn), lambda i, j: (0, j)),
    ]
    args = [h2, w_in]
    aliases = {}
    if kv_prev is not None:
        in_specs += [pl.BlockSpec(memory_space=pl.ANY)] * 2
        args += list(kv_prev)
        aliases = {2: 1, 3: 2}
    kv_shape = jax.ShapeDtypeStruct((depth, m_rows, N_HEADS, HEAD_DIM), F32)
    return pl.pallas_call(
        functools.partial(_inproj_kernel, nj_seg=nj_seg, layer=l),
        grid=(m_rows // tm, N_SEG * nj_seg),
        in_specs=in_specs,
        out_specs=[
            pl.BlockSpec((None, KV_HEADS, tm, HEAD_DIM), lambda i, j: (j // nj_seg, j % nj_seg, i, 0)),
            pl.BlockSpec(memory_space=pl.ANY),
            pl.BlockSpec(memory_space=pl.ANY),
        ],
        out_shape=[jax.ShapeDtypeStruct((N_SEG, N_HEADS, m_rows, HEAD_DIM), F32), kv_shape, kv_shape],
        scratch_shapes=[pltpu.VMEM((tm, KV_HEADS, HEAD_DIM), F32), pltpu.SemaphoreType.DMA((1,))],
        input_output_aliases=aliases,
        compiler_params=_params(("arbitrary", "arbitrary"), vmem_mib),
        name="inproj",
    )(*args)


def _attn_prompt_kernel(q_ref, k_ref, v_ref, g_ref, bias_ref, wsrc_ref, *rest, seq, cast_next):
    if cast_next:
        wsrc2_ref, u_ref, wdst_ref, wdst2_ref, qa, ka, va, qb, kb, vb, o_scr, e_scr = rest
        wdst2_ref[...] = wsrc2_ref[...].astype(wdst2_ref.dtype)
    else:
        u_ref, wdst_ref, qa, ka, va, qb, kb, vb, o_scr, e_scr = rest
    wdst_ref[...] = wsrc_ref[...].astype(wdst_ref.dtype)
    n_blk = seq // L_BAND

    def regroup(dst, src, stride):
        sub = seq // stride
        for r in range(stride):
            dst[r * sub:(r + 1) * sub, :] = src[pl.ds(r, sub, stride=stride), :]

    def branch(gi, dil, q_src, k_src, v_src):
        per_res = n_blk // dil
        grp = ATTN_GROUP
        assert n_blk % grp == 0 and (grp % per_res == 0 or per_res % grp == 0)

        def blocks(src, lo, hi):
            return src[lo * L_BAND:hi * L_BAND, :].astype(BF16).reshape(hi - lo, L_BAND, HEAD_DIM)

        for g0 in range(0, n_blk, grp):
            q3 = blocks(q_src, g0, g0 + grp)
            k3 = blocks(k_src, g0, g0 + grp)
            v3 = blocks(v_src, g0, g0 + grp)
            if per_res > 1:
                if g0 == 0:
                    kp = jnp.concatenate([k3[:1], k3[:-1]], axis=0)
                    vp = jnp.concatenate([v3[:1], v3[:-1]], axis=0)
                else:
                    kp = blocks(k_src, g0 - 1, g0 + grp - 1)
                    vp = blocks(v_src, g0 - 1, g0 + grp - 1)
                kk = jnp.concatenate([kp, k3], axis=1)
                vv = jnp.concatenate([vp, v3], axis=1)
                blk = lax.broadcasted_iota(jnp.int32, (grp, 1, 2 * L_BAND), 0) + g0
                col = lax.broadcasted_iota(jnp.int32, (grp, 1, 2 * L_BAND), 2)
                first = jnp.where((blk % per_res == 0) & (col < L_BAND), MASKED, 0.0)
                bias = bias_ref[gi][None] + first
            else:
                kk, vv = k3, v3
                bias = bias_ref[gi, :, L_BAND:][None]
            s = jnp.einsum('bqd,bkd->bqk', q3, kk, preferred_element_type=F32) * ATT_SCALE + bias
            mx = jnp.max(s, axis=-1, keepdims=True)
            p = jnp.exp(s - mx)
            den = jnp.sum(p, axis=-1, keepdims=True)
            o = jnp.einsum('bqk,bkd->bqd', p.astype(BF16), vv, preferred_element_type=F32) / den
            lse = jnp.broadcast_to(mx + jnp.log(den), (grp, L_BAND, HEAD_DIM))
            o = o.reshape(grp * L_BAND, HEAD_DIM)
            lse = lse.reshape(grp * L_BAND, HEAD_DIM)
            if dil == 1:
                o_scr[gi, g0 * L_BAND:(g0 + grp) * L_BAND, :] = o
                e_scr[gi, g0 * L_BAND:(g0 + grp) * L_BAND, :] = lse
            else:
                sub = seq // dil
                for r in range(g0 // per_res, (g0 + grp) // per_res):
                    part = slice((r * per_res - g0) * L_BAND, ((r + 1) * per_res - g0) * L_BAND)
                    o_scr[gi, pl.ds(r, sub, stride=dil), :] = o[part, :]
                    e_scr[gi, pl.ds(r, sub, stride=dil), :] = lse[part, :]

    (_, d0), (_, d1), (_, d2) = DILATED_GROUPS
    assert d0 == 1 and d2 == d1 * d1
    branch(0, d0, q_ref, k_ref, v_ref)
    for dst, src in ((qa, q_ref), (ka, k_ref), (va, v_ref)):
        regroup(dst, src, d1)
    branch(1, d1, qa, ka, va)
    for dst, src in ((qb, qa), (kb, ka), (vb, va)):
        regroup(dst, src, d1)
    branch(2, d2, qb, kb, vb)

    chunk = 256

    def combine(c, _):
        rows = pl.ds(pl.multiple_of(c * chunk, chunk), chunk)
        es = [e_scr[gi, rows, :] for gi in range(N_GROUPS)]
        top = functools.reduce(jnp.maximum, es)
        ws = [jnp.exp(e - top) for e in es]
        num = sum(w * o_scr[gi, rows, :] for gi, w in enumerate(ws))
        den = sum(ws)
        g = g_ref[rows, :]
        u_ref[rows, :] = ((num / den) * _silu(g)).astype(u_ref.dtype)
        return 0

    lax.fori_loop(0, seq // chunk, combine, 0)


def _attn_prompt(a6, pbias, w_out, l, batch, seq, w_in=None):
    blk = (None, None, None, seq, HEAD_DIM)
    steps = batch * N_HEADS
    w_rows = D_MODEL // steps
    half_rows = D_MODEL // (2 * steps)
    step = lambda b, h: b * N_HEADS + h
    in_specs = [pl.BlockSpec(blk, lambda b, h, slot=slot: (slot, h, b, 0, 0)) for slot in range(4)] + [
        pl.BlockSpec((N_GROUPS, None, L_BAND, 2 * L_BAND), lambda b, h: (0, h, 0, 0)),
        pl.BlockSpec((None, w_rows, D_MODEL), lambda b, h: (l, step(b, h), 0)),
    ]
    out_specs = [
        pl.BlockSpec((None, seq, HEAD_DIM), lambda b, h: (b, 0, h)),
        pl.BlockSpec((w_rows, D_MODEL), lambda b, h: (step(b, h), 0)),
    ]
    out_shape = [
        jax.ShapeDtypeStruct((batch, seq, D_ATT), BF16),
        jax.ShapeDtypeStruct((D_MODEL, D_MODEL), BF16),
    ]
    args = [a6, a6, a6, a6, pbias, w_out]
    if w_in is not None:
        in_specs.append(pl.BlockSpec((None, half_rows, D_IN), lambda b, h: (l + 1, step(b, h), 0)))
        out_specs.append(pl.BlockSpec((half_rows, D_IN), lambda b, h: (step(b, h), 0)))
        out_shape.append(jax.ShapeDtypeStruct((D_MODEL, D_IN), BF16))
        args.append(w_in)
    res = pl.pallas_call(
        functools.partial(_attn_prompt_kernel, seq=seq, cast_next=w_in is not None),
        grid=(batch, N_HEADS),
        in_specs=in_specs,
        out_specs=out_specs,
        out_shape=out_shape,
        scratch_shapes=[pltpu.VMEM((seq, HEAD_DIM), F32)] * 6 + [pltpu.VMEM((N_GROUPS, seq, HEAD_DIM), F32)] * 2,
        compiler_params=_params(("arbitrary", "arbitrary"), 48),
        name="attn_prompt",
    )(*args)
    return res if w_in is not None else (*res, None)


def _attn_sample_kernel(q_ref, kn_ref, vn_ref, g_ref, bias_ref, *rest, t_new):
    n_str = 2 * t_new
    tail_refs, strided = rest[0:2], rest[2:2 + n_str]
    u_ref, q_scr, k_scr, v_scr = rest[2 + n_str:]
    pad = SAMPLE_DENSE - SAMPLE_TAIL
    rows = pl.ds(pl.program_id(0) * t_new, t_new)
    q_scr[t_new:8, :] = jnp.zeros((8 - t_new, HEAD_DIM), F32)
    k_scr[t_new:pad, :] = jnp.zeros((pad - t_new, HEAD_DIM), F32)
    v_scr[t_new:pad, :] = jnp.zeros((pad - t_new, HEAD_DIM), F32)

    def head_rows(ref, hh):
        n = ref.shape[0]
        return ref.reshape(n * KV_HEADS, HEAD_DIM)[pl.ds(hh, n, stride=KV_HEADS), :].astype(BF16)

    for hh in range(KV_HEADS):
        lanes = slice(hh * HEAD_DIM, (hh + 1) * HEAD_DIM)
        q_scr[0:t_new, :] = q_ref[hh, rows, :]
        keys = []
        for which, (new_ref, scr) in enumerate(((kn_ref, k_scr), (vn_ref, v_scr))):
            scr[0:t_new, :] = new_ref[hh, rows, :]
            parts = [head_rows(tail_refs[which], hh), scr[...].astype(BF16)]
            parts += [head_rows(r, hh) for r in strided[which * t_new:(which + 1) * t_new]]
            keys.append(jnp.concatenate(parts, axis=0))
        kk, vv = keys
        s = lax.dot_general(q_scr[...].astype(BF16), kk, (((1,), (1,)), ((), ())),
                            preferred_element_type=F32) * ATT_SCALE
        ms, ls, ps = [], [], []
        for gi in range(N_GROUPS):
            sg = s + bias_ref[gi, hh]
            mx = jnp.max(sg, axis=-1, keepdims=True)
            p = jnp.exp(sg - mx)
            ms.append(mx)
            ls.append(jnp.sum(p, axis=-1, keepdims=True))
            ps.append(p)
        o_all = jnp.dot(jnp.concatenate(ps, axis=0).astype(BF16), vv, preferred_element_type=F32)
        os_ = [o_all[8 * gi:8 * (gi + 1), :] for gi in range(N_GROUPS)]
        top = functools.reduce(jnp.maximum, ms)
        ws = [jnp.exp(m - top) for m in ms]
        num = sum(w * o for w, o in zip(ws, os_))
        den = sum(w * d for w, d in zip(ws, ls))
        g = g_ref[hh, rows, :]
        u_ref[:, lanes] = (num / den)[0:t_new, :] * _silu(g)


def _attn_sample(a6, cache_k, cache_v, sbias, l, batch, t_new, c_len):
    d_last = DILATED_GROUPS[-1][1]
    assert c_len == WINDOW_MAX and c_len % SAMPLE_TAIL == 0 and c_len % d_last == 0 and t_new == SAMPLE_T
    assert all(w <= SAMPLE_TAIL for w, _ in DILATED_GROUPS[:-1])
    new_blk = (None, KV_HEADS, batch * t_new, HEAD_DIM)
    tail_spec = pl.BlockSpec((None, None, SAMPLE_TAIL, KV_HEADS, HEAD_DIM),
                             lambda b, hg: (l, b, c_len // SAMPLE_TAIL - 1, hg, 0))
    strided_specs = [pl.BlockSpec((None, None, c_len // d_last, None, KV_HEADS, HEAD_DIM),
                                  lambda b, hg, t=t: (l, b, 0, t, hg, 0)) for t in range(t_new)]
    strided_view = (cache_k.shape[0], batch, c_len // d_last, d_last, N_HEADS, HEAD_DIM)
    pad = SAMPLE_DENSE - SAMPLE_TAIL
    return pl.pallas_call(
        functools.partial(_attn_sample_kernel, t_new=t_new),
        grid=(batch, N_HEADS // KV_HEADS),
        in_specs=[
            pl.BlockSpec(new_blk, lambda b, hg: (0, hg, 0, 0)),
            pl.BlockSpec(new_blk, lambda b, hg: (1, hg, 0, 0)),
            pl.BlockSpec(new_blk, lambda b, hg: (2, hg, 0, 0)),
            pl.BlockSpec(new_blk, lambda b, hg: (3, hg, 0, 0)),
            pl.BlockSpec((N_GROUPS, KV_HEADS, 8, SAMPLE_KEYS), lambda b, hg: (0, hg, 0, 0)),
            tail_spec, tail_spec,
        ] + strided_specs + strided_specs,
        out_specs=pl.BlockSpec((None, t_new, KV_HEADS * HEAD_DIM), lambda b, hg: (b, 0, hg)),
        out_shape=jax.ShapeDtypeStruct((batch, t_new, D_ATT), F32),
        scratch_shapes=[pltpu.VMEM((8, HEAD_DIM), F32), pltpu.VMEM((pad, HEAD_DIM), F32),
                        pltpu.VMEM((pad, HEAD_DIM), F32)],
        compiler_params=_params(("arbitrary", "arbitrary"), 48),
        name="attn_sample",
    )(a6, a6, a6, a6, sbias, cache_k, cache_v,
      *([cache_k.reshape(strided_view)] * t_new), *([cache_v.reshape(strided_view)] * t_new))


def _lru_gates(xc, wa_ref, ba_ref, wx_ref, bx_ref, lam_ref):
    xcb = xc.astype(BF16)
    th_r = jnp.tanh(jnp.dot(xcb, (0.5 * wa_ref[...]).astype(BF16), preferred_element_type=F32) + 0.5 * ba_ref[...])
    th_i = jnp.tanh(jnp.dot(xcb, (0.5 * wx_ref[...]).astype(BF16), preferred_element_type=F32) + 0.5 * bx_ref[...])
    nl = -lam_ref[...]
    softplus = jnp.maximum(nl, 0.0) + jnp.log1p(jnp.exp(-jnp.abs(nl)))
    half_log_a = (-0.25 * LRU_C * softplus) * (1.0 + th_r)
    t = jnp.tanh(half_log_a)
    qn = t / (t - 1.0)
    a = 1.0 - 2.0 * qn
    w = qn * (1.0 - qn)
    root = jnp.where(w > 0.0, w * lax.rsqrt(w), 0.0)
    b = root * ((1.0 + th_i) * xc)
    return a, b


def _lru_prompt_kernel(x_ref, g_ref, wc_ref, bc_ref, wa_ref, ba_ref, wx_ref, bx_ref, lam_ref, *rest, seq, cast):
    if cast:
        wsrc_ref, _, u_ref, h_ref, cs_ref, wdst_ref, xpad, a_scr, b_scr = rest
        wdst_ref[...] = wsrc_ref[...].astype(wdst_ref.dtype)
    else:
        u_ref, h_ref, cs_ref, xpad, a_scr, b_scr = rest
    seg_len = seq // SCAN_SEGS
    lead = (CONV_W - 1) * SCAN_SEGS
    row = lax.broadcasted_iota(jnp.int32, (SCAN_SEGS, LRU_BLOCK), 0)
    for i in range(SCAN_SEGS):
        xpad[pl.ds(lead + i, seg_len, stride=SCAN_SEGS), :] = x_ref[i * seg_len:(i + 1) * seg_len, :]
    for k in range(1, CONV_W):
        tail = xpad[lead + (seg_len - k) * SCAN_SEGS:lead + (seg_len - k + 1) * SCAN_SEGS, :]
        xpad[lead - k * SCAN_SEGS:lead - (k - 1) * SCAN_SEGS, :] = jnp.where(row == 0, 0.0, pltpu.roll(tail, 1, 0))
    xc = bc_ref[...]
    for j in range(CONV_W):
        xc = xc + xpad[j * SCAN_SEGS:j * SCAN_SEGS + seq, :] * wc_ref[j:j + 1, :]
    a, b = _lru_gates(xc, wa_ref, ba_ref, wx_ref, bx_ref, lam_ref)
    a_scr[...] = a
    b_scr[...] = b

    def body(c, carry):
        h_loc, a_cum = carry
        rows = pl.ds(pl.multiple_of(c * SCAN_SEGS, SCAN_SEGS), SCAN_SEGS)
        ca = a_scr[rows, :]
        h_loc = ca * h_loc + b_scr[rows, :]
        a_cum = ca * a_cum
        b_scr[rows, :] = h_loc
        a_scr[rows, :] = a_cum
        return h_loc, a_cum

    init = (jnp.zeros((SCAN_SEGS, LRU_BLOCK), F32), jnp.ones((SCAN_SEGS, LRU_BLOCK), F32))
    h_end, a_end = lax.fori_loop(0, seg_len, body, init, unroll=8)
    carry = jnp.zeros((1, LRU_BLOCK), F32)
    carries = jnp.zeros((SCAN_SEGS, LRU_BLOCK), F32)
    for i in range(1, SCAN_SEGS):
        carry = h_end[i - 1:i, :] + a_end[i - 1:i, :] * carry
        carries = jnp.where(row == i, carry, carries)
    shape3 = (seg_len, SCAN_SEGS, LRU_BLOCK)
    y = b_scr[...].reshape(shape3) + a_scr[...].reshape(shape3) * carries[None]
    b_scr[...] = y.reshape(seq, LRU_BLOCK)
    for i in range(SCAN_SEGS):
        rows = slice(i * seg_len, (i + 1) * seg_len)
        g = g_ref[rows, :]
        u_ref[rows, :] = (b_scr[pl.ds(i, seg_len, stride=SCAN_SEGS), :] * _silu(g)).astype(u_ref.dtype)
    h_ref[...] = b_scr[seq - 1:seq, :]
    cs_ref[...] = x_ref[seq - (CONV_W - 1):seq, :]


def _lru_weight_specs(l, n_of):
    return [
        pl.BlockSpec((None, CONV_W, LRU_BLOCK), lambda *g: (l, 0, n_of(*g))),
        pl.BlockSpec((None, 1, LRU_BLOCK), lambda *g: (l, 0, n_of(*g))),
        pl.BlockSpec((None, None, LRU_BLOCK, LRU_BLOCK), lambda *g: (l, n_of(*g), 0, 0)),
        pl.BlockSpec((None, 1, LRU_BLOCK), lambda *g: (l, 0, n_of(*g))),
        pl.BlockSpec((None, None, LRU_BLOCK, LRU_BLOCK), lambda *g: (l, n_of(*g), 0, 0)),
        pl.BlockSpec((None, 1, LRU_BLOCK), lambda *g: (l, 0, n_of(*g))),
        pl.BlockSpec((None, 1, LRU_BLOCK), lambda *g: (l, 0, n_of(*g))),
    ]


def _lru_prompt(a6, lru_w, l, batch, seq, w_in=None, w_half=None):
    blk = (None, None, None, seq, LRU_BLOCK)
    n_of = lambda b, n: n
    cast = w_in is not None
    steps = batch * N_LRU_BLOCKS
    w_rows = D_MODEL // (2 * steps)
    w_block = lambda b, n: steps + b * N_LRU_BLOCKS + n
    in_specs = [
        pl.BlockSpec(blk, lambda b, n: (4, n, b, 0, 0)),
        pl.BlockSpec(blk, lambda b, n: (5, n, b, 0, 0)),
    ] + _lru_weight_specs(l, n_of)
    out_specs = [
        pl.BlockSpec((None, seq, LRU_BLOCK), lambda b, n: (b, 0, n)),
        pl.BlockSpec((None, 1, LRU_BLOCK), lambda b, n: (b, 0, n)),
        pl.BlockSpec((None, CONV_W - 1, LRU_BLOCK), lambda b, n: (b, 0, n)),
    ]
    out_shape = [
        jax.ShapeDtypeStruct((batch, seq, D_LRU), BF16),
        jax.ShapeDtypeStruct((batch, 1, D_LRU), F32),
        jax.ShapeDtypeStruct((batch, CONV_W - 1, D_LRU), F32),
    ]
    args = [a6, a6, *lru_w]
    if cast:
        in_specs.append(pl.BlockSpec((None, w_rows, D_IN), lambda b, n: (l + 1, w_block(b, n), 0)))
        in_specs.append(pl.BlockSpec(memory_space=pl.ANY))
        out_specs.append(pl.BlockSpec((w_rows, D_IN), lambda b, n: (w_block(b, n), 0)))
        out_shape.append(jax.ShapeDtypeStruct((D_MODEL, D_IN), BF16))
        args += [w_in, w_half]
    res = pl.pallas_call(
        functools.partial(_lru_prompt_kernel, seq=seq, cast=cast),
        grid=(batch, N_LRU_BLOCKS),
        in_specs=in_specs,
        out_specs=out_specs,
        out_shape=out_shape,
        scratch_shapes=[pltpu.VMEM((seq + (CONV_W - 1) * SCAN_SEGS, LRU_BLOCK), F32),
                        pltpu.VMEM((seq, LRU_BLOCK), F32), pltpu.VMEM((seq, LRU_BLOCK), F32)],
        input_output_aliases={len(args) - 1: len(out_shape) - 1} if cast else {},
        compiler_params=_params(("arbitrary", "arbitrary"), 40),
        name="lru_prompt",
    )(*args)
    return res if cast else (*res, None)


def _lru_sample_kernel(x_ref, g_ref, sc_ref, h0_ref, wc_ref, bc_ref, wa_ref, ba_ref, wx_ref, bx_ref, lam_ref,
                       u_ref, h_ref, cs_ref, xc_scr, a_scr, b_scr, y_scr, *, batch, t_new):
    n_state = CONV_W - 1

    def ext_row(b, i):
        if i < n_state:
            return sc_ref[b, i:i + 1, :]
        r = b * t_new + i - n_state
        return x_ref[r:r + 1, :]

    for b in range(batch):
        for t in range(t_new):
            acc = bc_ref[...] + ext_row(b, t) * wc_ref[0:1, :]
            for j in range(1, CONV_W):
                acc = acc + ext_row(b, t + j) * wc_ref[j:j + 1, :]
            xc_scr[b * t_new + t:b * t_new + t + 1, :] = acc
    a, bb = _lru_gates(xc_scr[...], wa_ref, ba_ref, wx_ref, bx_ref, lam_ref)
    a_scr[...] = a
    b_scr[...] = bb
    for b in range(batch):
        h = h0_ref[b:b + 1, :]
        for t in range(t_new):
            r = b * t_new + t
            h = a_scr[r:r + 1, :] * h + b_scr[r:r + 1, :]
            y_scr[r:r + 1, :] = h
        h_ref[b:b + 1, :] = h
        for i in range(n_state):
            cs_ref[b, i:i + 1, :] = ext_row(b, t_new + i)
    for b in range(batch):
        g = g_ref[b * t_new:(b + 1) * t_new, :]
        u_ref[b] = y_scr[b * t_new:(b + 1) * t_new, :] * _silu(g)


def _lru_sample(a6, state_conv, state_h, lru_w, l, batch, t_new):
    blk = (None, None, batch * t_new, LRU_BLOCK)
    n_of = lambda n: n
    rows = batch * t_new
    return pl.pallas_call(
        functools.partial(_lru_sample_kernel, batch=batch, t_new=t_new),
        grid=(N_LRU_BLOCKS,),
        in_specs=[
            pl.BlockSpec(blk, lambda n: (4, n, 0, 0)),
            pl.BlockSpec(blk, lambda n: (5, n, 0, 0)),
            pl.BlockSpec((None, batch, CONV_W - 1, LRU_BLOCK), lambda n: (l, 0, 0, n)),
            pl.BlockSpec((None, batch, LRU_BLOCK), lambda n: (l, 0, n)),
        ] + _lru_weight_specs(l, n_of),
        out_specs=[
            pl.BlockSpec((batch, t_new, LRU_BLOCK), lambda n: (0, 0, n)),
            pl.BlockSpec((batch, LRU_BLOCK), lambda n: (0, n)),
            pl.BlockSpec((batch, CONV_W - 1, LRU_BLOCK), lambda n: (0, 0, n)),
        ],
        out_shape=[
            jax.ShapeDtypeStruct((batch, t_new, D_LRU), F32),
            jax.ShapeDtypeStruct((batch, D_LRU), F32),
            jax.ShapeDtypeStruct((batch, CONV_W - 1, D_LRU), F32),
        ],
        scratch_shapes=[pltpu.VMEM((rows, LRU_BLOCK), F32)] * 4,
        compiler_params=_params(("arbitrary",), 32),
        name="lru_sample",
    )(a6, a6, state_conv, state_h, *lru_w)


def _outproj_kernel(ua_ref, ul_ref, w_ref, x_ref, gate_ref, gp_ref, *rest, n_tiles, has_next):
    if has_next:
        shift_ref, scale_ref, gn_ref, y_ref, h_ref, acc_a, acc_b = rest
    else:
        y_ref, acc_a, acc_b = rest
    i = pl.program_id(0)
    tm = x_ref.shape[0]
    chunk = min(tm, EPILOGUE_CHUNK)
    per_row = gate_ref.shape[0] == tm

    def matmul(acc):
        acc[...] = (jnp.dot(ua_ref[...].astype(BF16), w_ref[0:D_ATT, :], preferred_element_type=F32)
                    + jnp.dot(ul_ref[...].astype(BF16), w_ref[D_ATT:D_MODEL, :], preferred_element_type=F32))

    def epilogue(acc):
        for c in range(tm // chunk):
            rows = slice(c * chunk, (c + 1) * chunk)
            mrows = rows if per_row else slice(None)
            y = acc[rows, :]
            y = y * lax.rsqrt(jnp.mean(y * y, axis=-1, keepdims=True) + EPS)
            out = x_ref[rows, :] + gate_ref[mrows, :] * (y * gp_ref[...])
            y_ref[rows, :] = out
            if has_next:
                hn = out * lax.rsqrt(jnp.mean(out * out, axis=-1, keepdims=True) + EPS)
                hn = hn * gn_ref[...]
                h_ref[rows, :] = (hn * (1.0 + scale_ref[mrows, :]) + shift_ref[mrows, :]).astype(h_ref.dtype)

    accs = (acc_a, acc_b)

    @pl.when(i == 0)
    def _():
        matmul(accs[0])

    for parity in range(2):
        @pl.when((i > 0) & (i < n_tiles) & (i % 2 == parity))
        def _(parity=parity):
            epilogue(accs[1 - parity])
            matmul(accs[parity])

    @pl.when(i == n_tiles)
    def _():
        epilogue(accs[(n_tiles - 1) % 2])


def _outproj(u_att, u_lru, w_out, x2, gate, mod_idx, g_post, l, nxt, *, tm, rows_per_mod, vmem_mib):
    m_rows = x2.shape[0]
    n_tiles = m_rows // tm
    mm_tile = lambda i: jnp.minimum(i, n_tiles - 1)
    ep_tile = lambda i: jnp.maximum(i - 1, 0)
    row_spec = pl.BlockSpec((tm, D_MODEL), lambda i: (ep_tile(i), 0))
    in_specs = [
        pl.BlockSpec((tm, D_ATT), lambda i: (mm_tile(i), 0)),
        pl.BlockSpec((tm, D_LRU), lambda i: (mm_tile(i), 0)),
        pl.BlockSpec((D_MODEL, D_MODEL), lambda i: (0, 0), pipeline_mode=pl.Buffered(1)),
        row_spec,
        _mod_specs(mod_idx, 2, tm, rows_per_mod, ep_tile),
        pl.BlockSpec((None, 1, D_MODEL), lambda i: (l, 0, 0)),
    ]
    args = [u_att, u_lru, w_out, x2, gate, g_post]
    out_specs = [row_spec]
    out_shape = [jax.ShapeDtypeStruct((m_rows, D_MODEL), F32)]
    if nxt is not None:
        shift_n, scale_n, idx_n, g_pre, l_next = nxt
        in_specs += [
            _mod_specs(idx_n, 0, tm, rows_per_mod, ep_tile),
            _mod_specs(idx_n, 1, tm, rows_per_mod, ep_tile),
            pl.BlockSpec((None, 1, D_MODEL), lambda i: (l_next, 0, 0)),
        ]
        args += [shift_n, scale_n, g_pre]
        out_specs.append(row_spec)
        out_shape.append(jax.ShapeDtypeStruct((m_rows, D_MODEL), BF16))
    res = pl.pallas_call(
        functools.partial(_outproj_kernel, n_tiles=n_tiles, has_next=nxt is not None),
        grid=(n_tiles + 1,),
        in_specs=in_specs,
        out_specs=out_specs,
        out_shape=out_shape,
        scratch_shapes=[pltpu.VMEM((tm, D_MODEL), F32)] * 2,
        compiler_params=_params(("arbitrary",), vmem_mib),
        name="outproj",
    )(*args)
    return res if nxt is not None else (res[0], None)


def kernel(x_prompt, x_sample, cache_k, cache_v, state_h, state_conv, c_prompt, c_sample, rel_table, w_ada, b_ada,
           g_pre, w_in, w_conv, b_conv, w_a, b_a, w_x, b_x, lam, w_out, g_post):
    depth = w_in.shape[0]
    bp, seq, _ = x_prompt.shape
    bs, t_new, _ = x_sample.shape
    c_len = cache_k.shape[2]
    assert bp + bs <= MOD_ROWS and t_new <= 8
    assert seq % (L_BAND * DILATED_GROUPS[-1][1]) == 0 and c_len >= WINDOW_MAX

    c_all = jnp.concatenate([c_prompt, c_sample, jnp.zeros((MOD_ROWS - bp - bs, D_MODEL), F32)], axis=0)
    mod = _ada(c_all, w_ada, b_ada)
    mod_tab = mod.reshape(depth * MOD_ROWS * 3, 1, D_MODEL)
    pbias, sbias = _bias_tables(rel_table, c_len, t_new)

    w_in_l = w_in[0].astype(BF16)
    g_pre3 = g_pre.reshape(depth, 1, D_MODEL)
    g_post3 = g_post.reshape(depth, 1, D_MODEL)

    xp = x_prompt.reshape(bp * seq, D_MODEL)
    xs = x_sample.reshape(bs * t_new, D_MODEL)
    rows_s = bs * t_new
    lru_w = (w_conv, b_conv.reshape(depth, 1, D_LRU), w_a, b_a.reshape(depth, 1, D_LRU),
             w_x, b_x.reshape(depth, 1, D_LRU), lam.reshape(depth, 1, D_LRU))
    kv_p = kv_s = None
    hp_l, cp_l, hs_l, cs_l = [], [], [], []
    tm_norm, tm_in, tm_out = 512, 1024, 128
    p_idx = lambda l: (lambda b, which: (l * MOD_ROWS + b) * 3 + which)
    mod_s = [jnp.repeat(mod[l, bp:bp + bs], t_new, axis=0) for l in range(depth)]
    shift_s = [m[:, :D_MODEL] for m in mod_s]
    scale_s = [m[:, D_MODEL:2 * D_MODEL] for m in mod_s]
    gate_s = [m[:, 2 * D_MODEL:] for m in mod_s]
    hp = _prenorm(xp, mod_tab, mod_tab, p_idx(0), g_pre3, 0, tm=tm_norm, rows_per_mod=seq // tm_norm)
    hs = _prenorm(xs, shift_s[0], scale_s[0], None, g_pre3, 0, tm=rows_s, rows_per_mod=1)
    for l in range(depth):
        more = l + 1 < depth
        a6, kp, vp = _inproj(hp, w_in_l, l, depth, kv_p, tm=tm_in, vmem_mib=56)
        kv_p = (kp, vp)
        a6 = a6.reshape(N_SEG, N_HEADS, bp, seq, HEAD_DIM)
        u_att, w_out_l, w_half = _attn_prompt(a6, pbias, w_out, l, bp, seq, w_in if more else None)
        u_lru, h_last, conv, w_in_next = _lru_prompt(a6, lru_w, l, bp, seq, w_in if more else None, w_half)
        nxt = (mod_tab, mod_tab, p_idx(l + 1), g_pre3, l + 1) if more else None
        xp, hp = _outproj(u_att.reshape(bp * seq, D_ATT), u_lru.reshape(bp * seq, D_LRU), w_out_l, xp, mod_tab,
                          p_idx(l), g_post3, l, nxt, tm=tm_out, rows_per_mod=seq // tm_out, vmem_mib=56)
        hp_l.append(h_last.reshape(bp, D_LRU))
        cp_l.append(conv)
        a6, ks, vs = _inproj(hs, w_in_l, l, depth, kv_s, tm=rows_s, vmem_mib=40)
        kv_s = (ks, vs)
        u_att = _attn_sample(a6, cache_k, cache_v, sbias, l, bs, t_new, c_len)
        u_lru, h_last, conv = _lru_sample(a6, state_conv, state_h, lru_w, l, bs, t_new)
        nxt = (shift_s[l + 1], scale_s[l + 1], None, g_pre3, l + 1) if more else None
        xs, hs = _outproj(u_att.reshape(rows_s, D_ATT), u_lru.reshape(rows_s, D_LRU), w_out_l, xs, gate_s[l], None,
                          g_post3, l, nxt, tm=rows_s, rows_per_mod=1, vmem_mib=48)
        hs_l.append(h_last)
        cs_l.append(conv)
        w_in_l = w_in_next

    kp, vp = kv_p
    ks, vs = kv_s
    return (xp.reshape(bp, seq, D_MODEL), xs.reshape(bs, t_new, D_MODEL),
            kp.reshape(depth, bp, seq, N_HEADS, HEAD_DIM), vp.reshape(depth, bp, seq, N_HEADS, HEAD_DIM),
            jnp.stack(hp_l), jnp.stack(cp_l),
            ks.reshape(depth, bs, t_new, N_HEADS, HEAD_DIM), vs.reshape(depth, bs, t_new, N_HEADS, HEAD_DIM),
            jnp.stack(hs_l), jnp.stack(cs_l))
```

```python
import functools
import math

import numpy as np
import jax
import jax.numpy as jnp
from jax import lax
from jax.experimental import pallas as pl
from jax.experimental.pallas import tpu as pltpu

D_MODEL = 4096
D_ATT = 2048
D_LRU = 2048
HEAD_DIM = 128
N_HEADS = 16
N_LRU_BLOCKS = 16
LRU_BLOCK = 128
CONV_W = 4
LRU_C = 8.0
DILATED_GROUPS = ((128, 1), (512, 4), (2048, 16))
N_GROUPS = len(DILATED_GROUPS)
WINDOW_MAX = 2048
N_BUCKETS = 32
MAX_EXACT = N_BUCKETS // 2
MAX_DISTANCE = WINDOW_MAX
EPS = 1e-6
ATT_SCALE = HEAD_DIM ** -0.5
D_IN = 4 * D_ATT + 2 * D_LRU
N_SEG = D_IN // D_ATT
SEG_K, SEG_V = 1, 2
L_BAND = 128
MASKED = -1e30
MOD_ROWS = 16
SAMPLE_TAIL = 512
SAMPLE_DENSE = SAMPLE_TAIL + 128
SAMPLE_T = 4
SAMPLE_KEYS = SAMPLE_DENSE + SAMPLE_T * (WINDOW_MAX // DILATED_GROUPS[-1][1])
MIB = 1024 * 1024
NORM_CHUNK = 64
EPILOGUE_CHUNK = 128
ATTN_GROUP = 8
SCAN_SEGS = 8
KV_HEADS = 8

F32 = jnp.float32
BF16 = jnp.bfloat16


def _params(semantics, vmem_mib):
    return pltpu.CompilerParams(dimension_semantics=semantics, vmem_limit_bytes=vmem_mib * MIB)


def _silu(x):
    half = 0.5 * x
    return half + half * jnp.tanh(half)


def _rel_bucket_np(dist):
    d = dist.astype(np.float32)
    large = np.float32(MAX_EXACT) + np.log(np.maximum(d, np.float32(1.0)) / np.float32(MAX_EXACT)) / np.float32(
        math.log(MAX_DISTANCE / MAX_EXACT)) * np.float32(N_BUCKETS - MAX_EXACT)
    large = np.minimum(large.astype(np.int32), N_BUCKETS - 1)
    return np.where(dist < MAX_EXACT, dist, large).astype(np.int32)


def _prompt_bucket_index():
    qi = np.arange(L_BAND)[:, None]
    kj = np.arange(2 * L_BAND)[None, :]
    dist = qi + L_BAND - kj
    band = (dist >= 0) & (dist <= L_BAND)
    out = []
    for _, dil in DILATED_GROUPS:
        b = _rel_bucket_np(np.clip(dist, 0, L_BAND) * dil)
        out.append(np.where(band, b, -1))
    return np.stack(out).astype(np.int32)


def _sample_bucket_index(c_len, t_new):
    d_last = DILATED_GROUPS[-1][1]
    grp = c_len // d_last
    t = np.arange(8)[:, None]
    col = np.arange(SAMPLE_KEYS)[None, :]
    dense = col < SAMPLE_DENSE
    owner = np.where(dense, -1, (col - SAMPLE_DENSE) // grp)
    idx = np.where(dense, c_len - SAMPLE_TAIL + col, ((col - SAMPLE_DENSE) % grp) * d_last + owner)
    real = np.where(dense, col < SAMPLE_TAIL + t_new, owner < t_new)
    delta = c_len + t - idx
    out = []
    for gi, (window, dil) in enumerate(DILATED_GROUPS):
        valid = real & (delta >= 0) & (delta % dil == 0) & (delta <= window) & (t < t_new)
        if gi == N_GROUPS - 1:
            valid &= np.where(dense, delta < dil, owner == t)
        else:
            valid &= dense
        b = _rel_bucket_np(np.clip(delta, 0, window))
        b = np.where(valid, b, -1)
        b = np.where(t >= t_new, 0, b)
        out.append(b)
    return np.stack(out).astype(np.int32)


def _ada_kernel(c_ref, w_ref, b_ref, o_ref):
    c = c_ref[...]
    a = _silu(c).astype(BF16)
    o_ref[...] = jnp.dot(a, w_ref[...].astype(BF16), preferred_element_type=F32) + b_ref[...]


def _ada(c_all, w_ada, b_ada):
    depth = w_ada.shape[0]
    tn = 1024
    return pl.pallas_call(
        _ada_kernel,
        grid=(depth, 3 * D_MODEL // tn),
        in_specs=[
            pl.BlockSpec((MOD_ROWS, D_MODEL), lambda l, j: (0, 0)),
            pl.BlockSpec((None, D_MODEL, tn), lambda l, j: (l, 0, j)),
            pl.BlockSpec((None, 1, tn), lambda l, j: (l, 0, j)),
        ],
        out_specs=pl.BlockSpec((None, MOD_ROWS, tn), lambda l, j: (l, 0, j)),
        out_shape=jax.ShapeDtypeStruct((depth, MOD_ROWS, 3 * D_MODEL), F32),
        compiler_params=_params(("arbitrary", "arbitrary"), 52),
        name="ada",
    )(c_all, w_ada, b_ada.reshape(depth, 1, 3 * D_MODEL))


def _bias_kernel(tab_ref, pidx_ref, sidx_ref, pb_ref, sb_ref):
    h = pl.program_id(0)
    for idx_ref, out_ref in ((pidx_ref, pb_ref), (sidx_ref, sb_ref)):
        for g in range(N_GROUPS):
            idx = idx_ref[g]
            acc = jnp.full(idx.shape, MASKED, F32)
            for b in range(N_BUCKETS):
                acc = jnp.where(idx == b, tab_ref[b, h], acc)
            out_ref[g] = acc


def _bias_tables(rel_table, c_len, t_new):
    pidx = jnp.asarray(_prompt_bucket_index())
    sidx = jnp.asarray(_sample_bucket_index(c_len, t_new))
    return pl.pallas_call(
        _bias_kernel,
        grid=(N_HEADS,),
        in_specs=[
            pl.BlockSpec(memory_space=pltpu.SMEM),
            pl.BlockSpec((N_GROUPS, L_BAND, 2 * L_BAND), lambda h: (0, 0, 0)),
            pl.BlockSpec((N_GROUPS, 8, SAMPLE_KEYS), lambda h: (0, 0, 0)),
        ],
        out_specs=[
            pl.BlockSpec((N_GROUPS, None, L_BAND, 2 * L_BAND), lambda h: (0, h, 0, 0)),
            pl.BlockSpec((N_GROUPS, None, 8, SAMPLE_KEYS), lambda h: (0, h, 0, 0)),
        ],
        out_shape=[
            jax.ShapeDtypeStruct((N_GROUPS, N_HEADS, L_BAND, 2 * L_BAND), F32),
            jax.ShapeDtypeStruct((N_GROUPS, N_HEADS, 8, SAMPLE_KEYS), F32),
        ],
        compiler_params=_params(("arbitrary",), 32),
        name="bias_tables",
    )(rel_table, pidx, sidx)


def _mod_specs(mod_idx, which, tm, rows_per_mod, tile_of=lambda i: i):
    if mod_idx is None:
        return pl.BlockSpec((tm, D_MODEL), lambda i, *_: (tile_of(i), 0))
    return pl.BlockSpec((None, 1, D_MODEL), lambda i, *_: (mod_idx(tile_of(i) // rows_per_mod, which), 0, 0))


def _prenorm_kernel(x_ref, shift_ref, scale_ref, g_ref, h_ref):
    tm = x_ref.shape[0]
    chunk = min(tm, NORM_CHUNK)
    per_row = shift_ref.shape[0] == tm

    def norm(c, _):
        rows = pl.ds(pl.multiple_of(c * chunk, chunk), chunk)
        mrows = rows if per_row else slice(None)
        x = x_ref[rows, :]
        y = x * lax.rsqrt(jnp.mean(x * x, axis=-1, keepdims=True) + EPS)
        y = y * g_ref[...]
        h_ref[rows, :] = (y * (1.0 + scale_ref[mrows, :]) + shift_ref[mrows, :]).astype(h_ref.dtype)
        return 0

    lax.fori_loop(0, tm // chunk, norm, 0)


def _prenorm(x2, shift, scale, mod_idx, g_pre, l, *, tm, rows_per_mod):
    m_rows = x2.shape[0]
    return pl.pallas_call(
        _prenorm_kernel,
        grid=(m_rows // tm,),
        in_specs=[
            pl.BlockSpec((tm, D_MODEL), lambda i: (i, 0)),
            _mod_specs(mod_idx, 0, tm, rows_per_mod),
            _mod_specs(mod_idx, 1, tm, rows_per_mod),
            pl.BlockSpec((None, 1, D_MODEL), lambda i: (l, 0, 0)),
        ],
        out_specs=pl.BlockSpec((tm, D_MODEL), lambda i: (i, 0)),
        out_shape=jax.ShapeDtypeStruct((m_rows, D_MODEL), BF16),
        compiler_params=_params(("arbitrary",), 40),
        name="prenorm",
    )(x2, shift, scale, g_pre)


def _inproj_kernel(h_ref, hs_ref, w_ref, *rest, nj_seg, layer):
    a_ref, k_hbm, v_hbm, as_ref, ks_ref, vs_ref, stage, sem = rest[-8:]
    i, j = pl.program_id(0), pl.program_id(1)
    tm, rs = h_ref.shape[0], hs_ref.shape[0]
    seg, part = j // nj_seg, j % nj_seg
    first_kv, last_kv = SEG_K * nj_seg, (SEG_V + 1) * nj_seg - 1
    lanes = [slice(hh * HEAD_DIM, (hh + 1) * HEAD_DIM) for hh in range(KV_HEADS)]

    def head_copy(dst_hbm, group):
        rows = pl.ds(i * tm, tm)
        heads = pl.ds(group * KV_HEADS, KV_HEADS)
        return pltpu.make_async_copy(stage, dst_hbm.at[layer, rows, heads, :], sem.at[0])

    acc = jnp.dot(h_ref[...], w_ref[...], preferred_element_type=F32)
    for hh in range(KV_HEADS):
        a_ref[hh] = acc[:, lanes[hh]]

    @pl.when((j > first_kv) & (j <= last_kv + 1))
    def _():
        head_copy(k_hbm, 0).wait()

    def stage_and_send(dst_hbm):
        flat = stage.reshape(tm * KV_HEADS, HEAD_DIM)
        for hh in range(KV_HEADS):
            flat[pl.ds(hh, tm, stride=KV_HEADS), :] = acc[:, lanes[hh]]
        head_copy(dst_hbm, part).start()

    @pl.when(seg == SEG_K)
    def _():
        stage_and_send(k_hbm)

    @pl.when(seg == SEG_V)
    def _():
        stage_and_send(v_hbm)

    @pl.when(i == 0)
    def _():
        acc_s = jnp.dot(hs_ref[...], w_ref[...], preferred_element_type=F32)
        for hh in range(KV_HEADS):
            as_ref[hh] = acc_s[:, lanes[hh]]
        for kv_ref, kv_seg in ((ks_ref, SEG_K), (vs_ref, SEG_V)):
            @pl.when(seg == kv_seg)
            def _(kv_ref=kv_ref):
                flat = kv_ref.reshape(rs * KV_HEADS, HEAD_DIM)
                for hh in range(KV_HEADS):
                    flat[pl.ds(hh, rs, stride=KV_HEADS), :] = acc_s[:, lanes[hh]]


def _inproj(h2, hs2, w_in, l, depth, kv_prev, *, tm, vmem_mib):
    m_rows, rs = h2.shape[0], hs2.shape[0]
    tn = KV_HEADS * HEAD_DIM
    nj_seg = D_ATT // tn
    in_specs = [
        pl.BlockSpec((tm, D_MODEL), lambda i, j: (i, 0)),
        pl.BlockSpec((rs, D_MODEL), lambda i, j: (0, 0)),
        pl.BlockSpec((D_MODEL, tn), lambda i, j: (0, j)),
    ]
    args = [h2, hs2, w_in]
    aliases = {}
    if kv_prev is not None:
        in_specs += [pl.BlockSpec(memory_space=pl.ANY)] * 4
        args += list(kv_prev)
        aliases = {3: 1, 4: 2, 5: 4, 6: 5}

    def as_idx(i, j):
        first = i == 0
        return jnp.where(first, j // nj_seg, N_SEG - 1), jnp.where(first, j % nj_seg, nj_seg - 1), 0, 0

    def kvs_idx(which):
        def idx(i, j):
            seg, part = j // nj_seg, j % nj_seg
            col = jnp.where(seg < which, 0, jnp.where(seg == which, part, nj_seg - 1))
            return l, 0, jnp.where(i == 0, col, nj_seg - 1), 0
        return idx

    kv_shape = jax.ShapeDtypeStruct((depth, m_rows, N_HEADS, HEAD_DIM), F32)
    kvs_shape = jax.ShapeDtypeStruct((depth, rs, N_HEADS, HEAD_DIM), F32)
    return pl.pallas_call(
        functools.partial(_inproj_kernel, nj_seg=nj_seg, layer=l),
        grid=(m_rows // tm, N_SEG * nj_seg),
        in_specs=in_specs,
        out_specs=[
            pl.BlockSpec((None, KV_HEADS, tm, HEAD_DIM), lambda i, j: (j // nj_seg, j % nj_seg, i, 0)),
            pl.BlockSpec(memory_space=pl.ANY),
            pl.BlockSpec(memory_space=pl.ANY),
            pl.BlockSpec((None, KV_HEADS, rs, HEAD_DIM), as_idx),
            pl.BlockSpec((None, rs, KV_HEADS, HEAD_DIM), kvs_idx(SEG_K)),
            pl.BlockSpec((None, rs, KV_HEADS, HEAD_DIM), kvs_idx(SEG_V)),
        ],
        out_shape=[jax.ShapeDtypeStruct((N_SEG, N_HEADS, m_rows, HEAD_DIM), F32), kv_shape, kv_shape,
                   jax.ShapeDtypeStruct((N_SEG, N_HEADS, rs, HEAD_DIM), F32), kvs_shape, kvs_shape],
        scratch_shapes=[pltpu.VMEM((tm, KV_HEADS, HEAD_DIM), F32), pltpu.SemaphoreType.DMA((1,))],
        input_output_aliases=aliases,
        compiler_params=_params(("arbitrary", "arbitrary"), vmem_mib),
        name="inproj",
    )(*args)


def _attn_prompt_kernel(q_ref, k_ref, v_ref, g_ref, bias_ref, wsrc_ref, *rest, seq, cast_next):
    if cast_next:
        wsrc2_ref, u_ref, wdst_ref, wdst2_ref, qa, ka, va, qb, kb, vb, o_scr, e_scr = rest
        wdst2_ref[...] = wsrc2_ref[...].astype(wdst2_ref.dtype)
    else:
        u_ref, wdst_ref, qa, ka, va, qb, kb, vb, o_scr, e_scr = rest
    wdst_ref[...] = wsrc_ref[...].astype(wdst_ref.dtype)
    n_blk = seq // L_BAND

    def regroup(dst, src, stride):
        sub = seq // stride
        for r in range(stride):
            dst[r * sub:(r + 1) * sub, :] = src[pl.ds(r, sub, stride=stride), :]

    def branch(gi, dil, q_src, k_src, v_src):
        per_res = n_blk // dil
        grp = ATTN_GROUP
        assert n_blk % grp == 0 and (grp % per_res == 0 or per_res % grp == 0)

        def blocks(src, lo, hi):
            return src[lo * L_BAND:hi * L_BAND, :].astype(BF16).reshape(hi - lo, L_BAND, HEAD_DIM)

        for g0 in range(0, n_blk, grp):
            q3 = blocks(q_src, g0, g0 + grp)
            k3 = blocks(k_src, g0, g0 + grp)
            v3 = blocks(v_src, g0, g0 + grp)
            if per_res > 1:
                if g0 == 0:
                    kp = jnp.concatenate([k3[:1], k3[:-1]], axis=0)
                    vp = jnp.concatenate([v3[:1], v3[:-1]], axis=0)
                else:
                    kp = blocks(k_src, g0 - 1, g0 + grp - 1)
                    vp = blocks(v_src, g0 - 1, g0 + grp - 1)
                kk = jnp.concatenate([kp, k3], axis=1)
                vv = jnp.concatenate([vp, v3], axis=1)
                blk = lax.broadcasted_iota(jnp.int32, (grp, 1, 2 * L_BAND), 0) + g0
                col = lax.broadcasted_iota(jnp.int32, (grp, 1, 2 * L_BAND), 2)
                first = jnp.where((blk % per_res == 0) & (col < L_BAND), MASKED, 0.0)
                bias = bias_ref[gi][None] + first
            else:
                kk, vv = k3, v3
                bias = bias_ref[gi, :, L_BAND:][None]
            s = jnp.einsum('bqd,bkd->bqk', q3, kk, preferred_element_type=F32) * ATT_SCALE + bias
            mx = jnp.max(s, axis=-1, keepdims=True)
            p = jnp.exp(s - mx)
            den = jnp.sum(p, axis=-1, keepdims=True)
            o = jnp.einsum('bqk,bkd->bqd', p.astype(BF16), vv, preferred_element_type=F32) / den
            lse = jnp.broadcast_to(mx + jnp.log(den), (grp, L_BAND, HEAD_DIM))
            o = o.reshape(grp * L_BAND, HEAD_DIM)
            lse = lse.reshape(grp * L_BAND, HEAD_DIM)
            if dil == 1:
                o_scr[gi, g0 * L_BAND:(g0 + grp) * L_BAND, :] = o
                e_scr[gi, g0 * L_BAND:(g0 + grp) * L_BAND, :] = lse
            else:
                sub = seq // dil
                for r in range(g0 // per_res, (g0 + grp) // per_res):
                    part = slice((r * per_res - g0) * L_BAND, ((r + 1) * per_res - g0) * L_BAND)
                    o_scr[gi, pl.ds(r, sub, stride=dil), :] = o[part, :]
                    e_scr[gi, pl.ds(r, sub, stride=dil), :] = lse[part, :]

    (_, d0), (_, d1), (_, d2) = DILATED_GROUPS
    assert d0 == 1 and d2 == d1 * d1
    branch(0, d0, q_ref, k_ref, v_ref)
    for dst, src in ((qa, q_ref), (ka, k_ref), (va, v_ref)):
        regroup(dst, src, d1)
    branch(1, d1, qa, ka, va)
    for dst, src in ((qb, qa), (kb, ka), (vb, va)):
        regroup(dst, src, d1)
    branch(2, d2, qb, kb, vb)

    chunk = 256

    def combine(c, _):
        rows = pl.ds(pl.multiple_of(c * chunk, chunk), chunk)
        es = [e_scr[gi, rows, :] for gi in range(N_GROUPS)]
        top = functools.reduce(jnp.maximum, es)
        ws = [jnp.exp(e - top) for e in es]
        num = sum(w * o_scr[gi, rows, :] for gi, w in enumerate(ws))
        den = sum(ws)
        g = g_ref[rows, :]
        u_ref[rows, :] = ((num / den) * _silu(g)).astype(u_ref.dtype)
        return 0

    lax.fori_loop(0, seq // chunk, combine, 0)


def _attn_prompt(a6, pbias, w_out, l, batch, seq, w_in=None):
    blk = (None, None, None, seq, HEAD_DIM)
    steps = batch * N_HEADS
    w_rows = D_MODEL // steps
    half_rows = D_MODEL // (2 * steps)
    step = lambda b, h: b * N_HEADS + h
    in_specs = [pl.BlockSpec(blk, lambda b, h, slot=slot: (slot, h, b, 0, 0)) for slot in range(4)] + [
        pl.BlockSpec((N_GROUPS, None, L_BAND, 2 * L_BAND), lambda b, h: (0, h, 0, 0)),
        pl.BlockSpec((None, w_rows, D_MODEL), lambda b, h: (l, step(b, h), 0)),
    ]
    out_specs = [
        pl.BlockSpec((None, seq, HEAD_DIM), lambda b, h: (b, 0, h)),
        pl.BlockSpec((w_rows, D_MODEL), lambda b, h: (step(b, h), 0)),
    ]
    out_shape = [
        jax.ShapeDtypeStruct((batch, seq, D_ATT), BF16),
        jax.ShapeDtypeStruct((D_MODEL, D_MODEL), BF16),
    ]
    args = [a6, a6, a6, a6, pbias, w_out]
    if w_in is not None:
        in_specs.append(pl.BlockSpec((None, half_rows, D_IN), lambda b, h: (l + 1, step(b, h), 0)))
        out_specs.append(pl.BlockSpec((half_rows, D_IN), lambda b, h: (step(b, h), 0)))
        out_shape.append(jax.ShapeDtypeStruct((D_MODEL, D_IN), BF16))
        args.append(w_in)
    res = pl.pallas_call(
        functools.partial(_attn_prompt_kernel, seq=seq, cast_next=w_in is not None),
        grid=(batch, N_HEADS),
        in_specs=in_specs,
        out_specs=out_specs,
        out_shape=out_shape,
        scratch_shapes=[pltpu.VMEM((seq, HEAD_DIM), F32)] * 6 + [pltpu.VMEM((N_GROUPS, seq, HEAD_DIM), F32)] * 2,
        compiler_params=_params(("arbitrary", "arbitrary"), 48),
        name="attn_prompt",
    )(*args)
    return res if w_in is not None else (*res, None)


def _attn_sample_kernel(q_ref, kn_ref, vn_ref, g_ref, bias_ref, *rest, t_new):
    n_str = 2 * t_new
    tail_refs, strided = rest[0:2], rest[2:2 + n_str]
    u_ref, q_scr, k_scr, v_scr = rest[2 + n_str:]
    pad = SAMPLE_DENSE - SAMPLE_TAIL
    rows = pl.ds(pl.program_id(0) * t_new, t_new)
    q_scr[t_new:8, :] = jnp.zeros((8 - t_new, HEAD_DIM), F32)
    k_scr[t_new:pad, :] = jnp.zeros((pad - t_new, HEAD_DIM), F32)
    v_scr[t_new:pad, :] = jnp.zeros((pad - t_new, HEAD_DIM), F32)

    def head_rows(ref, hh):
        n = ref.shape[0]
        return ref.reshape(n * KV_HEADS, HEAD_DIM)[pl.ds(hh, n, stride=KV_HEADS), :].astype(BF16)

    for hh in range(KV_HEADS):
        lanes = slice(hh * HEAD_DIM, (hh + 1) * HEAD_DIM)
        q_scr[0:t_new, :] = q_ref[hh, rows, :]
        keys = []
        for which, (new_ref, scr) in enumerate(((kn_ref, k_scr), (vn_ref, v_scr))):
            scr[0:t_new, :] = new_ref[hh, rows, :]
            parts = [head_rows(tail_refs[which], hh), scr[...].astype(BF16)]
            parts += [head_rows(r, hh) for r in strided[which * t_new:(which + 1) * t_new]]
            keys.append(jnp.concatenate(parts, axis=0))
        kk, vv = keys
        s = lax.dot_general(q_scr[...].astype(BF16), kk, (((1,), (1,)), ((), ())),
                            preferred_element_type=F32) * ATT_SCALE
        ms, ls, ps = [], [], []
        for gi in range(N_GROUPS):
            sg = s + bias_ref[gi, hh]
            mx = jnp.max(sg, axis=-1, keepdims=True)
            p = jnp.exp(sg - mx)
            ms.append(mx)
            ls.append(jnp.sum(p, axis=-1, keepdims=True))
            ps.append(p)
        o_all = jnp.dot(jnp.concatenate(ps, axis=0).astype(BF16), vv, preferred_element_type=F32)
        os_ = [o_all[8 * gi:8 * (gi + 1), :] for gi in range(N_GROUPS)]
        top = functools.reduce(jnp.maximum, ms)
        ws = [jnp.exp(m - top) for m in ms]
        num = sum(w * o for w, o in zip(ws, os_))
        den = sum(w * d for w, d in zip(ws, ls))
        g = g_ref[hh, rows, :]
        u_ref[:, lanes] = (num / den)[0:t_new, :] * _silu(g)


def _attn_sample(a6, cache_k, cache_v, sbias, l, batch, t_new, c_len):
    d_last = DILATED_GROUPS[-1][1]
    assert c_len == WINDOW_MAX and c_len % SAMPLE_TAIL == 0 and c_len % d_last == 0 and t_new == SAMPLE_T
    assert all(w <= SAMPLE_TAIL for w, _ in DILATED_GROUPS[:-1])
    new_blk = (None, KV_HEADS, batch * t_new, HEAD_DIM)
    tail_spec = pl.BlockSpec((None, None, SAMPLE_TAIL, KV_HEADS, HEAD_DIM),
                             lambda b, hg: (l, b, c_len // SAMPLE_TAIL - 1, hg, 0))
    strided_specs = [pl.BlockSpec((None, None, c_len // d_last, None, KV_HEADS, HEAD_DIM),
                                  lambda b, hg, t=t: (l, b, 0, t, hg, 0)) for t in range(t_new)]
    strided_view = (cache_k.shape[0], batch, c_len // d_last, d_last, N_HEADS, HEAD_DIM)
    pad = SAMPLE_DENSE - SAMPLE_TAIL
    return pl.pallas_call(
        functools.partial(_attn_sample_kernel, t_new=t_new),
        grid=(batch, N_HEADS // KV_HEADS),
        in_specs=[
            pl.BlockSpec(new_blk, lambda b, hg: (0, hg, 0, 0)),
            pl.BlockSpec(new_blk, lambda b, hg: (1, hg, 0, 0)),
            pl.BlockSpec(new_blk, lambda b, hg: (2, hg, 0, 0)),
            pl.BlockSpec(new_blk, lambda b, hg: (3, hg, 0, 0)),
            pl.BlockSpec((N_GROUPS, KV_HEADS, 8, SAMPLE_KEYS), lambda b, hg: (0, hg, 0, 0)),
            tail_spec, tail_spec,
        ] + strided_specs + strided_specs,
        out_specs=pl.BlockSpec((None, t_new, KV_HEADS * HEAD_DIM), lambda b, hg: (b, 0, hg)),
        out_shape=jax.ShapeDtypeStruct((batch, t_new, D_ATT), F32),
        scratch_shapes=[pltpu.VMEM((8, HEAD_DIM), F32), pltpu.VMEM((pad, HEAD_DIM), F32),
                        pltpu.VMEM((pad, HEAD_DIM), F32)],
        compiler_params=_params(("arbitrary", "arbitrary"), 48),
        name="attn_sample",
    )(a6, a6, a6, a6, sbias, cache_k, cache_v,
      *([cache_k.reshape(strided_view)] * t_new), *([cache_v.reshape(strided_view)] * t_new))


def _lru_gates(xc, wa_ref, ba_ref, wx_ref, bx_ref, lam_ref):
    xcb = xc.astype(BF16)
    th_r = jnp.tanh(jnp.dot(xcb, (0.5 * wa_ref[...]).astype(BF16), preferred_element_type=F32) + 0.5 * ba_ref[...])
    th_i = jnp.tanh(jnp.dot(xcb, (0.5 * wx_ref[...]).astype(BF16), preferred_element_type=F32) + 0.5 * bx_ref[...])
    nl = -lam_ref[...]
    softplus = jnp.maximum(nl, 0.0) + jnp.log1p(jnp.exp(-jnp.abs(nl)))
    half_log_a = (-0.25 * LRU_C * softplus) * (1.0 + th_r)
    t = jnp.tanh(half_log_a)
    qn = t / (t - 1.0)
    a = 1.0 - 2.0 * qn
    w = qn * (1.0 - qn)
    root = jnp.where(w > 0.0, w * lax.rsqrt(w), 0.0)
    b = root * ((1.0 + th_i) * xc)
    return a, b


def _lru_prompt_kernel(x_ref, g_ref, wc_ref, bc_ref, wa_ref, ba_ref, wx_ref, bx_ref, lam_ref, *rest, seq, cast):
    if cast:
        wsrc_ref, _, u_ref, h_ref, cs_ref, wdst_ref, xpad, a_scr, b_scr = rest
        wdst_ref[...] = wsrc_ref[...].astype(wdst_ref.dtype)
    else:
        u_ref, h_ref, cs_ref, xpad, a_scr, b_scr = rest
    seg_len = seq // SCAN_SEGS
    lead = (CONV_W - 1) * SCAN_SEGS
    row = lax.broadcasted_iota(jnp.int32, (SCAN_SEGS, LRU_BLOCK), 0)
    for i in range(SCAN_SEGS):
        xpad[pl.ds(lead + i, seg_len, stride=SCAN_SEGS), :] = x_ref[i * seg_len:(i + 1) * seg_len, :]
    for k in range(1, CONV_W):
        tail = xpad[lead + (seg_len - k) * SCAN_SEGS:lead + (seg_len - k + 1) * SCAN_SEGS, :]
        xpad[lead - k * SCAN_SEGS:lead - (k - 1) * SCAN_SEGS, :] = jnp.where(row == 0, 0.0, pltpu.roll(tail, 1, 0))
    xc = bc_ref[...]
    for j in range(CONV_W):
        xc = xc + xpad[j * SCAN_SEGS:j * SCAN_SEGS + seq, :] * wc_ref[j:j + 1, :]
    a, b = _lru_gates(xc, wa_ref, ba_ref, wx_ref, bx_ref, lam_ref)
    a_scr[...] = a
    b_scr[...] = b

    def body(c, carry):
        h_loc, a_cum = carry
        rows = pl.ds(pl.multiple_of(c * SCAN_SEGS, SCAN_SEGS), SCAN_SEGS)
        ca = a_scr[rows, :]
        h_loc = ca * h_loc + b_scr[rows, :]
        a_cum = ca * a_cum
        b_scr[rows, :] = h_loc
        a_scr[rows, :] = a_cum
        return h_loc, a_cum

    init = (jnp.zeros((SCAN_SEGS, LRU_BLOCK), F32), jnp.ones((SCAN_SEGS, LRU_BLOCK), F32))
    h_end, a_end = lax.fori_loop(0, seg_len, body, init, unroll=8)
    carry = jnp.zeros((1, LRU_BLOCK), F32)
    carries = jnp.zeros((SCAN_SEGS, LRU_BLOCK), F32)
    for i in range(1, SCAN_SEGS):
        carry = h_end[i - 1:i, :] + a_end[i - 1:i, :] * carry
        carries = jnp.where(row == i, carry, carries)
    shape3 = (seg_len, SCAN_SEGS, LRU_BLOCK)
    y = b_scr[...].reshape(shape3) + a_scr[...].reshape(shape3) * carries[None]
    b_scr[...] = y.reshape(seq, LRU_BLOCK)
    for i in range(SCAN_SEGS):
        rows = slice(i * seg_len, (i + 1) * seg_len)
        g = g_ref[rows, :]
        u_ref[rows, :] = (b_scr[pl.ds(i, seg_len, stride=SCAN_SEGS), :] * _silu(g)).astype(u_ref.dtype)
    h_ref[...] = b_scr[seq - 1:seq, :]
    cs_ref[...] = x_ref[seq - (CONV_W - 1):seq, :]


def _lru_weight_specs(l, n_of):
    return [
        pl.BlockSpec((None, CONV_W, LRU_BLOCK), lambda *g: (l, 0, n_of(*g))),
        pl.BlockSpec((None, 1, LRU_BLOCK), lambda *g: (l, 0, n_of(*g))),
        pl.BlockSpec((None, None, LRU_BLOCK, LRU_BLOCK), lambda *g: (l, n_of(*g), 0, 0)),
        pl.BlockSpec((None, 1, LRU_BLOCK), lambda *g: (l, 0, n_of(*g))),
        pl.BlockSpec((None, None, LRU_BLOCK, LRU_BLOCK), lambda *g: (l, n_of(*g), 0, 0)),
        pl.BlockSpec((None, 1, LRU_BLOCK), lambda *g: (l, 0, n_of(*g))),
        pl.BlockSpec((None, 1, LRU_BLOCK), lambda *g: (l, 0, n_of(*g))),
    ]


def _lru_prompt(a6, lru_w, l, batch, seq, w_in=None, w_half=None):
    blk = (None, None, None, seq, LRU_BLOCK)
    n_of = lambda b, n: n
    cast = w_in is not None
    steps = batch * N_LRU_BLOCKS
    w_rows = D_MODEL // (2 * steps)
    w_block = lambda b, n: steps + b * N_LRU_BLOCKS + n
    in_specs = [
        pl.BlockSpec(blk, lambda b, n: (4, n, b, 0, 0)),
        pl.BlockSpec(blk, lambda b, n: (5, n, b, 0, 0)),
    ] + _lru_weight_specs(l, n_of)
    out_specs = [
        pl.BlockSpec((None, seq, LRU_BLOCK), lambda b, n: (b, 0, n)),
        pl.BlockSpec((None, 1, LRU_BLOCK), lambda b, n: (b, 0, n)),
        pl.BlockSpec((None, CONV_W - 1, LRU_BLOCK), lambda b, n: (b, 0, n)),
    ]
    out_shape = [
        jax.ShapeDtypeStruct((batch, seq, D_LRU), BF16),
        jax.ShapeDtypeStruct((batch, 1, D_LRU), F32),
        jax.ShapeDtypeStruct((batch, CONV_W - 1, D_LRU), F32),
    ]
    args = [a6, a6, *lru_w]
    if cast:
        in_specs.append(pl.BlockSpec((None, w_rows, D_IN), lambda b, n: (l + 1, w_block(b, n), 0)))
        in_specs.append(pl.BlockSpec(memory_space=pl.ANY))
        out_specs.append(pl.BlockSpec((w_rows, D_IN), lambda b, n: (w_block(b, n), 0)))
        out_shape.append(jax.ShapeDtypeStruct((D_MODEL, D_IN), BF16))
        args += [w_in, w_half]
    res = pl.pallas_call(
        functools.partial(_lru_prompt_kernel, seq=seq, cast=cast),
        grid=(batch, N_LRU_BLOCKS),
        in_specs=in_specs,
        out_specs=out_specs,
        out_shape=out_shape,
        scratch_shapes=[pltpu.VMEM((seq + (CONV_W - 1) * SCAN_SEGS, LRU_BLOCK), F32),
                        pltpu.VMEM((seq, LRU_BLOCK), F32), pltpu.VMEM((seq, LRU_BLOCK), F32)],
        input_output_aliases={len(args) - 1: len(out_shape) - 1} if cast else {},
        compiler_params=_params(("arbitrary", "arbitrary"), 40),
        name="lru_prompt",
    )(*args)
    return res if cast else (*res, None)


def _lru_sample_kernel(x_ref, g_ref, sc_ref, h0_ref, wc_ref, bc_ref, wa_ref, ba_ref, wx_ref, bx_ref, lam_ref,
                       u_ref, h_ref, cs_ref, xc_scr, a_scr, b_scr, y_scr, *, batch, t_new):
    n_state = CONV_W - 1

    def ext_row(b, i):
        if i < n_state:
            return sc_ref[b, i:i + 1, :]
        r = b * t_new + i - n_state
        return x_ref[r:r + 1, :]

    for b in range(batch):
        for t in range(t_new):
            acc = bc_ref[...] + ext_row(b, t) * wc_ref[0:1, :]
            for j in range(1, CONV_W):
                acc = acc + ext_row(b, t + j) * wc_ref[j:j + 1, :]
            xc_scr[b * t_new + t:b * t_new + t + 1, :] = acc
    a, bb = _lru_gates(xc_scr[...], wa_ref, ba_ref, wx_ref, bx_ref, lam_ref)
    a_scr[...] = a
    b_scr[...] = bb
    for b in range(batch):
        h = h0_ref[b:b + 1, :]
        for t in range(t_new):
            r = b * t_new + t
            h = a_scr[r:r + 1, :] * h + b_scr[r:r + 1, :]
            y_scr[r:r + 1, :] = h
        h_ref[b:b + 1, :] = h
        for i in range(n_state):
            cs_ref[b, i:i + 1, :] = ext_row(b, t_new + i)
    for b in range(batch):
        g = g_ref[b * t_new:(b + 1) * t_new, :]
        u_ref[b] = y_scr[b * t_new:(b + 1) * t_new, :] * _silu(g)


def _lru_sample(a6, state_conv, state_h, lru_w, l, batch, t_new):
    blk = (None, None, batch * t_new, LRU_BLOCK)
    n_of = lambda n: n
    rows = batch * t_new
    return pl.pallas_call(
        functools.partial(_lru_sample_kernel, batch=batch, t_new=t_new),
        grid=(N_LRU_BLOCKS,),
        in_specs=[
            pl.BlockSpec(blk, lambda n: (4, n, 0, 0)),
            pl.BlockSpec(blk, lambda n: (5, n, 0, 0)),
            pl.BlockSpec((None, batch, CONV_W - 1, LRU_BLOCK), lambda n: (l, 0, 0, n)),
            pl.BlockSpec((None, batch, LRU_BLOCK), lambda n: (l, 0, n)),
        ] + _lru_weight_specs(l, n_of),
        out_specs=[
            pl.BlockSpec((batch, t_new, LRU_BLOCK), lambda n: (0, 0, n)),
            pl.BlockSpec((batch, LRU_BLOCK), lambda n: (0, n)),
            pl.BlockSpec((batch, CONV_W - 1, LRU_BLOCK), lambda n: (0, 0, n)),
        ],
        out_shape=[
            jax.ShapeDtypeStruct((batch, t_new, D_LRU), F32),
            jax.ShapeDtypeStruct((batch, D_LRU), F32),
            jax.ShapeDtypeStruct((batch, CONV_W - 1, D_LRU), F32),
        ],
        scratch_shapes=[pltpu.VMEM((rows, LRU_BLOCK), F32)] * 4,
        compiler_params=_params(("arbitrary",), 32),
        name="lru_sample",
    )(a6, a6, state_conv, state_h, *lru_w)


def _outproj_kernel(ua_ref, ul_ref, w_ref, x_ref, gate_ref, gp_ref, *rest, n_tiles, has_next):
    if has_next:
        shift_ref, scale_ref, gn_ref, y_ref, h_ref, acc_a, acc_b = rest
    else:
        y_ref, acc_a, acc_b = rest
    i = pl.program_id(0)
    tm = x_ref.shape[0]
    chunk = min(tm, EPILOGUE_CHUNK)
    per_row = gate_ref.shape[0] == tm

    def matmul(acc):
        acc[...] = (jnp.dot(ua_ref[...].astype(BF16), w_ref[0:D_ATT, :], preferred_element_type=F32)
                    + jnp.dot(ul_ref[...].astype(BF16), w_ref[D_ATT:D_MODEL, :], preferred_element_type=F32))

    def epilogue(acc):
        for c in range(tm // chunk):
            rows = slice(c * chunk, (c + 1) * chunk)
            mrows = rows if per_row else slice(None)
            y = acc[rows, :]
            y = y * lax.rsqrt(jnp.mean(y * y, axis=-1, keepdims=True) + EPS)
            out = x_ref[rows, :] + gate_ref[mrows, :] * (y * gp_ref[...])
            y_ref[rows, :] = out
            if has_next:
                hn = out * lax.rsqrt(jnp.mean(out * out, axis=-1, keepdims=True) + EPS)
                hn = hn * gn_ref[...]
                h_ref[rows, :] = (hn * (1.0 + scale_ref[mrows, :]) + shift_ref[mrows, :]).astype(h_ref.dtype)

    accs = (acc_a, acc_b)

    @pl.when(i == 0)
    def _():
        matmul(accs[0])

    for parity in range(2):
        @pl.when((i > 0) & (i < n_tiles) & (i % 2 == parity))
        def _(parity=parity):
            epilogue(accs[1 - parity])
            matmul(accs[parity])

    @pl.when(i == n_tiles)
    def _():
        epilogue(accs[(n_tiles - 1) % 2])


def _outproj(u_att, u_lru, w_out, x2, gate, mod_idx, g_post, l, nxt, *, tm, rows_per_mod, vmem_mib):
    m_rows = x2.shape[0]
    n_tiles = m_rows // tm
    mm_tile = lambda i: jnp.minimum(i, n_tiles - 1)
    ep_tile = lambda i: jnp.maximum(i - 1, 0)
    row_spec = pl.BlockSpec((tm, D_MODEL), lambda i: (ep_tile(i), 0))
    in_specs = [
        pl.BlockSpec((tm, D_ATT), lambda i: (mm_tile(i), 0)),
        pl.BlockSpec((tm, D_LRU), lambda i: (mm_tile(i), 0)),
        pl.BlockSpec((D_MODEL, D_MODEL), lambda i: (0, 0), pipeline_mode=pl.Buffered(1)),
        row_spec,
        _mod_specs(mod_idx, 2, tm, rows_per_mod, ep_tile),
        pl.BlockSpec((None, 1, D_MODEL), lambda i: (l, 0, 0)),
    ]
    args = [u_att, u_lru, w_out, x2, gate, g_post]
    out_specs = [row_spec]
    out_shape = [jax.ShapeDtypeStruct((m_rows, D_MODEL), F32)]
    if nxt is not None:
        shift_n, scale_n, idx_n, g_pre, l_next = nxt
        in_specs += [
            _mod_specs(idx_n, 0, tm, rows_per_mod, ep_tile),
            _mod_specs(idx_n, 1, tm, rows_per_mod, ep_tile),
            pl.BlockSpec((None, 1, D_MODEL), lambda i: (l_next, 0, 0)),
        ]
        args += [shift_n, scale_n, g_pre]
        out_specs.append(row_spec)
        out_shape.append(jax.ShapeDtypeStruct((m_rows, D_MODEL), BF16))
    res = pl.pallas_call(
        functools.partial(_outproj_kernel, n_tiles=n_tiles, has_next=nxt is not None),
        grid=(n_tiles + 1,),
        in_specs=in_specs,
        out_specs=out_specs,
        out_shape=out_shape,
        scratch_shapes=[pltpu.VMEM((tm, D_MODEL), F32)] * 2,
        compiler_params=_params(("arbitrary",), vmem_mib),
        name="outproj",
    )(*args)
    return res if nxt is not None else (res[0], None)


def kernel(x_prompt, x_sample, cache_k, cache_v, state_h, state_conv, c_prompt, c_sample, rel_table, w_ada, b_ada,
           g_pre, w_in, w_conv, b_conv, w_a, b_a, w_x, b_x, lam, w_out, g_post):
    depth = w_in.shape[0]
    bp, seq, _ = x_prompt.shape
    bs, t_new, _ = x_sample.shape
    c_len = cache_k.shape[2]
    assert bp + bs <= MOD_ROWS and t_new <= 8
    assert seq % (L_BAND * DILATED_GROUPS[-1][1]) == 0 and c_len >= WINDOW_MAX

    c_all = jnp.concatenate([c_prompt, c_sample, jnp.zeros((MOD_ROWS - bp - bs, D_MODEL), F32)], axis=0)
    mod = _ada(c_all, w_ada, b_ada)
    mod_tab = mod.reshape(depth * MOD_ROWS * 3, 1, D_MODEL)
    pbias, sbias = _bias_tables(rel_table, c_len, t_new)

    w_in_l = w_in[0].astype(BF16)
    g_pre3 = g_pre.reshape(depth, 1, D_MODEL)
    g_post3 = g_post.reshape(depth, 1, D_MODEL)

    xp = x_prompt.reshape(bp * seq, D_MODEL)
    xs = x_sample.reshape(bs * t_new, D_MODEL)
    rows_s = bs * t_new
    lru_w = (w_conv, b_conv.reshape(depth, 1, D_LRU), w_a, b_a.reshape(depth, 1, D_LRU),
             w_x, b_x.reshape(depth, 1, D_LRU), lam.reshape(depth, 1, D_LRU))
    kv_all = None
    hp_l, cp_l, hs_l, cs_l = [], [], [], []
    tm_norm, tm_in, tm_out = 512, 1024, 128
    p_idx = lambda l: (lambda b, which: (l * MOD_ROWS + b) * 3 + which)
    mod_s = [jnp.repeat(mod[l, bp:bp + bs], t_new, axis=0) for l in range(depth)]
    shift_s = [m[:, :D_MODEL] for m in mod_s]
    scale_s = [m[:, D_MODEL:2 * D_MODEL] for m in mod_s]
    gate_s = [m[:, 2 * D_MODEL:] for m in mod_s]
    hp = _prenorm(xp, mod_tab, mod_tab, p_idx(0), g_pre3, 0, tm=tm_norm, rows_per_mod=seq // tm_norm)
    hs = _prenorm(xs, shift_s[0], scale_s[0], None, g_pre3, 0, tm=rows_s, rows_per_mod=1)
    for l in range(depth):
        more = l + 1 < depth
        a6, kp, vp, a6_s, ks, vs = _inproj(hp, hs, w_in_l, l, depth, kv_all, tm=tm_in, vmem_mib=58)
        kv_all = (kp, vp, ks, vs)
        a6 = a6.reshape(N_SEG, N_HEADS, bp, seq, HEAD_DIM)
        u_att, w_out_l, w_half = _attn_prompt(a6, pbias, w_out, l, bp, seq, w_in if more else None)
        u_lru, h_last, conv, w_in_next = _lru_prompt(a6, lru_w, l, bp, seq, w_in if more else None, w_half)
        nxt = (mod_tab, mod_tab, p_idx(l + 1), g_pre3, l + 1) if more else None
        xp, hp = _outproj(u_att.reshape(bp * seq, D_ATT), u_lru.reshape(bp * seq, D_LRU), w_out_l, xp, mod_tab,
                          p_idx(l), g_post3, l, nxt, tm=tm_out, rows_per_mod=seq // tm_out, vmem_mib=56)
        hp_l.append(h_last.reshape(bp, D_LRU))
        cp_l.append(conv)
        u_att = _attn_sample(a6_s, cache_k, cache_v, sbias, l, bs, t_new, c_len)
        u_lru, h_last, conv = _lru_sample(a6_s, state_conv, state_h, lru_w, l, bs, t_new)
        nxt = (shift_s[l + 1], scale_s[l + 1], None, g_pre3, l + 1) if more else None
        xs, hs = _outproj(u_att.reshape(rows_s, D_ATT), u_lru.reshape(rows_s, D_LRU), w_out_l, xs, gate_s[l], None,
                          g_post3, l, nxt, tm=rows_s, rows_per_mod=1, vmem_mib=48)
        hs_l.append(h_last)
        cs_l.append(conv)
        w_in_l = w_in_next

    kp, vp, ks, vs = kv_all
    return (xp.reshape(bp, seq, D_MODEL), xs.reshape(bs, t_new, D_MODEL),
            kp.reshape(depth, bp, seq, N_HEADS, HEAD_DIM), vp.reshape(depth, bp, seq, N_HEADS, HEAD_DIM),
            jnp.stack(hp_l), jnp.stack(cp_l),
            ks.reshape(depth, bs, t_new, N_HEADS, HEAD_DIM), vs.reshape(depth, bs, t_new, N_HEADS, HEAD_DIM),
            jnp.stack(hs_l), jnp.stack(cs_l))
```

```python
import functools
import math

import numpy as np
import jax
import jax.numpy as jnp
from jax import lax
from jax.experimental import pallas as pl
from jax.experimental.pallas import tpu as pltpu

D_MODEL = 4096
D_ATT = 2048
D_LRU = 2048
HEAD_DIM = 128
N_HEADS = 16
N_LRU_BLOCKS = 16
LRU_BLOCK = 128
CONV_W = 4
LRU_C = 8.0
DILATED_GROUPS = ((128, 1), (512, 4), (2048, 16))
N_GROUPS = len(DILATED_GROUPS)
WINDOW_MAX = 2048
N_BUCKETS = 32
MAX_EXACT = N_BUCKETS // 2
MAX_DISTANCE = WINDOW_MAX
EPS = 1e-6
ATT_SCALE = HEAD_DIM ** -0.5
D_IN = 4 * D_ATT + 2 * D_LRU
N_SEG = D_IN // D_ATT
SEG_K, SEG_V = 1, 2
L_BAND = 128
MASKED = -1e30
MOD_ROWS = 16
SAMPLE_TAIL = 512
SAMPLE_DENSE = SAMPLE_TAIL + 128
SAMPLE_T = 4
SAMPLE_KEYS = SAMPLE_DENSE + SAMPLE_T * (WINDOW_MAX // DILATED_GROUPS[-1][1])
MIB = 1024 * 1024
PROMPT_ROW_TILES = (512, 1024, 128)
BIG_CALL_VMEM_MIB = 56
SMALL_CALL_VMEM_MIB = 48
NORM_CHUNK = 64
EPILOGUE_CHUNK = 128
ATTN_GROUP = 8
SCAN_SEGS = 8
KV_HEADS = 8

F32 = jnp.float32
BF16 = jnp.bfloat16


def _params(semantics, vmem_mib):
    return pltpu.CompilerParams(dimension_semantics=semantics, vmem_limit_bytes=vmem_mib * MIB)


def _silu(x):
    half = 0.5 * x
    return half + half * jnp.tanh(half)


def _rel_bucket_np(dist):
    d = dist.astype(np.float32)
    large = np.float32(MAX_EXACT) + np.log(np.maximum(d, np.float32(1.0)) / np.float32(MAX_EXACT)) / np.float32(
        math.log(MAX_DISTANCE / MAX_EXACT)) * np.float32(N_BUCKETS - MAX_EXACT)
    large = np.minimum(large.astype(np.int32), N_BUCKETS - 1)
    return np.where(dist < MAX_EXACT, dist, large).astype(np.int32)


def _prompt_bucket_index():
    qi = np.arange(L_BAND)[:, None]
    kj = np.arange(2 * L_BAND)[None, :]
    dist = qi + L_BAND - kj
    band = (dist >= 0) & (dist <= L_BAND)
    out = []
    for _, dil in DILATED_GROUPS:
        b = _rel_bucket_np(np.clip(dist, 0, L_BAND) * dil)
        out.append(np.where(band, b, -1))
    return np.stack(out).astype(np.int32)


def _sample_bucket_index(c_len, t_new):
    d_last = DILATED_GROUPS[-1][1]
    grp = c_len // d_last
    t = np.arange(8)[:, None]
    col = np.arange(SAMPLE_KEYS)[None, :]
    dense = col < SAMPLE_DENSE
    owner = np.where(dense, -1, (col - SAMPLE_DENSE) // grp)
    idx = np.where(dense, c_len - SAMPLE_TAIL + col, ((col - SAMPLE_DENSE) % grp) * d_last + owner)
    real = np.where(dense, col < SAMPLE_TAIL + t_new, owner < t_new)
    delta = c_len + t - idx
    out = []
    for gi, (window, dil) in enumerate(DILATED_GROUPS):
        valid = real & (delta >= 0) & (delta % dil == 0) & (delta <= window) & (t < t_new)
        if gi == N_GROUPS - 1:
            valid &= np.where(dense, delta < dil, owner == t)
        else:
            valid &= dense
        b = _rel_bucket_np(np.clip(delta, 0, window))
        b = np.where(valid, b, -1)
        b = np.where(t >= t_new, 0, b)
        out.append(b)
    return np.stack(out).astype(np.int32)


def _ada_kernel(c_ref, w_ref, b_ref, o_ref):
    c = c_ref[...]
    a = _silu(c).astype(BF16)
    o_ref[...] = jnp.dot(a, w_ref[...].astype(BF16), preferred_element_type=F32) + b_ref[...]


def _ada(c_all, w_ada, b_ada):
    depth = w_ada.shape[0]
    tn = 512
    return pl.pallas_call(
        _ada_kernel,
        grid=(depth, 3 * D_MODEL // tn),
        in_specs=[
            pl.BlockSpec((MOD_ROWS, D_MODEL), lambda l, j: (0, 0)),
            pl.BlockSpec((None, D_MODEL, tn), lambda l, j: (l, 0, j)),
            pl.BlockSpec((None, 1, tn), lambda l, j: (l, 0, j)),
        ],
        out_specs=pl.BlockSpec((None, MOD_ROWS, tn), lambda l, j: (l, 0, j)),
        out_shape=jax.ShapeDtypeStruct((depth, MOD_ROWS, 3 * D_MODEL), F32),
        compiler_params=_params(("arbitrary", "arbitrary"), 40),
        name="ada",
    )(c_all, w_ada, b_ada.reshape(depth, 1, 3 * D_MODEL))


def _bias_kernel(tab_ref, pidx_ref, sidx_ref, pb_ref, sb_ref):
    h = pl.program_id(0)
    for idx_ref, out_ref in ((pidx_ref, pb_ref), (sidx_ref, sb_ref)):
        for g in range(N_GROUPS):
            idx = idx_ref[g]
            acc = jnp.full(idx.shape, MASKED, F32)
            for b in range(N_BUCKETS):
                acc = jnp.where(idx == b, tab_ref[b, h], acc)
            out_ref[g] = acc


def _bias_tables(rel_table, c_len, t_new):
    pidx = jnp.asarray(_prompt_bucket_index())
    sidx = jnp.asarray(_sample_bucket_index(c_len, t_new))
    return pl.pallas_call(
        _bias_kernel,
        grid=(N_HEADS,),
        in_specs=[
            pl.BlockSpec(memory_space=pltpu.SMEM),
            pl.BlockSpec((N_GROUPS, L_BAND, 2 * L_BAND), lambda h: (0, 0, 0)),
            pl.BlockSpec((N_GROUPS, 8, SAMPLE_KEYS), lambda h: (0, 0, 0)),
        ],
        out_specs=[
            pl.BlockSpec((N_GROUPS, None, L_BAND, 2 * L_BAND), lambda h: (0, h, 0, 0)),
            pl.BlockSpec((N_GROUPS, None, 8, SAMPLE_KEYS), lambda h: (0, h, 0, 0)),
        ],
        out_shape=[
            jax.ShapeDtypeStruct((N_GROUPS, N_HEADS, L_BAND, 2 * L_BAND), F32),
            jax.ShapeDtypeStruct((N_GROUPS, N_HEADS, 8, SAMPLE_KEYS), F32),
        ],
        compiler_params=_params(("arbitrary",), 32),
        name="bias_tables",
    )(rel_table, pidx, sidx)


def _mod_specs(mod_idx, which, tm, rows_per_mod, tile_of=lambda i: i):
    if mod_idx is None:
        return pl.BlockSpec((tm, D_MODEL), lambda i, *_: (tile_of(i), 0))
    return pl.BlockSpec((None, 1, D_MODEL), lambda i, *_: (mod_idx(tile_of(i) // rows_per_mod, which), 0, 0))


def _prenorm_kernel(x_ref, shift_ref, scale_ref, g_ref, h_ref):
    tm = x_ref.shape[0]
    chunk = min(tm, NORM_CHUNK)
    per_row = shift_ref.shape[0] == tm

    def norm(c, _):
        rows = pl.ds(pl.multiple_of(c * chunk, chunk), chunk)
        mrows = rows if per_row else slice(None)
        x = x_ref[rows, :]
        y = x * lax.rsqrt(jnp.mean(x * x, axis=-1, keepdims=True) + EPS)
        y = y * g_ref[...]
        h_ref[rows, :] = (y * (1.0 + scale_ref[mrows, :]) + shift_ref[mrows, :]).astype(h_ref.dtype)
        return 0

    lax.fori_loop(0, tm // chunk, norm, 0)


def _prenorm(x2, shift, scale, mod_idx, g_pre, l, *, tm, rows_per_mod):
    m_rows = x2.shape[0]
    return pl.pallas_call(
        _prenorm_kernel,
        grid=(m_rows // tm,),
        in_specs=[
            pl.BlockSpec((tm, D_MODEL), lambda i: (i, 0)),
            _mod_specs(mod_idx, 0, tm, rows_per_mod),
            _mod_specs(mod_idx, 1, tm, rows_per_mod),
            pl.BlockSpec((None, 1, D_MODEL), lambda i: (l, 0, 0)),
        ],
        out_specs=pl.BlockSpec((tm, D_MODEL), lambda i: (i, 0)),
        out_shape=jax.ShapeDtypeStruct((m_rows, D_MODEL), BF16),
        compiler_params=_params(("arbitrary",), 40),
        name="prenorm",
    )(x2, shift, scale, g_pre)


def _inproj_kernel(h_ref, w_ref, *rest, nj_seg, layer):
    a_ref, k_hbm, v_hbm, stage, sem = rest[-5:]
    i, j = pl.program_id(0), pl.program_id(1)
    tm = h_ref.shape[0]
    seg, part = j // nj_seg, j % nj_seg
    first_kv, last_kv = SEG_K * nj_seg, (SEG_V + 1) * nj_seg - 1

    def head_copy(dst_hbm, group):
        rows = pl.ds(i * tm, tm)
        heads = pl.ds(group * KV_HEADS, KV_HEADS)
        return pltpu.make_async_copy(stage, dst_hbm.at[layer, rows, heads, :], sem.at[0])

    acc = jnp.dot(h_ref[...], w_ref[...], preferred_element_type=F32)
    for hh in range(KV_HEADS):
        a_ref[hh] = acc[:, hh * HEAD_DIM:(hh + 1) * HEAD_DIM]

    @pl.when((j > first_kv) & (j <= last_kv + 1))
    def _():
        head_copy(k_hbm, 0).wait()

    def stage_and_send(dst_hbm):
        flat = stage.reshape(tm * KV_HEADS, HEAD_DIM)
        for hh in range(KV_HEADS):
            flat[pl.ds(hh, tm, stride=KV_HEADS), :] = acc[:, hh * HEAD_DIM:(hh + 1) * HEAD_DIM]
        head_copy(dst_hbm, part).start()

    @pl.when(seg == SEG_K)
    def _():
        stage_and_send(k_hbm)

    @pl.when(seg == SEG_V)
    def _():
        stage_and_send(v_hbm)


def _inproj(h2, w_in, l, depth, kv_prev, *, tm, vmem_mib):
    m_rows = h2.shape[0]
    tn = KV_HEADS * HEAD_DIM
    nj_seg = D_ATT // tn
    in_specs = [
        pl.BlockSpec((tm, D_MODEL), lambda i, j: (i, 0)),
        pl.BlockSpec((D_MODEL, tn), lambda i, j: (0, j)),
    ]
    args = [h2, w_in]
    aliases = {}
    if kv_prev is not None:
        in_specs += [pl.BlockSpec(memory_space=pl.ANY)] * 2
        args += list(kv_prev)
        aliases = {2: 1, 3: 2}
    kv_shape = jax.ShapeDtypeStruct((depth, m_rows, N_HEADS, HEAD_DIM), F32)
    return pl.pallas_call(
        functools.partial(_inproj_kernel, nj_seg=nj_seg, layer=l),
        grid=(m_rows // tm, N_SEG * nj_seg),
        in_specs=in_specs,
        out_specs=[
            pl.BlockSpec((None, KV_HEADS, tm, HEAD_DIM), lambda i, j: (j // nj_seg, j % nj_seg, i, 0)),
            pl.BlockSpec(memory_space=pl.ANY),
            pl.BlockSpec(memory_space=pl.ANY),
        ],
        out_shape=[jax.ShapeDtypeStruct((N_SEG, N_HEADS, m_rows, HEAD_DIM), F32), kv_shape, kv_shape],
        scratch_shapes=[pltpu.VMEM((tm, KV_HEADS, HEAD_DIM), F32), pltpu.SemaphoreType.DMA((1,))],
        input_output_aliases=aliases,
        compiler_params=_params(("arbitrary", "arbitrary"), vmem_mib),
        name="inproj",
    )(*args)


def _attn_prompt_kernel(q_ref, k_ref, v_ref, g_ref, bias_ref, wsrc_ref, *rest, seq, cast_next):
    if cast_next:
        wsrc2_ref, u_ref, wdst_ref, wdst2_ref, qa, ka, va, qb, kb, vb, o_scr, e_scr = rest
        wdst2_ref[...] = wsrc2_ref[...].astype(wdst2_ref.dtype)
    else:
        u_ref, wdst_ref, qa, ka, va, qb, kb, vb, o_scr, e_scr = rest
    wdst_ref[...] = wsrc_ref[...].astype(wdst_ref.dtype)
    n_blk = seq // L_BAND

    def regroup(dst, src, stride):
        sub = seq // stride
        for r in range(stride):
            dst[r * sub:(r + 1) * sub, :] = src[pl.ds(r, sub, stride=stride), :]

    def branch(gi, dil, q_src, k_src, v_src):
        per_res = n_blk // dil
        grp = ATTN_GROUP
        assert n_blk % grp == 0 and (grp % per_res == 0 or per_res % grp == 0)

        def blocks(src, lo, hi):
            return src[lo * L_BAND:hi * L_BAND, :].astype(BF16).reshape(hi - lo, L_BAND, HEAD_DIM)

        for g0 in range(0, n_blk, grp):
            q3 = blocks(q_src, g0, g0 + grp)
            k3 = blocks(k_src, g0, g0 + grp)
            v3 = blocks(v_src, g0, g0 + grp)
            if per_res > 1:
                if g0 == 0:
                    kp = jnp.concatenate([k3[:1], k3[:-1]], axis=0)
                    vp = jnp.concatenate([v3[:1], v3[:-1]], axis=0)
                else:
                    kp = blocks(k_src, g0 - 1, g0 + grp - 1)
                    vp = blocks(v_src, g0 - 1, g0 + grp - 1)
                kk = jnp.concatenate([kp, k3], axis=1)
                vv = jnp.concatenate([vp, v3], axis=1)
                blk = lax.broadcasted_iota(jnp.int32, (grp, 1, 2 * L_BAND), 0) + g0
                col = lax.broadcasted_iota(jnp.int32, (grp, 1, 2 * L_BAND), 2)
                first = jnp.where((blk % per_res == 0) & (col < L_BAND), MASKED, 0.0)
                bias = bias_ref[gi][None] + first
            else:
                kk, vv = k3, v3
                bias = bias_ref[gi, :, L_BAND:][None]
            s = jnp.einsum('bqd,bkd->bqk', q3, kk, preferred_element_type=F32) * ATT_SCALE + bias
            mx = jnp.max(s, axis=-1, keepdims=True)
            p = jnp.exp(s - mx)
            den = jnp.sum(p, axis=-1, keepdims=True)
            o = jnp.einsum('bqk,bkd->bqd', p.astype(BF16), vv, preferred_element_type=F32) / den
            lse = jnp.broadcast_to(mx + jnp.log(den), (grp, L_BAND, HEAD_DIM))
            o = o.reshape(grp * L_BAND, HEAD_DIM)
            lse = lse.reshape(grp * L_BAND, HEAD_DIM)
            if dil == 1:
                o_scr[gi, g0 * L_BAND:(g0 + grp) * L_BAND, :] = o
                e_scr[gi, g0 * L_BAND:(g0 + grp) * L_BAND, :] = lse
            else:
                sub = seq // dil
                for r in range(g0 // per_res, (g0 + grp) // per_res):
                    part = slice((r * per_res - g0) * L_BAND, ((r + 1) * per_res - g0) * L_BAND)
                    o_scr[gi, pl.ds(r, sub, stride=dil), :] = o[part, :]
                    e_scr[gi, pl.ds(r, sub, stride=dil), :] = lse[part, :]

    (_, d0), (_, d1), (_, d2) = DILATED_GROUPS
    assert d0 == 1 and d2 == d1 * d1
    branch(0, d0, q_ref, k_ref, v_ref)
    for dst, src in ((qa, q_ref), (ka, k_ref), (va, v_ref)):
        regroup(dst, src, d1)
    branch(1, d1, qa, ka, va)
    for dst, src in ((qb, qa), (kb, ka), (vb, va)):
        regroup(dst, src, d1)
    branch(2, d2, qb, kb, vb)

    chunk = 256

    def combine(c, _):
        rows = pl.ds(pl.multiple_of(c * chunk, chunk), chunk)
        es = [e_scr[gi, rows, :] for gi in range(N_GROUPS)]
        top = functools.reduce(jnp.maximum, es)
        ws = [jnp.exp(e - top) for e in es]
        num = sum(w * o_scr[gi, rows, :] for gi, w in enumerate(ws))
        den = sum(ws)
        g = g_ref[rows, :]
        u_ref[rows, :] = ((num / den) * _silu(g)).astype(u_ref.dtype)
        return 0

    lax.fori_loop(0, seq // chunk, combine, 0)


def _attn_prompt(a6, pbias, w_out, l, batch, seq, w_in=None):
    blk = (None, None, None, seq, HEAD_DIM)
    steps = batch * N_HEADS
    w_rows = D_MODEL // steps
    half_rows = D_MODEL // (2 * steps)
    step = lambda b, h: b * N_HEADS + h
    in_specs = [pl.BlockSpec(blk, lambda b, h, slot=slot: (slot, h, b, 0, 0)) for slot in range(4)] + [
        pl.BlockSpec((N_GROUPS, None, L_BAND, 2 * L_BAND), lambda b, h: (0, h, 0, 0)),
        pl.BlockSpec((None, w_rows, D_MODEL), lambda b, h: (l, step(b, h), 0)),
    ]
    out_specs = [
        pl.BlockSpec((None, seq, HEAD_DIM), lambda b, h: (b, 0, h)),
        pl.BlockSpec((w_rows, D_MODEL), lambda b, h: (step(b, h), 0)),
    ]
    out_shape = [
        jax.ShapeDtypeStruct((batch, seq, D_ATT), BF16),
        jax.ShapeDtypeStruct((D_MODEL, D_MODEL), BF16),
    ]
    args = [a6, a6, a6, a6, pbias, w_out]
    if w_in is not None:
        in_specs.append(pl.BlockSpec((None, half_rows, D_IN), lambda b, h: (l + 1, step(b, h), 0)))
        out_specs.append(pl.BlockSpec((half_rows, D_IN), lambda b, h: (step(b, h), 0)))
        out_shape.append(jax.ShapeDtypeStruct((D_MODEL, D_IN), BF16))
        args.append(w_in)
    res = pl.pallas_call(
        functools.partial(_attn_prompt_kernel, seq=seq, cast_next=w_in is not None),
        grid=(batch, N_HEADS),
        in_specs=in_specs,
        out_specs=out_specs,
        out_shape=out_shape,
        scratch_shapes=[pltpu.VMEM((seq, HEAD_DIM), F32)] * 6 + [pltpu.VMEM((N_GROUPS, seq, HEAD_DIM), F32)] * 2,
        compiler_params=_params(("arbitrary", "arbitrary"), 48),
        name="attn_prompt",
    )(*args)
    return res if w_in is not None else (*res, None)


def _attn_sample_kernel(q_ref, kn_ref, vn_ref, g_ref, bias_ref, *rest, t_new):
    n_str = 2 * t_new
    tail_refs, strided = rest[0:2], rest[2:2 + n_str]
    u_ref, q_scr, k_scr, v_scr = rest[2 + n_str:]
    pad = SAMPLE_DENSE - SAMPLE_TAIL
    rows = pl.ds(pl.program_id(0) * t_new, t_new)
    q_scr[t_new:8, :] = jnp.zeros((8 - t_new, HEAD_DIM), F32)
    k_scr[t_new:pad, :] = jnp.zeros((pad - t_new, HEAD_DIM), F32)
    v_scr[t_new:pad, :] = jnp.zeros((pad - t_new, HEAD_DIM), F32)

    def head_rows(ref, hh):
        n = ref.shape[0]
        return ref.reshape(n * KV_HEADS, HEAD_DIM)[pl.ds(hh, n, stride=KV_HEADS), :].astype(BF16)

    for hh in range(KV_HEADS):
        lanes = slice(hh * HEAD_DIM, (hh + 1) * HEAD_DIM)
        q_scr[0:t_new, :] = q_ref[hh, rows, :]
        keys = []
        for which, (new_ref, scr) in enumerate(((kn_ref, k_scr), (vn_ref, v_scr))):
            scr[0:t_new, :] = new_ref[hh, rows, :]
            parts = [head_rows(tail_refs[which], hh), scr[...].astype(BF16)]
            parts += [head_rows(r, hh) for r in strided[which * t_new:(which + 1) * t_new]]
            keys.append(jnp.concatenate(parts, axis=0))
        kk, vv = keys
        s = lax.dot_general(q_scr[...].astype(BF16), kk, (((1,), (1,)), ((), ())),
                            preferred_element_type=F32) * ATT_SCALE
        ms, ls, ps = [], [], []
        for gi in range(N_GROUPS):
            sg = s + bias_ref[gi, hh]
            mx = jnp.max(sg, axis=-1, keepdims=True)
            p = jnp.exp(sg - mx)
            ms.append(mx)
            ls.append(jnp.sum(p, axis=-1, keepdims=True))
            ps.append(p)
        o_all = jnp.dot(jnp.concatenate(ps, axis=0).astype(BF16), vv, preferred_element_type=F32)
        os_ = [o_all[8 * gi:8 * (gi + 1), :] for gi in range(N_GROUPS)]
        top = functools.reduce(jnp.maximum, ms)
        ws = [jnp.exp(m - top) for m in ms]
        num = sum(w * o for w, o in zip(ws, os_))
        den = sum(w * d for w, d in zip(ws, ls))
        g = g_ref[hh, rows, :]
        u_ref[:, lanes] = (num / den)[0:t_new, :] * _silu(g)


def _attn_sample(a6, cache_k, cache_v, sbias, l, batch, t_new, c_len):
    d_last = DILATED_GROUPS[-1][1]
    assert c_len == WINDOW_MAX and c_len % SAMPLE_TAIL == 0 and c_len % d_last == 0 and t_new == SAMPLE_T
    assert all(w <= SAMPLE_TAIL for w, _ in DILATED_GROUPS[:-1])
    new_blk = (None, KV_HEADS, batch * t_new, HEAD_DIM)
    tail_spec = pl.BlockSpec((None, None, SAMPLE_TAIL, KV_HEADS, HEAD_DIM),
                             lambda b, hg: (l, b, c_len // SAMPLE_TAIL - 1, hg, 0))
    strided_specs = [pl.BlockSpec((None, None, c_len // d_last, None, KV_HEADS, HEAD_DIM),
                                  lambda b, hg, t=t: (l, b, 0, t, hg, 0)) for t in range(t_new)]
    strided_view = (cache_k.shape[0], batch, c_len // d_last, d_last, N_HEADS, HEAD_DIM)
    pad = SAMPLE_DENSE - SAMPLE_TAIL
    return pl.pallas_call(
        functools.partial(_attn_sample_kernel, t_new=t_new),
        grid=(batch, N_HEADS // KV_HEADS),
        in_specs=[
            pl.BlockSpec(new_blk, lambda b, hg: (0, hg, 0, 0)),
            pl.BlockSpec(new_blk, lambda b, hg: (1, hg, 0, 0)),
            pl.BlockSpec(new_blk, lambda b, hg: (2, hg, 0, 0)),
            pl.BlockSpec(new_blk, lambda b, hg: (3, hg, 0, 0)),
            pl.BlockSpec((N_GROUPS, KV_HEADS, 8, SAMPLE_KEYS), lambda b, hg: (0, hg, 0, 0)),
            tail_spec, tail_spec,
        ] + strided_specs + strided_specs,
        out_specs=pl.BlockSpec((None, t_new, KV_HEADS * HEAD_DIM), lambda b, hg: (b, 0, hg)),
        out_shape=jax.ShapeDtypeStruct((batch, t_new, D_ATT), F32),
        scratch_shapes=[pltpu.VMEM((8, HEAD_DIM), F32), pltpu.VMEM((pad, HEAD_DIM), F32),
                        pltpu.VMEM((pad, HEAD_DIM), F32)],
        compiler_params=_params(("arbitrary", "arbitrary"), 48),
        name="attn_sample",
    )(a6, a6, a6, a6, sbias, cache_k, cache_v,
      *([cache_k.reshape(strided_view)] * t_new), *([cache_v.reshape(strided_view)] * t_new))


def _lru_gates(xc, wa_ref, ba_ref, wx_ref, bx_ref, lam_ref):
    xcb = xc.astype(BF16)
    th_r = jnp.tanh(jnp.dot(xcb, (0.5 * wa_ref[...]).astype(BF16), preferred_element_type=F32) + 0.5 * ba_ref[...])
    th_i = jnp.tanh(jnp.dot(xcb, (0.5 * wx_ref[...]).astype(BF16), preferred_element_type=F32) + 0.5 * bx_ref[...])
    nl = -lam_ref[...]
    softplus = jnp.maximum(nl, 0.0) + jnp.log1p(jnp.exp(-jnp.abs(nl)))
    half_log_a = (-0.25 * LRU_C * softplus) * (1.0 + th_r)
    t = jnp.tanh(half_log_a)
    qn = t / (t - 1.0)
    a = 1.0 - 2.0 * qn
    w = qn * (1.0 - qn)
    root = jnp.where(w > 0.0, w * lax.rsqrt(w), 0.0)
    b = root * ((1.0 + th_i) * xc)
    return a, b


def _lru_prompt_kernel(x_ref, g_ref, wc_ref, bc_ref, wa_ref, ba_ref, wx_ref, bx_ref, lam_ref, *rest, seq, cast):
    if cast:
        wsrc_ref, _, u_ref, h_ref, cs_ref, wdst_ref, xpad, a_scr, b_scr = rest
        wdst_ref[...] = wsrc_ref[...].astype(wdst_ref.dtype)
    else:
        u_ref, h_ref, cs_ref, xpad, a_scr, b_scr = rest
    seg_len = seq // SCAN_SEGS
    lead = (CONV_W - 1) * SCAN_SEGS
    row = lax.broadcasted_iota(jnp.int32, (SCAN_SEGS, LRU_BLOCK), 0)
    for i in range(SCAN_SEGS):
        xpad[pl.ds(lead + i, seg_len, stride=SCAN_SEGS), :] = x_ref[i * seg_len:(i + 1) * seg_len, :]
    for k in range(1, CONV_W):
        tail = xpad[lead + (seg_len - k) * SCAN_SEGS:lead + (seg_len - k + 1) * SCAN_SEGS, :]
        xpad[lead - k * SCAN_SEGS:lead - (k - 1) * SCAN_SEGS, :] = jnp.where(row == 0, 0.0, pltpu.roll(tail, 1, 0))
    xc = bc_ref[...]
    for j in range(CONV_W):
        xc = xc + xpad[j * SCAN_SEGS:j * SCAN_SEGS + seq, :] * wc_ref[j:j + 1, :]
    a, b = _lru_gates(xc, wa_ref, ba_ref, wx_ref, bx_ref, lam_ref)
    a_scr[...] = a
    b_scr[...] = b

    def body(c, carry):
        h_loc, a_cum = carry
        rows = pl.ds(pl.multiple_of(c * SCAN_SEGS, SCAN_SEGS), SCAN_SEGS)
        ca = a_scr[rows, :]
        h_loc = ca * h_loc + b_scr[rows, :]
        a_cum = ca * a_cum
        b_scr[rows, :] = h_loc
        a_scr[rows, :] = a_cum
        return h_loc, a_cum

    init = (jnp.zeros((SCAN_SEGS, LRU_BLOCK), F32), jnp.ones((SCAN_SEGS, LRU_BLOCK), F32))
    h_end, a_end = lax.fori_loop(0, seg_len, body, init, unroll=8)
    carry = jnp.zeros((1, LRU_BLOCK), F32)
    carries = jnp.zeros((SCAN_SEGS, LRU_BLOCK), F32)
    for i in range(1, SCAN_SEGS):
        carry = h_end[i - 1:i, :] + a_end[i - 1:i, :] * carry
        carries = jnp.where(row == i, carry, carries)
    shape3 = (seg_len, SCAN_SEGS, LRU_BLOCK)
    y = b_scr[...].reshape(shape3) + a_scr[...].reshape(shape3) * carries[None]
    b_scr[...] = y.reshape(seq, LRU_BLOCK)
    for i in range(SCAN_SEGS):
        rows = slice(i * seg_len, (i + 1) * seg_len)
        g = g_ref[rows, :]
        u_ref[rows, :] = (b_scr[pl.ds(i, seg_len, stride=SCAN_SEGS), :] * _silu(g)).astype(u_ref.dtype)
    h_ref[...] = b_scr[seq - 1:seq, :]
    cs_ref[...] = x_ref[seq - (CONV_W - 1):seq, :]


def _lru_weight_specs(l, n_of):
    return [
        pl.BlockSpec((None, CONV_W, LRU_BLOCK), lambda *g: (l, 0, n_of(*g))),
        pl.BlockSpec((None, 1, LRU_BLOCK), lambda *g: (l, 0, n_of(*g))),
        pl.BlockSpec((None, None, LRU_BLOCK, LRU_BLOCK), lambda *g: (l, n_of(*g), 0, 0)),
        pl.BlockSpec((None, 1, LRU_BLOCK), lambda *g: (l, 0, n_of(*g))),
        pl.BlockSpec((None, None, LRU_BLOCK, LRU_BLOCK), lambda *g: (l, n_of(*g), 0, 0)),
        pl.BlockSpec((None, 1, LRU_BLOCK), lambda *g: (l, 0, n_of(*g))),
        pl.BlockSpec((None, 1, LRU_BLOCK), lambda *g: (l, 0, n_of(*g))),
    ]


def _lru_prompt(a6, lru_w, l, batch, seq, w_in=None, w_half=None):
    blk = (None, None, None, seq, LRU_BLOCK)
    n_of = lambda b, n: n
    cast = w_in is not None
    steps = batch * N_LRU_BLOCKS
    w_rows = D_MODEL // (2 * steps)
    w_block = lambda b, n: steps + b * N_LRU_BLOCKS + n
    in_specs = [
        pl.BlockSpec(blk, lambda b, n: (4, n, b, 0, 0)),
        pl.BlockSpec(blk, lambda b, n: (5, n, b, 0, 0)),
    ] + _lru_weight_specs(l, n_of)
    out_specs = [
        pl.BlockSpec((None, seq, LRU_BLOCK), lambda b, n: (b, 0, n)),
        pl.BlockSpec((None, 1, LRU_BLOCK), lambda b, n: (b, 0, n)),
        pl.BlockSpec((None, CONV_W - 1, LRU_BLOCK), lambda b, n: (b, 0, n)),
    ]
    out_shape = [
        jax.ShapeDtypeStruct((batch, seq, D_LRU), BF16),
        jax.ShapeDtypeStruct((batch, 1, D_LRU), F32),
        jax.ShapeDtypeStruct((batch, CONV_W - 1, D_LRU), F32),
    ]
    args = [a6, a6, *lru_w]
    if cast:
        in_specs.append(pl.BlockSpec((None, w_rows, D_IN), lambda b, n: (l + 1, w_block(b, n), 0)))
        in_specs.append(pl.BlockSpec(memory_space=pl.ANY))
        out_specs.append(pl.BlockSpec((w_rows, D_IN), lambda b, n: (w_block(b, n), 0)))
        out_shape.append(jax.ShapeDtypeStruct((D_MODEL, D_IN), BF16))
        args += [w_in, w_half]
    res = pl.pallas_call(
        functools.partial(_lru_prompt_kernel, seq=seq, cast=cast),
        grid=(batch, N_LRU_BLOCKS),
        in_specs=in_specs,
        out_specs=out_specs,
        out_shape=out_shape,
        scratch_shapes=[pltpu.VMEM((seq + (CONV_W - 1) * SCAN_SEGS, LRU_BLOCK), F32),
                        pltpu.VMEM((seq, LRU_BLOCK), F32), pltpu.VMEM((seq, LRU_BLOCK), F32)],
        input_output_aliases={len(args) - 1: len(out_shape) - 1} if cast else {},
        compiler_params=_params(("arbitrary", "arbitrary"), 40),
        name="lru_prompt",
    )(*args)
    return res if cast else (*res, None)


def _lru_sample_kernel(x_ref, g_ref, sc_ref, h0_ref, wc_ref, bc_ref, wa_ref, ba_ref, wx_ref, bx_ref, lam_ref,
                       u_ref, h_ref, cs_ref, xc_scr, a_scr, b_scr, y_scr, *, batch, t_new):
    n_state = CONV_W - 1

    def ext_row(b, i):
        if i < n_state:
            return sc_ref[b, i:i + 1, :]
        r = b * t_new + i - n_state
        return x_ref[r:r + 1, :]

    for b in range(batch):
        for t in range(t_new):
            acc = bc_ref[...] + ext_row(b, t) * wc_ref[0:1, :]
            for j in range(1, CONV_W):
                acc = acc + ext_row(b, t + j) * wc_ref[j:j + 1, :]
            xc_scr[b * t_new + t:b * t_new + t + 1, :] = acc
    a, bb = _lru_gates(xc_scr[...], wa_ref, ba_ref, wx_ref, bx_ref, lam_ref)
    a_scr[...] = a
    b_scr[...] = bb
    for b in range(batch):
        h = h0_ref[b:b + 1, :]
        for t in range(t_new):
            r = b * t_new + t
            h = a_scr[r:r + 1, :] * h + b_scr[r:r + 1, :]
            y_scr[r:r + 1, :] = h
        h_ref[b:b + 1, :] = h
        for i in range(n_state):
            cs_ref[b, i:i + 1, :] = ext_row(b, t_new + i)
    for b in range(batch):
        g = g_ref[b * t_new:(b + 1) * t_new, :]
        u_ref[b] = y_scr[b * t_new:(b + 1) * t_new, :] * _silu(g)


def _lru_sample(a6, state_conv, state_h, lru_w, l, batch, t_new):
    blk = (None, None, batch * t_new, LRU_BLOCK)
    n_of = lambda n: n
    rows = batch * t_new
    return pl.pallas_call(
        functools.partial(_lru_sample_kernel, batch=batch, t_new=t_new),
        grid=(N_LRU_BLOCKS,),
        in_specs=[
            pl.BlockSpec(blk, lambda n: (4, n, 0, 0)),
            pl.BlockSpec(blk, lambda n: (5, n, 0, 0)),
            pl.BlockSpec((None, batch, CONV_W - 1, LRU_BLOCK), lambda n: (l, 0, 0, n)),
            pl.BlockSpec((None, batch, LRU_BLOCK), lambda n: (l, 0, n)),
        ] + _lru_weight_specs(l, n_of),
        out_specs=[
            pl.BlockSpec((batch, t_new, LRU_BLOCK), lambda n: (0, 0, n)),
            pl.BlockSpec((batch, LRU_BLOCK), lambda n: (0, n)),
            pl.BlockSpec((batch, CONV_W - 1, LRU_BLOCK), lambda n: (0, 0, n)),
        ],
        out_shape=[
            jax.ShapeDtypeStruct((batch, t_new, D_LRU), F32),
            jax.ShapeDtypeStruct((batch, D_LRU), F32),
            jax.ShapeDtypeStruct((batch, CONV_W - 1, D_LRU), F32),
        ],
        scratch_shapes=[pltpu.VMEM((rows, LRU_BLOCK), F32)] * 4,
        compiler_params=_params(("arbitrary",), 32),
        name="lru_sample",
    )(a6, a6, state_conv, state_h, *lru_w)


def _outproj_kernel(ua_ref, ul_ref, w_ref, x_ref, gate_ref, gp_ref, *rest, n_tiles, has_next):
    if has_next:
        shift_ref, scale_ref, gn_ref, y_ref, h_ref, acc_a, acc_b = rest
    else:
        y_ref, acc_a, acc_b = rest
    i = pl.program_id(0)
    tm = x_ref.shape[0]
    chunk = min(tm, EPILOGUE_CHUNK)
    per_row = gate_ref.shape[0] == tm

    def matmul(acc):
        acc[...] = (jnp.dot(ua_ref[...].astype(BF16), w_ref[0:D_ATT, :], preferred_element_type=F32)
                    + jnp.dot(ul_ref[...].astype(BF16), w_ref[D_ATT:D_MODEL, :], preferred_element_type=F32))

    def epilogue(acc):
        for c in range(tm // chunk):
            rows = slice(c * chunk, (c + 1) * chunk)
            mrows = rows if per_row else slice(None)
            y = acc[rows, :]
            y = y * lax.rsqrt(jnp.mean(y * y, axis=-1, keepdims=True) + EPS)
            out = x_ref[rows, :] + gate_ref[mrows, :] * (y * gp_ref[...])
            y_ref[rows, :] = out
            if has_next:
                hn = out * lax.rsqrt(jnp.mean(out * out, axis=-1, keepdims=True) + EPS)
                hn = hn * gn_ref[...]
                h_ref[rows, :] = (hn * (1.0 + scale_ref[mrows, :]) + shift_ref[mrows, :]).astype(h_ref.dtype)

    accs = (acc_a, acc_b)

    @pl.when(i == 0)
    def _():
        matmul(accs[0])

    for parity in range(2):
        @pl.when((i > 0) & (i < n_tiles) & (i % 2 == parity))
        def _(parity=parity):
            epilogue(accs[1 - parity])
            matmul(accs[parity])

    @pl.when(i == n_tiles)
    def _():
        epilogue(accs[(n_tiles - 1) % 2])


def _outproj(u_att, u_lru, w_out, x2, gate, mod_idx, g_post, l, nxt, *, tm, rows_per_mod, vmem_mib):
    m_rows = x2.shape[0]
    n_tiles = m_rows // tm
    mm_tile = lambda i: jnp.minimum(i, n_tiles - 1)
    ep_tile = lambda i: jnp.maximum(i - 1, 0)
    row_spec = pl.BlockSpec((tm, D_MODEL), lambda i: (ep_tile(i), 0))
    in_specs = [
        pl.BlockSpec((tm, D_ATT), lambda i: (mm_tile(i), 0)),
        pl.BlockSpec((tm, D_LRU), lambda i: (mm_tile(i), 0)),
        pl.BlockSpec((D_MODEL, D_MODEL), lambda i: (0, 0), pipeline_mode=pl.Buffered(1)),
        row_spec,
        _mod_specs(mod_idx, 2, tm, rows_per_mod, ep_tile),
        pl.BlockSpec((None, 1, D_MODEL), lambda i: (l, 0, 0)),
    ]
    args = [u_att, u_lru, w_out, x2, gate, g_post]
    out_specs = [row_spec]
    out_shape = [jax.ShapeDtypeStruct((m_rows, D_MODEL), F32)]
    if nxt is not None:
        shift_n, scale_n, idx_n, g_pre, l_next = nxt
        in_specs += [
            _mod_specs(idx_n, 0, tm, rows_per_mod, ep_tile),
            _mod_specs(idx_n, 1, tm, rows_per_mod, ep_tile),
            pl.BlockSpec((None, 1, D_MODEL), lambda i: (l_next, 0, 0)),
        ]
        args += [shift_n, scale_n, g_pre]
        out_specs.append(row_spec)
        out_shape.append(jax.ShapeDtypeStruct((m_rows, D_MODEL), BF16))
    res = pl.pallas_call(
        functools.partial(_outproj_kernel, n_tiles=n_tiles, has_next=nxt is not None),
        grid=(n_tiles + 1,),
        in_specs=in_specs,
        out_specs=out_specs,
        out_shape=out_shape,
        scratch_shapes=[pltpu.VMEM((tm, D_MODEL), F32)] * 2,
        compiler_params=_params(("arbitrary",), vmem_mib),
        name="outproj",
    )(*args)
    return res if nxt is not None else (res[0], None)


def kernel(x_prompt, x_sample, cache_k, cache_v, state_h, state_conv, c_prompt, c_sample, rel_table, w_ada, b_ada,
           g_pre, w_in, w_conv, b_conv, w_a, b_a, w_x, b_x, lam, w_out, g_post):
    depth = w_in.shape[0]
    bp, seq, _ = x_prompt.shape
    bs, t_new, _ = x_sample.shape
    c_len = cache_k.shape[2]
    assert bp + bs <= MOD_ROWS and t_new <= 8
    assert seq % (L_BAND * DILATED_GROUPS[-1][1]) == 0 and c_len >= WINDOW_MAX

    c_all = jnp.concatenate([c_prompt, c_sample, jnp.zeros((MOD_ROWS - bp - bs, D_MODEL), F32)], axis=0)
    mod = _ada(c_all, w_ada, b_ada)
    mod_tab = mod.reshape(depth * MOD_ROWS * 3, 1, D_MODEL)
    pbias, sbias = _bias_tables(rel_table, c_len, t_new)

    w_in_l = w_in[0].astype(BF16)
    g_pre3 = g_pre.reshape(depth, 1, D_MODEL)
    g_post3 = g_post.reshape(depth, 1, D_MODEL)

    xp = x_prompt.reshape(bp * seq, D_MODEL)
    xs = x_sample.reshape(bs * t_new, D_MODEL)
    rows_s = bs * t_new
    lru_w = (w_conv, b_conv.reshape(depth, 1, D_LRU), w_a, b_a.reshape(depth, 1, D_LRU),
             w_x, b_x.reshape(depth, 1, D_LRU), lam.reshape(depth, 1, D_LRU))
    kv_p = kv_s = None
    hp_l, cp_l, hs_l, cs_l = [], [], [], []
    tm_norm, tm_in, tm_out = PROMPT_ROW_TILES
    p_idx = lambda l: (lambda b, which: (l * MOD_ROWS + b) * 3 + which)
    mod_s = [jnp.repeat(mod[l, bp:bp + bs], t_new, axis=0) for l in range(depth)]
    shift_s = [m[:, :D_MODEL] for m in mod_s]
    scale_s = [m[:, D_MODEL:2 * D_MODEL] for m in mod_s]
    gate_s = [m[:, 2 * D_MODEL:] for m in mod_s]
    hp = _prenorm(xp, mod_tab, mod_tab, p_idx(0), g_pre3, 0, tm=tm_norm, rows_per_mod=seq // tm_norm)
    hs = _prenorm(xs, shift_s[0], scale_s[0], None, g_pre3, 0, tm=rows_s, rows_per_mod=1)
    for l in range(depth):
        more = l + 1 < depth
        a6, kp, vp = _inproj(hp, w_in_l, l, depth, kv_p, tm=tm_in, vmem_mib=BIG_CALL_VMEM_MIB)
        kv_p = (kp, vp)
        a6 = a6.reshape(N_SEG, N_HEADS, bp, seq, HEAD_DIM)
        u_att, w_out_l, w_half = _attn_prompt(a6, pbias, w_out, l, bp, seq, w_in if more else None)
        u_lru, h_last, conv, w_in_next = _lru_prompt(a6, lru_w, l, bp, seq, w_in if more else None, w_half)
        nxt = (mod_tab, mod_tab, p_idx(l + 1), g_pre3, l + 1) if more else None
        xp, hp = _outproj(u_att.reshape(bp * seq, D_ATT), u_lru.reshape(bp * seq, D_LRU), w_out_l, xp, mod_tab,
                          p_idx(l), g_post3, l, nxt, tm=tm_out, rows_per_mod=seq // tm_out,
                          vmem_mib=BIG_CALL_VMEM_MIB)
        hp_l.append(h_last.reshape(bp, D_LRU))
        cp_l.append(conv)
        a6, ks, vs = _inproj(hs, w_in_l, l, depth, kv_s, tm=rows_s, vmem_mib=SMALL_CALL_VMEM_MIB)
        kv_s = (ks, vs)
        u_att = _attn_sample(a6, cache_k, cache_v, sbias, l, bs, t_new, c_len)
        u_lru, h_last, conv = _lru_sample(a6, state_conv, state_h, lru_w, l, bs, t_new)
        nxt = (shift_s[l + 1], scale_s[l + 1], None, g_pre3, l + 1) if more else None
        xs, hs = _outproj(u_att.reshape(rows_s, D_ATT), u_lru.reshape(rows_s, D_LRU), w_out_l, xs, gate_s[l], None,
                          g_post3, l, nxt, tm=rows_s, rows_per_mod=1, vmem_mib=SMALL_CALL_VMEM_MIB)
        hs_l.append(h_last)
        cs_l.append(conv)
        w_in_l = w_in_next

    kp, vp = kv_p
    ks, vs = kv_s
    return (xp.reshape(bp, seq, D_MODEL), xs.reshape(bs, t_new, D_MODEL),
            kp.reshape(depth, bp, seq, N_HEADS, HEAD_DIM), vp.reshape(depth, bp, seq, N_HEADS, HEAD_DIM),
            jnp.stack(hp_l), jnp.stack(cp_l),
            ks.reshape(depth, bs, t_new, N_HEADS, HEAD_DIM), vs.reshape(depth, bs, t_new, N_HEADS, HEAD_DIM),
            jnp.stack(hs_l), jnp.stack(cs_l))
```

```python
import functools
import math

import numpy as np
import jax
import jax.numpy as jnp
from jax import lax
from jax.experimental import pallas as pl
from jax.experimental.pallas import tpu as pltpu

D_MODEL = 4096
D_ATT = 2048
D_LRU = 2048
HEAD_DIM = 128
N_HEADS = 16
N_LRU_BLOCKS = 16
LRU_BLOCK = 128
CONV_W = 4
LRU_C = 8.0
DILATED_GROUPS = ((128, 1), (512, 4), (2048, 16))
N_GROUPS = len(DILATED_GROUPS)
WINDOW_MAX = 2048
N_BUCKETS = 32
MAX_EXACT = N_BUCKETS // 2
MAX_DISTANCE = WINDOW_MAX
EPS = 1e-6
ATT_SCALE = HEAD_DIM ** -0.5
D_IN = 4 * D_ATT + 2 * D_LRU
N_SEG = D_IN // D_ATT
SEG_K, SEG_V = 1, 2
L_BAND = 128
MASKED = -1e30
MOD_ROWS = 16
SAMPLE_TAIL = 512
SAMPLE_DENSE = SAMPLE_TAIL + 128
SAMPLE_T = 4
SAMPLE_KEYS = SAMPLE_DENSE + SAMPLE_T * (WINDOW_MAX // DILATED_GROUPS[-1][1])
MIB = 1024 * 1024
PROMPT_ROW_TILES = (512, 1024, 128)
BIG_CALL_VMEM_MIB = 56
SMALL_CALL_VMEM_MIB = 48
NORM_CHUNK = 64
EPILOGUE_CHUNK = 128
ATTN_GROUP = 8
SCAN_SEGS = 8
KV_HEADS = 8

F32 = jnp.float32
BF16 = jnp.bfloat16


def _params(semantics, vmem_mib):
    return pltpu.CompilerParams(dimension_semantics=semantics, vmem_limit_bytes=vmem_mib * MIB)


def _silu(x):
    half = 0.5 * x
    return half + half * jnp.tanh(half)


def _rel_bucket_np(dist):
    d = dist.astype(np.float32)
    large = np.float32(MAX_EXACT) + np.log(np.maximum(d, np.float32(1.0)) / np.float32(MAX_EXACT)) / np.float32(
        math.log(MAX_DISTANCE / MAX_EXACT)) * np.float32(N_BUCKETS - MAX_EXACT)
    large = np.minimum(large.astype(np.int32), N_BUCKETS - 1)
    return np.where(dist < MAX_EXACT, dist, large).astype(np.int32)


def _prompt_bucket_index():
    qi = np.arange(L_BAND)[:, None]
    kj = np.arange(2 * L_BAND)[None, :]
    dist = qi + L_BAND - kj
    band = (dist >= 0) & (dist <= L_BAND)
    out = []
    for _, dil in DILATED_GROUPS:
        b = _rel_bucket_np(np.clip(dist, 0, L_BAND) * dil)
        out.append(np.where(band, b, -1))
    return np.stack(out).astype(np.int32)


def _sample_bucket_index(c_len, t_new):
    d_last = DILATED_GROUPS[-1][1]
    grp = c_len // d_last
    t = np.arange(8)[:, None]
    col = np.arange(SAMPLE_KEYS)[None, :]
    dense = col < SAMPLE_DENSE
    owner = np.where(dense, -1, (col - SAMPLE_DENSE) // grp)
    idx = np.where(dense, c_len - SAMPLE_TAIL + col, ((col - SAMPLE_DENSE) % grp) * d_last + owner)
    real = np.where(dense, col < SAMPLE_TAIL + t_new, owner < t_new)
    delta = c_len + t - idx
    out = []
    for gi, (window, dil) in enumerate(DILATED_GROUPS):
        valid = real & (delta >= 0) & (delta % dil == 0) & (delta <= window) & (t < t_new)
        if gi == N_GROUPS - 1:
            valid &= np.where(dense, delta < dil, owner == t)
        else:
            valid &= dense
        b = _rel_bucket_np(np.clip(delta, 0, window))
        b = np.where(valid, b, -1)
        b = np.where(t >= t_new, 0, b)
        out.append(b)
    return np.stack(out).astype(np.int32)


def _ada_kernel(c_ref, w_ref, b_ref, o_ref):
    c = c_ref[...]
    a = _silu(c).astype(BF16)
    o_ref[...] = jnp.dot(a, w_ref[...].astype(BF16), preferred_element_type=F32) + b_ref[...]


def _ada(c_all, w_ada, b_ada):
    depth = w_ada.shape[0]
    tn = 512
    return pl.pallas_call(
        _ada_kernel,
        grid=(depth, 3 * D_MODEL // tn),
        in_specs=[
            pl.BlockSpec((MOD_ROWS, D_MODEL), lambda l, j: (0, 0)),
            pl.BlockSpec((None, D_MODEL, tn), lambda l, j: (l, 0, j)),
            pl.BlockSpec((None, 1, tn), lambda l, j: (l, 0, j)),
        ],
        out_specs=pl.BlockSpec((None, MOD_ROWS, tn), lambda l, j: (l, 0, j)),
        out_shape=jax.ShapeDtypeStruct((depth, MOD_ROWS, 3 * D_MODEL), F32),
        compiler_params=_params(("arbitrary", "arbitrary"), 40),
        name="ada",
    )(c_all, w_ada, b_ada.reshape(depth, 1, 3 * D_MODEL))


def _bias_kernel(tab_ref, pidx_ref, sidx_ref, pb_ref, sb_ref):
    h = pl.program_id(0)
    for idx_ref, out_ref in ((pidx_ref, pb_ref), (sidx_ref, sb_ref)):
        for g in range(N_GROUPS):
            idx = idx_ref[g]
            acc = jnp.full(idx.shape, MASKED, F32)
            for b in range(N_BUCKETS):
                acc = jnp.where(idx == b, tab_ref[b, h], acc)
            out_ref[g] = acc


def _bias_tables(rel_table, c_len, t_new):
    pidx = jnp.asarray(_prompt_bucket_index())
    sidx = jnp.asarray(_sample_bucket_index(c_len, t_new))
    return pl.pallas_call(
        _bias_kernel,
        grid=(N_HEADS,),
        in_specs=[
            pl.BlockSpec(memory_space=pltpu.SMEM),
            pl.BlockSpec((N_GROUPS, L_BAND, 2 * L_BAND), lambda h: (0, 0, 0)),
            pl.BlockSpec((N_GROUPS, 8, SAMPLE_KEYS), lambda h: (0, 0, 0)),
        ],
        out_specs=[
            pl.BlockSpec((N_GROUPS, None, L_BAND, 2 * L_BAND), lambda h: (0, h, 0, 0)),
            pl.BlockSpec((N_GROUPS, None, 8, SAMPLE_KEYS), lambda h: (0, h, 0, 0)),
        ],
        out_shape=[
            jax.ShapeDtypeStruct((N_GROUPS, N_HEADS, L_BAND, 2 * L_BAND), F32),
            jax.ShapeDtypeStruct((N_GROUPS, N_HEADS, 8, SAMPLE_KEYS), F32),
        ],
        compiler_params=_params(("arbitrary",), 32),
        name="bias_tables",
    )(rel_table, pidx, sidx)


def _mod_specs(mod_idx, which, tm, rows_per_mod, tile_of=lambda i: i):
    if mod_idx is None:
        return pl.BlockSpec((tm, D_MODEL), lambda i, *_: (tile_of(i), 0))
    return pl.BlockSpec((None, 1, D_MODEL), lambda i, *_: (mod_idx(tile_of(i) // rows_per_mod, which), 0, 0))


def _prenorm_kernel(x_ref, shift_ref, scale_ref, g_ref, h_ref):
    tm = x_ref.shape[0]
    chunk = min(tm, NORM_CHUNK)
    per_row = shift_ref.shape[0] == tm

    def norm(c, _):
        rows = pl.ds(pl.multiple_of(c * chunk, chunk), chunk)
        mrows = rows if per_row else slice(None)
        x = x_ref[rows, :]
        y = x * lax.rsqrt(jnp.mean(x * x, axis=-1, keepdims=True) + EPS)
        y = y * g_ref[...]
        h_ref[rows, :] = (y * (1.0 + scale_ref[mrows, :]) + shift_ref[mrows, :]).astype(h_ref.dtype)
        return 0

    lax.fori_loop(0, tm // chunk, norm, 0)


def _prenorm(x2, shift, scale, mod_idx, g_pre, l, *, tm, rows_per_mod):
    m_rows = x2.shape[0]
    return pl.pallas_call(
        _prenorm_kernel,
        grid=(m_rows // tm,),
        in_specs=[
            pl.BlockSpec((tm, D_MODEL), lambda i: (i, 0)),
            _mod_specs(mod_idx, 0, tm, rows_per_mod),
            _mod_specs(mod_idx, 1, tm, rows_per_mod),
            pl.BlockSpec((None, 1, D_MODEL), lambda i: (l, 0, 0)),
        ],
        out_specs=pl.BlockSpec((tm, D_MODEL), lambda i: (i, 0)),
        out_shape=jax.ShapeDtypeStruct((m_rows, D_MODEL), BF16),
        compiler_params=_params(("arbitrary",), 40),
        name="prenorm",
    )(x2, shift, scale, g_pre)


def _inproj_kernel(h_ref, w_ref, *rest, nj_seg, layer):
    a_ref, k_hbm, v_hbm, stage, sem = rest[-5:]
    i, j = pl.program_id(0), pl.program_id(1)
    tm = h_ref.shape[0]
    seg, part = j // nj_seg, j % nj_seg
    first_kv, last_kv = SEG_K * nj_seg, (SEG_V + 1) * nj_seg - 1

    def head_copy(dst_hbm, group):
        rows = pl.ds(i * tm, tm)
        heads = pl.ds(group * KV_HEADS, KV_HEADS)
        return pltpu.make_async_copy(stage, dst_hbm.at[layer, rows, heads, :], sem.at[0])

    acc = jnp.dot(h_ref[...], w_ref[...], preferred_element_type=F32)
    for hh in range(KV_HEADS):
        a_ref[hh] = acc[:, hh * HEAD_DIM:(hh + 1) * HEAD_DIM]

    @pl.when((j > first_kv) & (j <= last_kv + 1))
    def _():
        head_copy(k_hbm, 0).wait()

    def stage_and_send(dst_hbm):
        flat = stage.reshape(tm * KV_HEADS, HEAD_DIM)
        for hh in range(KV_HEADS):
            flat[pl.ds(hh, tm, stride=KV_HEADS), :] = acc[:, hh * HEAD_DIM:(hh + 1) * HEAD_DIM]
        head_copy(dst_hbm, part).start()

    @pl.when(seg == SEG_K)
    def _():
        stage_and_send(k_hbm)

    @pl.when(seg == SEG_V)
    def _():
        stage_and_send(v_hbm)


def _inproj(h2, w_in, l, depth, kv_prev, *, tm, vmem_mib):
    m_rows = h2.shape[0]
    tn = KV_HEADS * HEAD_DIM
    nj_seg = D_ATT // tn
    in_specs = [
        pl.BlockSpec((tm, D_MODEL), lambda i, j: (i, 0)),
        pl.BlockSpec((D_MODEL, tn), lambda i, j: (0, j)),
    ]
    args = [h2, w_in]
    aliases = {}
    if kv_prev is not None:
        in_specs += [pl.BlockSpec(memory_space=pl.ANY)] * 2
        args += list(kv_prev)
        aliases = {2: 1, 3: 2}
    kv_shape = jax.ShapeDtypeStruct((depth, m_rows, N_HEADS, HEAD_DIM), F32)
    return pl.pallas_call(
        functools.partial(_inproj_kernel, nj_seg=nj_seg, layer=l),
        grid=(m_rows // tm, N_SEG * nj_seg),
        in_specs=in_specs,
        out_specs=[
            pl.BlockSpec((None, KV_HEADS, tm, HEAD_DIM), lambda i, j: (j // nj_seg, j % nj_seg, i, 0)),
            pl.BlockSpec(memory_space=pl.ANY),
            pl.BlockSpec(memory_space=pl.ANY),
        ],
        out_shape=[jax.ShapeDtypeStruct((N_SEG, N_HEADS, m_rows, HEAD_DIM), F32), kv_shape, kv_shape],
        scratch_shapes=[pltpu.VMEM((tm, KV_HEADS, HEAD_DIM), F32), pltpu.SemaphoreType.DMA((1,))],
        input_output_aliases=aliases,
        compiler_params=_params(("arbitrary", "arbitrary"), vmem_mib),
        name="inproj",
    )(*args)


def _attn_branches(q_ref, k_ref, v_ref, bias_ref, qa, ka, va, qb, kb, vb, o_scr, e_scr, *, seq):
    n_blk = seq // L_BAND

    def regroup(dst, src, stride):
        sub = seq // stride
        for r in range(stride):
            dst[r * sub:(r + 1) * sub, :] = src[pl.ds(r, sub, stride=stride), :]

    def branch(gi, dil, q_src, k_src, v_src):
        per_res = n_blk // dil
        grp = ATTN_GROUP
        assert n_blk % grp == 0 and (grp % per_res == 0 or per_res % grp == 0)

        def blocks(src, lo, hi):
            return src[lo * L_BAND:hi * L_BAND, :].astype(BF16).reshape(hi - lo, L_BAND, HEAD_DIM)

        for g0 in range(0, n_blk, grp):
            q3 = blocks(q_src, g0, g0 + grp)
            k3 = blocks(k_src, g0, g0 + grp)
            v3 = blocks(v_src, g0, g0 + grp)
            if per_res > 1:
                if g0 == 0:
                    kp = jnp.concatenate([k3[:1], k3[:-1]], axis=0)
                    vp = jnp.concatenate([v3[:1], v3[:-1]], axis=0)
                else:
                    kp = blocks(k_src, g0 - 1, g0 + grp - 1)
                    vp = blocks(v_src, g0 - 1, g0 + grp - 1)
                kk = jnp.concatenate([kp, k3], axis=1)
                vv = jnp.concatenate([vp, v3], axis=1)
                blk = lax.broadcasted_iota(jnp.int32, (grp, 1, 2 * L_BAND), 0) + g0
                col = lax.broadcasted_iota(jnp.int32, (grp, 1, 2 * L_BAND), 2)
                first = jnp.where((blk % per_res == 0) & (col < L_BAND), MASKED, 0.0)
                bias = bias_ref[gi][None] + first
            else:
                kk, vv = k3, v3
                bias = bias_ref[gi, :, L_BAND:][None]
            s = jnp.einsum('bqd,bkd->bqk', q3, kk, preferred_element_type=F32) * ATT_SCALE + bias
            mx = jnp.max(s, axis=-1, keepdims=True)
            p = jnp.exp(s - mx)
            den = jnp.sum(p, axis=-1, keepdims=True)
            o = jnp.einsum('bqk,bkd->bqd', p.astype(BF16), vv, preferred_element_type=F32) / den
            lse = jnp.broadcast_to(mx + jnp.log(den), (grp, L_BAND, HEAD_DIM))
            o = o.reshape(grp * L_BAND, HEAD_DIM)
            lse = lse.reshape(grp * L_BAND, HEAD_DIM)
            if dil == 1:
                o_scr[gi, g0 * L_BAND:(g0 + grp) * L_BAND, :] = o
                e_scr[gi, g0 * L_BAND:(g0 + grp) * L_BAND, :] = lse
            else:
                sub = seq // dil
                for r in range(g0 // per_res, (g0 + grp) // per_res):
                    part = slice((r * per_res - g0) * L_BAND, ((r + 1) * per_res - g0) * L_BAND)
                    o_scr[gi, pl.ds(r, sub, stride=dil), :] = o[part, :]
                    e_scr[gi, pl.ds(r, sub, stride=dil), :] = lse[part, :]

    (_, d0), (_, d1), (_, d2) = DILATED_GROUPS
    assert d0 == 1 and d2 == d1 * d1
    branch(0, d0, q_ref, k_ref, v_ref)
    for dst, src in ((qa, q_ref), (ka, k_ref), (va, v_ref)):
        regroup(dst, src, d1)
    branch(1, d1, qa, ka, va)
    for dst, src in ((qb, qa), (kb, ka), (vb, va)):
        regroup(dst, src, d1)
    branch(2, d2, qb, kb, vb)


def _attn_combine(g_ref, u_ref, o_scr, e_scr, *, seq):
    chunk = 256

    def combine(c, _):
        rows = pl.ds(pl.multiple_of(c * chunk, chunk), chunk)
        es = [e_scr[gi, rows, :] for gi in range(N_GROUPS)]
        top = functools.reduce(jnp.maximum, es)
        ws = [jnp.exp(e - top) for e in es]
        num = sum(w * o_scr[gi, rows, :] for gi, w in enumerate(ws))
        den = sum(ws)
        g = g_ref[rows, :]
        u_ref[rows, :] = ((num / den) * _silu(g)).astype(u_ref.dtype)
        return 0

    lax.fori_loop(0, seq // chunk, combine, 0)


def _mixers_prompt_kernel(q_ref, k_ref, v_ref, ga_ref, bias_ref, x_ref, gl_ref, wc_ref, bc_ref, wa_ref, ba_ref,
                          wx_ref, bx_ref, lam_ref, wsrc_ref, *rest, seq, cast_next):
    if cast_next:
        wsrc2_ref, ua_ref, ul_ref, h_ref, cs_ref, wdst_ref, wdst2_ref = rest[:7]
        wdst2_ref[...] = wsrc2_ref[...].astype(wdst2_ref.dtype)
        scratch = rest[7:]
    else:
        ua_ref, ul_ref, h_ref, cs_ref, wdst_ref = rest[:5]
        scratch = rest[5:]
    qa, ka, va, qb, kb, vb, o_scr, e_scr, xpad, a_scr, b_scr = scratch
    wdst_ref[...] = wsrc_ref[...].astype(wdst_ref.dtype)
    _attn_branches(q_ref, k_ref, v_ref, bias_ref, qa, ka, va, qb, kb, vb, o_scr, e_scr, seq=seq)
    _lru_prepare(x_ref, wc_ref, bc_ref, wa_ref, ba_ref, wx_ref, bx_ref, lam_ref, xpad, a_scr, b_scr, seq=seq)
    _lru_scan(x_ref, gl_ref, ul_ref, h_ref, cs_ref, a_scr, b_scr, seq=seq)
    _attn_combine(ga_ref, ua_ref, o_scr, e_scr, seq=seq)


def _mixers_prompt(a6, pbias, lru_w, w_out, l, batch, seq, w_in=None):
    assert N_HEADS == N_LRU_BLOCKS
    blk = (None, None, None, seq, HEAD_DIM)
    steps = batch * N_HEADS
    step = lambda b, i: b * N_HEADS + i
    slot_spec = lambda slot: pl.BlockSpec(blk, lambda b, i: (slot, i, b, 0, 0))
    in_specs = [slot_spec(0), slot_spec(1), slot_spec(2), slot_spec(3),
                pl.BlockSpec((N_GROUPS, None, L_BAND, 2 * L_BAND), lambda b, i: (0, i, 0, 0)),
                slot_spec(4), slot_spec(5)] + _lru_weight_specs(l, lambda b, i: i) + [
        pl.BlockSpec((None, D_MODEL // steps, D_MODEL), lambda b, i: (l, step(b, i), 0)),
    ]
    col_spec = pl.BlockSpec((None, seq, HEAD_DIM), lambda b, i: (b, 0, i))
    out_specs = [
        col_spec, col_spec,
        pl.BlockSpec((None, 1, LRU_BLOCK), lambda b, i: (b, 0, i)),
        pl.BlockSpec((None, CONV_W - 1, LRU_BLOCK), lambda b, i: (b, 0, i)),
        pl.BlockSpec((D_MODEL // steps, D_MODEL), lambda b, i: (step(b, i), 0)),
    ]
    out_shape = [
        jax.ShapeDtypeStruct((batch, seq, D_ATT), BF16),
        jax.ShapeDtypeStruct((batch, seq, D_LRU), BF16),
        jax.ShapeDtypeStruct((batch, 1, D_LRU), F32),
        jax.ShapeDtypeStruct((batch, CONV_W - 1, D_LRU), F32),
        jax.ShapeDtypeStruct((D_MODEL, D_MODEL), BF16),
    ]
    args = [a6, a6, a6, a6, pbias, a6, a6, *lru_w, w_out]
    if w_in is not None:
        in_specs.append(pl.BlockSpec((None, D_MODEL // steps, D_IN), lambda b, i: (l + 1, step(b, i), 0)))
        out_specs.append(pl.BlockSpec((D_MODEL // steps, D_IN), lambda b, i: (step(b, i), 0)))
        out_shape.append(jax.ShapeDtypeStruct((D_MODEL, D_IN), BF16))
        args.append(w_in)
    res = pl.pallas_call(
        functools.partial(_mixers_prompt_kernel, seq=seq, cast_next=w_in is not None),
        grid=(batch, N_HEADS),
        in_specs=in_specs,
        out_specs=out_specs,
        out_shape=out_shape,
        scratch_shapes=[pltpu.VMEM((seq, HEAD_DIM), F32)] * 6 + [pltpu.VMEM((N_GROUPS, seq, HEAD_DIM), F32)] * 2 + [
            pltpu.VMEM((seq + (CONV_W - 1) * SCAN_SEGS, LRU_BLOCK), F32),
            pltpu.VMEM((seq, LRU_BLOCK), F32), pltpu.VMEM((seq, LRU_BLOCK), F32)],
        compiler_params=_params(("arbitrary", "arbitrary"), BIG_CALL_VMEM_MIB),
        name="mixers_prompt",
    )(*args)
    return res if w_in is not None else (*res, None)


def _attn_sample_kernel(q_ref, kn_ref, vn_ref, g_ref, bias_ref, *rest, t_new):
    n_str = 2 * t_new
    tail_refs, strided = rest[0:2], rest[2:2 + n_str]
    u_ref, q_scr, k_scr, v_scr = rest[2 + n_str:]
    pad = SAMPLE_DENSE - SAMPLE_TAIL
    rows = pl.ds(pl.program_id(0) * t_new, t_new)
    q_scr[t_new:8, :] = jnp.zeros((8 - t_new, HEAD_DIM), F32)
    k_scr[t_new:pad, :] = jnp.zeros((pad - t_new, HEAD_DIM), F32)
    v_scr[t_new:pad, :] = jnp.zeros((pad - t_new, HEAD_DIM), F32)

    def head_rows(ref, hh):
        n = ref.shape[0]
        return ref.reshape(n * KV_HEADS, HEAD_DIM)[pl.ds(hh, n, stride=KV_HEADS), :].astype(BF16)

    for hh in range(KV_HEADS):
        lanes = slice(hh * HEAD_DIM, (hh + 1) * HEAD_DIM)
        q_scr[0:t_new, :] = q_ref[hh, rows, :]
        keys = []
        for which, (new_ref, scr) in enumerate(((kn_ref, k_scr), (vn_ref, v_scr))):
            scr[0:t_new, :] = new_ref[hh, rows, :]
            parts = [head_rows(tail_refs[which], hh), scr[...].astype(BF16)]
            parts += [head_rows(r, hh) for r in strided[which * t_new:(which + 1) * t_new]]
            keys.append(jnp.concatenate(parts, axis=0))
        kk, vv = keys
        s = lax.dot_general(q_scr[...].astype(BF16), kk, (((1,), (1,)), ((), ())),
                            preferred_element_type=F32) * ATT_SCALE
        ms, ls, ps = [], [], []
        for gi in range(N_GROUPS):
            sg = s + bias_ref[gi, hh]
            mx = jnp.max(sg, axis=-1, keepdims=True)
            p = jnp.exp(sg - mx)
            ms.append(mx)
            ls.append(jnp.sum(p, axis=-1, keepdims=True))
            ps.append(p)
        o_all = jnp.dot(jnp.concatenate(ps, axis=0).astype(BF16), vv, preferred_element_type=F32)
        os_ = [o_all[8 * gi:8 * (gi + 1), :] for gi in range(N_GROUPS)]
        top = functools.reduce(jnp.maximum, ms)
        ws = [jnp.exp(m - top) for m in ms]
        num = sum(w * o for w, o in zip(ws, os_))
        den = sum(w * d for w, d in zip(ws, ls))
        g = g_ref[hh, rows, :]
        u_ref[:, lanes] = (num / den)[0:t_new, :] * _silu(g)


def _attn_sample(a6, cache_k, cache_v, sbias, l, batch, t_new, c_len):
    d_last = DILATED_GROUPS[-1][1]
    assert c_len == WINDOW_MAX and c_len % SAMPLE_TAIL == 0 and c_len % d_last == 0 and t_new == SAMPLE_T
    assert all(w <= SAMPLE_TAIL for w, _ in DILATED_GROUPS[:-1])
    new_blk = (None, KV_HEADS, batch * t_new, HEAD_DIM)
    tail_spec = pl.BlockSpec((None, None, SAMPLE_TAIL, KV_HEADS, HEAD_DIM),
                             lambda b, hg: (l, b, c_len // SAMPLE_TAIL - 1, hg, 0))
    strided_specs = [pl.BlockSpec((None, None, c_len // d_last, None, KV_HEADS, HEAD_DIM),
                                  lambda b, hg, t=t: (l, b, 0, t, hg, 0)) for t in range(t_new)]
    strided_view = (cache_k.shape[0], batch, c_len // d_last, d_last, N_HEADS, HEAD_DIM)
    pad = SAMPLE_DENSE - SAMPLE_TAIL
    return pl.pallas_call(
        functools.partial(_attn_sample_kernel, t_new=t_new),
        grid=(batch, N_HEADS // KV_HEADS),
        in_specs=[
            pl.BlockSpec(new_blk, lambda b, hg: (0, hg, 0, 0)),
            pl.BlockSpec(new_blk, lambda b, hg: (1, hg, 0, 0)),
            pl.BlockSpec(new_blk, lambda b, hg: (2, hg, 0, 0)),
            pl.BlockSpec(new_blk, lambda b, hg: (3, hg, 0, 0)),
            pl.BlockSpec((N_GROUPS, KV_HEADS, 8, SAMPLE_KEYS), lambda b, hg: (0, hg, 0, 0)),
            tail_spec, tail_spec,
        ] + strided_specs + strided_specs,
        out_specs=pl.BlockSpec((None, t_new, KV_HEADS * HEAD_DIM), lambda b, hg: (b, 0, hg)),
        out_shape=jax.ShapeDtypeStruct((batch, t_new, D_ATT), F32),
        scratch_shapes=[pltpu.VMEM((8, HEAD_DIM), F32), pltpu.VMEM((pad, HEAD_DIM), F32),
                        pltpu.VMEM((pad, HEAD_DIM), F32)],
        compiler_params=_params(("arbitrary", "arbitrary"), 48),
        name="attn_sample",
    )(a6, a6, a6, a6, sbias, cache_k, cache_v,
      *([cache_k.reshape(strided_view)] * t_new), *([cache_v.reshape(strided_view)] * t_new))


def _lru_gates(xc, wa_ref, ba_ref, wx_ref, bx_ref, lam_ref):
    xcb = xc.astype(BF16)
    th_r = jnp.tanh(jnp.dot(xcb, (0.5 * wa_ref[...]).astype(BF16), preferred_element_type=F32) + 0.5 * ba_ref[...])
    th_i = jnp.tanh(jnp.dot(xcb, (0.5 * wx_ref[...]).astype(BF16), preferred_element_type=F32) + 0.5 * bx_ref[...])
    nl = -lam_ref[...]
    softplus = jnp.maximum(nl, 0.0) + jnp.log1p(jnp.exp(-jnp.abs(nl)))
    half_log_a = (-0.25 * LRU_C * softplus) * (1.0 + th_r)
    t = jnp.tanh(half_log_a)
    qn = t / (t - 1.0)
    a = 1.0 - 2.0 * qn
    w = qn * (1.0 - qn)
    root = jnp.where(w > 0.0, w * lax.rsqrt(w), 0.0)
    b = root * ((1.0 + th_i) * xc)
    return a, b


def _lru_prepare(x_ref, wc_ref, bc_ref, wa_ref, ba_ref, wx_ref, bx_ref, lam_ref, xpad, a_scr, b_scr, *, seq):
    seg_len = seq // SCAN_SEGS
    lead = (CONV_W - 1) * SCAN_SEGS
    row = lax.broadcasted_iota(jnp.int32, (SCAN_SEGS, LRU_BLOCK), 0)
    for i in range(SCAN_SEGS):
        xpad[pl.ds(lead + i, seg_len, stride=SCAN_SEGS), :] = x_ref[i * seg_len:(i + 1) * seg_len, :]
    for k in range(1, CONV_W):
        tail = xpad[lead + (seg_len - k) * SCAN_SEGS:lead + (seg_len - k + 1) * SCAN_SEGS, :]
        xpad[lead - k * SCAN_SEGS:lead - (k - 1) * SCAN_SEGS, :] = jnp.where(row == 0, 0.0, pltpu.roll(tail, 1, 0))
    xc = bc_ref[...]
    for j in range(CONV_W):
        xc = xc + xpad[j * SCAN_SEGS:j * SCAN_SEGS + seq, :] * wc_ref[j:j + 1, :]
    a, b = _lru_gates(xc, wa_ref, ba_ref, wx_ref, bx_ref, lam_ref)
    a_scr[...] = a
    b_scr[...] = b


def _lru_scan(x_ref, g_ref, u_ref, h_ref, cs_ref, a_scr, b_scr, *, seq):
    seg_len = seq // SCAN_SEGS
    row = lax.broadcasted_iota(jnp.int32, (SCAN_SEGS, LRU_BLOCK), 0)

    def body(c, carry):
        h_loc, a_cum = carry
        rows = pl.ds(pl.multiple_of(c * SCAN_SEGS, SCAN_SEGS), SCAN_SEGS)
        ca = a_scr[rows, :]
        h_loc = ca * h_loc + b_scr[rows, :]
        a_cum = ca * a_cum
        b_scr[rows, :] = h_loc
        a_scr[rows, :] = a_cum
        return h_loc, a_cum

    init = (jnp.zeros((SCAN_SEGS, LRU_BLOCK), F32), jnp.ones((SCAN_SEGS, LRU_BLOCK), F32))
    h_end, a_end = lax.fori_loop(0, seg_len, body, init, unroll=8)
    carry = jnp.zeros((1, LRU_BLOCK), F32)
    carries = jnp.zeros((SCAN_SEGS, LRU_BLOCK), F32)
    for i in range(1, SCAN_SEGS):
        carry = h_end[i - 1:i, :] + a_end[i - 1:i, :] * carry
        carries = jnp.where(row == i, carry, carries)
    shape3 = (seg_len, SCAN_SEGS, LRU_BLOCK)
    y = b_scr[...].reshape(shape3) + a_scr[...].reshape(shape3) * carries[None]
    b_scr[...] = y.reshape(seq, LRU_BLOCK)
    for i in range(SCAN_SEGS):
        rows = slice(i * seg_len, (i + 1) * seg_len)
        g = g_ref[rows, :]
        u_ref[rows, :] = (b_scr[pl.ds(i, seg_len, stride=SCAN_SEGS), :] * _silu(g)).astype(u_ref.dtype)
    h_ref[...] = b_scr[seq - 1:seq, :]
    cs_ref[...] = x_ref[seq - (CONV_W - 1):seq, :]


def _lru_weight_specs(l, n_of):
    return [
        pl.BlockSpec((None, CONV_W, LRU_BLOCK), lambda *g: (l, 0, n_of(*g))),
        pl.BlockSpec((None, 1, LRU_BLOCK), lambda *g: (l, 0, n_of(*g))),
        pl.BlockSpec((None, None, LRU_BLOCK, LRU_BLOCK), lambda *g: (l, n_of(*g), 0, 0)),
        pl.BlockSpec((None, 1, LRU_BLOCK), lambda *g: (l, 0, n_of(*g))),
        pl.BlockSpec((None, None, LRU_BLOCK, LRU_BLOCK), lambda *g: (l, n_of(*g), 0, 0)),
        pl.BlockSpec((None, 1, LRU_BLOCK), lambda *g: (l, 0, n_of(*g))),
        pl.BlockSpec((None, 1, LRU_BLOCK), lambda *g: (l, 0, n_of(*g))),
    ]


def _lru_sample_kernel(x_ref, g_ref, sc_ref, h0_ref, wc_ref, bc_ref, wa_ref, ba_ref, wx_ref, bx_ref, lam_ref,
                       u_ref, h_ref, cs_ref, xc_scr, a_scr, b_scr, y_scr, *, batch, t_new):
    n_state = CONV_W - 1

    def ext_row(b, i):
        if i < n_state:
            return sc_ref[b, i:i + 1, :]
        r = b * t_new + i - n_state
        return x_ref[r:r + 1, :]

    for b in range(batch):
        for t in range(t_new):
            acc = bc_ref[...] + ext_row(b, t) * wc_ref[0:1, :]
            for j in range(1, CONV_W):
                acc = acc + ext_row(b, t + j) * wc_ref[j:j + 1, :]
            xc_scr[b * t_new + t:b * t_new + t + 1, :] = acc
    a, bb = _lru_gates(xc_scr[...], wa_ref, ba_ref, wx_ref, bx_ref, lam_ref)
    a_scr[...] = a
    b_scr[...] = bb
    for b in range(batch):
        h = h0_ref[b:b + 1, :]
        for t in range(t_new):
            r = b * t_new + t
            h = a_scr[r:r + 1, :] * h + b_scr[r:r + 1, :]
            y_scr[r:r + 1, :] = h
        h_ref[b:b + 1, :] = h
        for i in range(n_state):
            cs_ref[b, i:i + 1, :] = ext_row(b, t_new + i)
    for b in range(batch):
        g = g_ref[b * t_new:(b + 1) * t_new, :]
        u_ref[b] = y_scr[b * t_new:(b + 1) * t_new, :] * _silu(g)


def _lru_sample(a6, state_conv, state_h, lru_w, l, batch, t_new):
    blk = (None, None, batch * t_new, LRU_BLOCK)
    n_of = lambda n: n
    rows = batch * t_new
    return pl.pallas_call(
        functools.partial(_lru_sample_kernel, batch=batch, t_new=t_new),
        grid=(N_LRU_BLOCKS,),
        in_specs=[
            pl.BlockSpec(blk, lambda n: (4, n, 0, 0)),
            pl.BlockSpec(blk, lambda n: (5, n, 0, 0)),
            pl.BlockSpec((None, batch, CONV_W - 1, LRU_BLOCK), lambda n: (l, 0, 0, n)),
            pl.BlockSpec((None, batch, LRU_BLOCK), lambda n: (l, 0, n)),
        ] + _lru_weight_specs(l, n_of),
        out_specs=[
            pl.BlockSpec((batch, t_new, LRU_BLOCK), lambda n: (0, 0, n)),
            pl.BlockSpec((batch, LRU_BLOCK), lambda n: (0, n)),
            pl.BlockSpec((batch, CONV_W - 1, LRU_BLOCK), lambda n: (0, 0, n)),
        ],
        out_shape=[
            jax.ShapeDtypeStruct((batch, t_new, D_LRU), F32),
            jax.ShapeDtypeStruct((batch, D_LRU), F32),
            jax.ShapeDtypeStruct((batch, CONV_W - 1, D_LRU), F32),
        ],
        scratch_shapes=[pltpu.VMEM((rows, LRU_BLOCK), F32)] * 4,
        compiler_params=_params(("arbitrary",), 32),
        name="lru_sample",
    )(a6, a6, state_conv, state_h, *lru_w)


def _outproj_kernel(ua_ref, ul_ref, w_ref, x_ref, gate_ref, gp_ref, *rest, n_tiles, has_next):
    if has_next:
        shift_ref, scale_ref, gn_ref, y_ref, h_ref, acc_a, acc_b = rest
    else:
        y_ref, acc_a, acc_b = rest
    i = pl.program_id(0)
    tm = x_ref.shape[0]
    chunk = min(tm, EPILOGUE_CHUNK)
    per_row = gate_ref.shape[0] == tm

    def matmul(acc):
        acc[...] = (jnp.dot(ua_ref[...].astype(BF16), w_ref[0:D_ATT, :], preferred_element_type=F32)
                    + jnp.dot(ul_ref[...].astype(BF16), w_ref[D_ATT:D_MODEL, :], preferred_element_type=F32))

    def epilogue(acc):
        for c in range(tm // chunk):
            rows = slice(c * chunk, (c + 1) * chunk)
            mrows = rows if per_row else slice(None)
            y = acc[rows, :]
            y = y * lax.rsqrt(jnp.mean(y * y, axis=-1, keepdims=True) + EPS)
            out = x_ref[rows, :] + gate_ref[mrows, :] * (y * gp_ref[...])
            y_ref[rows, :] = out
            if has_next:
                hn = out * lax.rsqrt(jnp.mean(out * out, axis=-1, keepdims=True) + EPS)
                hn = hn * gn_ref[...]
                h_ref[rows, :] = (hn * (1.0 + scale_ref[mrows, :]) + shift_ref[mrows, :]).astype(h_ref.dtype)

    accs = (acc_a, acc_b)

    @pl.when(i == 0)
    def _():
        matmul(accs[0])

    for parity in range(2):
        @pl.when((i > 0) & (i < n_tiles) & (i % 2 == parity))
        def _(parity=parity):
            epilogue(accs[1 - parity])
            matmul(accs[parity])

    @pl.when(i == n_tiles)
    def _():
        epilogue(accs[(n_tiles - 1) % 2])


def _outproj(u_att, u_lru, w_out, x2, gate, mod_idx, g_post, l, nxt, *, tm, rows_per_mod, vmem_mib):
    m_rows = x2.shape[0]
    n_tiles = m_rows // tm
    mm_tile = lambda i: jnp.minimum(i, n_tiles - 1)
    ep_tile = lambda i: jnp.maximum(i - 1, 0)
    row_spec = pl.BlockSpec((tm, D_MODEL), lambda i: (ep_tile(i), 0))
    in_specs = [
        pl.BlockSpec((tm, D_ATT), lambda i: (mm_tile(i), 0)),
        pl.BlockSpec((tm, D_LRU), lambda i: (mm_tile(i), 0)),
        pl.BlockSpec((D_MODEL, D_MODEL), lambda i: (0, 0), pipeline_mode=pl.Buffered(1)),
        row_spec,
        _mod_specs(mod_idx, 2, tm, rows_per_mod, ep_tile),
        pl.BlockSpec((None, 1, D_MODEL), lambda i: (l, 0, 0)),
    ]
    args = [u_att, u_lru, w_out, x2, gate, g_post]
    out_specs = [row_spec]
    out_shape = [jax.ShapeDtypeStruct((m_rows, D_MODEL), F32)]
    if nxt is not None:
        shift_n, scale_n, idx_n, g_pre, l_next = nxt
        in_specs += [
            _mod_specs(idx_n, 0, tm, rows_per_mod, ep_tile),
            _mod_specs(idx_n, 1, tm, rows_per_mod, ep_tile),
            pl.BlockSpec((None, 1, D_MODEL), lambda i: (l_next, 0, 0)),
        ]
        args += [shift_n, scale_n, g_pre]
        out_specs.append(row_spec)
        out_shape.append(jax.ShapeDtypeStruct((m_rows, D_MODEL), BF16))
    res = pl.pallas_call(
        functools.partial(_outproj_kernel, n_tiles=n_tiles, has_next=nxt is not None),
        grid=(n_tiles + 1,),
        in_specs=in_specs,
        out_specs=out_specs,
        out_shape=out_shape,
        scratch_shapes=[pltpu.VMEM((tm, D_MODEL), F32)] * 2,
        compiler_params=_params(("arbitrary",), vmem_mib),
        name="outproj",
    )(*args)
    return res if nxt is not None else (res[0], None)


def kernel(x_prompt, x_sample, cache_k, cache_v, state_h, state_conv, c_prompt, c_sample, rel_table, w_ada, b_ada,
           g_pre, w_in, w_conv, b_conv, w_a, b_a, w_x, b_x, lam, w_out, g_post):
    depth = w_in.shape[0]
    bp, seq, _ = x_prompt.shape
    bs, t_new, _ = x_sample.shape
    c_len = cache_k.shape[2]
    assert bp + bs <= MOD_ROWS and t_new <= 8
    assert seq % (L_BAND * DILATED_GROUPS[-1][1]) == 0 and c_len >= WINDOW_MAX

    c_all = jnp.concatenate([c_prompt, c_sample, jnp.zeros((MOD_ROWS - bp - bs, D_MODEL), F32)], axis=0)
    mod = _ada(c_all, w_ada, b_ada)
    mod_tab = mod.reshape(depth * MOD_ROWS * 3, 1, D_MODEL)
    pbias, sbias = _bias_tables(rel_table, c_len, t_new)

    w_in_l = w_in[0].astype(BF16)
    g_pre3 = g_pre.reshape(depth, 1, D_MODEL)
    g_post3 = g_post.reshape(depth, 1, D_MODEL)

    xp = x_prompt.reshape(bp * seq, D_MODEL)
    xs = x_sample.reshape(bs * t_new, D_MODEL)
    rows_s = bs * t_new
    lru_w = (w_conv, b_conv.reshape(depth, 1, D_LRU), w_a, b_a.reshape(depth, 1, D_LRU),
             w_x, b_x.reshape(depth, 1, D_LRU), lam.reshape(depth, 1, D_LRU))
    kv_p = kv_s = None
    hp_l, cp_l, hs_l, cs_l = [], [], [], []
    tm_norm, tm_in, tm_out = PROMPT_ROW_TILES
    p_idx = lambda l: (lambda b, which: (l * MOD_ROWS + b) * 3 + which)
    mod_s = [jnp.repeat(mod[l, bp:bp + bs], t_new, axis=0) for l in range(depth)]
    shift_s = [m[:, :D_MODEL] for m in mod_s]
    scale_s = [m[:, D_MODEL:2 * D_MODEL] for m in mod_s]
    gate_s = [m[:, 2 * D_MODEL:] for m in mod_s]
    hp = _prenorm(xp, mod_tab, mod_tab, p_idx(0), g_pre3, 0, tm=tm_norm, rows_per_mod=seq // tm_norm)
    hs = _prenorm(xs, shift_s[0], scale_s[0], None, g_pre3, 0, tm=rows_s, rows_per_mod=1)
    for l in range(depth):
        more = l + 1 < depth
        a6, kp, vp = _inproj(hp, w_in_l, l, depth, kv_p, tm=tm_in, vmem_mib=BIG_CALL_VMEM_MIB)
        kv_p = (kp, vp)
        a6 = a6.reshape(N_SEG, N_HEADS, bp, seq, HEAD_DIM)
        u_att, u_lru, h_last, conv, w_out_l, w_in_next = _mixers_prompt(a6, pbias, lru_w, w_out, l, bp, seq,
                                                                        w_in if more else None)
        nxt = (mod_tab, mod_tab, p_idx(l + 1), g_pre3, l + 1) if more else None
        xp, hp = _outproj(u_att.reshape(bp * seq, D_ATT), u_lru.reshape(bp * seq, D_LRU), w_out_l, xp, mod_tab,
                          p_idx(l), g_post3, l, nxt, tm=tm_out, rows_per_mod=seq // tm_out,
                          vmem_mib=BIG_CALL_VMEM_MIB)
        hp_l.append(h_last.reshape(bp, D_LRU))
        cp_l.append(conv)
        a6, ks, vs = _inproj(hs, w_in_l, l, depth, kv_s, tm=rows_s, vmem_mib=SMALL_CALL_VMEM_MIB)
        kv_s = (ks, vs)
        u_att = _attn_sample(a6, cache_k, cache_v, sbias, l, bs, t_new, c_len)
        u_lru, h_last, conv = _lru_sample(a6, state_conv, state_h, lru_w, l, bs, t_new)
        nxt = (shift_s[l + 1], scale_s[l + 1], None, g_pre3, l + 1) if more else None
        xs, hs = _outproj(u_att.reshape(rows_s, D_ATT), u_lru.reshape(rows_s, D_LRU), w_out_l, xs, gate_s[l], None,
                          g_post3, l, nxt, tm=rows_s, rows_per_mod=1, vmem_mib=SMALL_CALL_VMEM_MIB)
        hs_l.append(h_last)
        cs_l.append(conv)
        w_in_l = w_in_next

    kp, vp = kv_p
    ks, vs = kv_s
    return (xp.reshape(bp, seq, D_MODEL), xs.reshape(bs, t_new, D_MODEL),
            kp.reshape(depth, bp, seq, N_HEADS, HEAD_DIM), vp.reshape(depth, bp, seq, N_HEADS, HEAD_DIM),
            jnp.stack(hp_l), jnp.stack(cp_l),
            ks.reshape(depth, bs, t_new, N_HEADS, HEAD_DIM), vs.reshape(depth, bs, t_new, N_HEADS, HEAD_DIM),
            jnp.stack(hs_l), jnp.stack(cs_l))
```

```python
import functools
import math

import numpy as np
import jax
import jax.numpy as jnp
from jax import lax
from jax.experimental import pallas as pl
from jax.experimental.pallas import tpu as pltpu

D_MODEL = 4096
D_ATT = 2048
D_LRU = 2048
HEAD_DIM = 128
N_HEADS = 16
N_LRU_BLOCKS = 16
LRU_BLOCK = 128
CONV_W = 4
LRU_C = 8.0
DILATED_GROUPS = ((128, 1), (512, 4), (2048, 16))
N_GROUPS = len(DILATED_GROUPS)
WINDOW_MAX = 2048
N_BUCKETS = 32
MAX_EXACT = N_BUCKETS // 2
MAX_DISTANCE = WINDOW_MAX
EPS = 1e-6
ATT_SCALE = HEAD_DIM ** -0.5
D_IN = 4 * D_ATT + 2 * D_LRU
N_SEG = D_IN // D_ATT
SEG_K, SEG_V = 1, 2
L_BAND = 128
MASKED = -1e30
MOD_ROWS = 16
SAMPLE_TAIL = 512
SAMPLE_DENSE = SAMPLE_TAIL + 128
SAMPLE_T = 4
SAMPLE_KEYS = SAMPLE_DENSE + SAMPLE_T * (WINDOW_MAX // DILATED_GROUPS[-1][1])
MIB = 1024 * 1024
PROMPT_ROW_TILES = (512, 1024, 128)
BIG_CALL_VMEM_MIB = 56
SMALL_CALL_VMEM_MIB = 48
NORM_CHUNK = 64
EPILOGUE_CHUNK = 128
ATTN_GROUP = 8
SCAN_SEGS = 8
KV_HEADS = 8

F32 = jnp.float32
BF16 = jnp.bfloat16


def _params(semantics, vmem_mib):
    return pltpu.CompilerParams(dimension_semantics=semantics, vmem_limit_bytes=vmem_mib * MIB)


def _silu(x):
    half = 0.5 * x
    return half + half * jnp.tanh(half)


def _rel_bucket_np(dist):
    d = dist.astype(np.float32)
    large = np.float32(MAX_EXACT) + np.log(np.maximum(d, np.float32(1.0)) / np.float32(MAX_EXACT)) / np.float32(
        math.log(MAX_DISTANCE / MAX_EXACT)) * np.float32(N_BUCKETS - MAX_EXACT)
    large = np.minimum(large.astype(np.int32), N_BUCKETS - 1)
    return np.where(dist < MAX_EXACT, dist, large).astype(np.int32)


def _prompt_bucket_index():
    qi = np.arange(L_BAND)[:, None]
    kj = np.arange(2 * L_BAND)[None, :]
    dist = qi + L_BAND - kj
    band = (dist >= 0) & (dist <= L_BAND)
    out = []
    for _, dil in DILATED_GROUPS:
        b = _rel_bucket_np(np.clip(dist, 0, L_BAND) * dil)
        out.append(np.where(band, b, -1))
    return np.stack(out).astype(np.int32)


def _sample_bucket_index(c_len, t_new):
    d_last = DILATED_GROUPS[-1][1]
    grp = c_len // d_last
    t = np.arange(8)[:, None]
    col = np.arange(SAMPLE_KEYS)[None, :]
    dense = col < SAMPLE_DENSE
    owner = np.where(dense, -1, (col - SAMPLE_DENSE) // grp)
    idx = np.where(dense, c_len - SAMPLE_TAIL + col, ((col - SAMPLE_DENSE) % grp) * d_last + owner)
    real = np.where(dense, col < SAMPLE_TAIL + t_new, owner < t_new)
    delta = c_len + t - idx
    out = []
    for gi, (window, dil) in enumerate(DILATED_GROUPS):
        valid = real & (delta >= 0) & (delta % dil == 0) & (delta <= window) & (t < t_new)
        if gi == N_GROUPS - 1:
            valid &= np.where(dense, delta < dil, owner == t)
        else:
            valid &= dense
        b = _rel_bucket_np(np.clip(delta, 0, window))
        b = np.where(valid, b, -1)
        b = np.where(t >= t_new, 0, b)
        out.append(b)
    return np.stack(out).astype(np.int32)


def _ada_kernel(c_ref, w_ref, b_ref, o_ref):
    c = c_ref[...]
    a = _silu(c).astype(BF16)
    o_ref[...] = jnp.dot(a, w_ref[...].astype(BF16), preferred_element_type=F32) + b_ref[...]


def _ada(c_all, w_ada, b_ada):
    depth = w_ada.shape[0]
    tn = 512
    return pl.pallas_call(
        _ada_kernel,
        grid=(depth, 3 * D_MODEL // tn),
        in_specs=[
            pl.BlockSpec((MOD_ROWS, D_MODEL), lambda l, j: (0, 0)),
            pl.BlockSpec((None, D_MODEL, tn), lambda l, j: (l, 0, j)),
            pl.BlockSpec((None, 1, tn), lambda l, j: (l, 0, j)),
        ],
        out_specs=pl.BlockSpec((None, MOD_ROWS, tn), lambda l, j: (l, 0, j)),
        out_shape=jax.ShapeDtypeStruct((depth, MOD_ROWS, 3 * D_MODEL), F32),
        compiler_params=_params(("arbitrary", "arbitrary"), 40),
        name="ada",
    )(c_all, w_ada, b_ada.reshape(depth, 1, 3 * D_MODEL))


def _bias_kernel(tab_ref, pidx_ref, sidx_ref, pb_ref, sb_ref):
    h = pl.program_id(0)
    for idx_ref, out_ref in ((pidx_ref, pb_ref), (sidx_ref, sb_ref)):
        for g in range(N_GROUPS):
            idx = idx_ref[g]
            acc = jnp.full(idx.shape, MASKED, F32)
            for b in range(N_BUCKETS):
                acc = jnp.where(idx == b, tab_ref[b, h], acc)
            out_ref[g] = acc


def _bias_tables(rel_table, c_len, t_new):
    pidx = jnp.asarray(_prompt_bucket_index())
    sidx = jnp.asarray(_sample_bucket_index(c_len, t_new))
    return pl.pallas_call(
        _bias_kernel,
        grid=(N_HEADS,),
        in_specs=[
            pl.BlockSpec(memory_space=pltpu.SMEM),
            pl.BlockSpec((N_GROUPS, L_BAND, 2 * L_BAND), lambda h: (0, 0, 0)),
            pl.BlockSpec((N_GROUPS, 8, SAMPLE_KEYS), lambda h: (0, 0, 0)),
        ],
        out_specs=[
            pl.BlockSpec((N_GROUPS, None, L_BAND, 2 * L_BAND), lambda h: (0, h, 0, 0)),
            pl.BlockSpec((N_GROUPS, None, 8, SAMPLE_KEYS), lambda h: (0, h, 0, 0)),
        ],
        out_shape=[
            jax.ShapeDtypeStruct((N_GROUPS, N_HEADS, L_BAND, 2 * L_BAND), F32),
            jax.ShapeDtypeStruct((N_GROUPS, N_HEADS, 8, SAMPLE_KEYS), F32),
        ],
        compiler_params=_params(("arbitrary",), 32),
        name="bias_tables",
    )(rel_table, pidx, sidx)


def _mod_specs(mod_idx, which, tm, rows_per_mod, tile_of=lambda i: i):
    if mod_idx is None:
        return pl.BlockSpec((tm, D_MODEL), lambda i, *_: (tile_of(i), 0))
    return pl.BlockSpec((None, 1, D_MODEL), lambda i, *_: (mod_idx(tile_of(i) // rows_per_mod, which), 0, 0))


def _prenorm_kernel(x_ref, shift_ref, scale_ref, g_ref, h_ref):
    tm = x_ref.shape[0]
    chunk = min(tm, NORM_CHUNK)
    per_row = shift_ref.shape[0] == tm

    def norm(c, _):
        rows = pl.ds(pl.multiple_of(c * chunk, chunk), chunk)
        mrows = rows if per_row else slice(None)
        x = x_ref[rows, :]
        y = x * lax.rsqrt(jnp.mean(x * x, axis=-1, keepdims=True) + EPS)
        y = y * g_ref[...]
        h_ref[rows, :] = (y * (1.0 + scale_ref[mrows, :]) + shift_ref[mrows, :]).astype(h_ref.dtype)
        return 0

    lax.fori_loop(0, tm // chunk, norm, 0)


def _prenorm(x2, shift, scale, mod_idx, g_pre, l, *, tm, rows_per_mod):
    m_rows = x2.shape[0]
    return pl.pallas_call(
        _prenorm_kernel,
        grid=(m_rows // tm,),
        in_specs=[
            pl.BlockSpec((tm, D_MODEL), lambda i: (i, 0)),
            _mod_specs(mod_idx, 0, tm, rows_per_mod),
            _mod_specs(mod_idx, 1, tm, rows_per_mod),
            pl.BlockSpec((None, 1, D_MODEL), lambda i: (l, 0, 0)),
        ],
        out_specs=pl.BlockSpec((tm, D_MODEL), lambda i: (i, 0)),
        out_shape=jax.ShapeDtypeStruct((m_rows, D_MODEL), BF16),
        compiler_params=_params(("arbitrary",), 40),
        name="prenorm",
    )(x2, shift, scale, g_pre)


def _inproj_kernel(h_ref, w_ref, *rest, nj_seg, layer):
    a_ref, k_hbm, v_hbm, stage, sem = rest[-5:]
    i, j = pl.program_id(0), pl.program_id(1)
    tm = h_ref.shape[0]
    seg, part = j // nj_seg, j % nj_seg
    first_kv, last_kv = SEG_K * nj_seg, (SEG_V + 1) * nj_seg - 1

    def head_copy(dst_hbm, group):
        rows = pl.ds(i * tm, tm)
        heads = pl.ds(group * KV_HEADS, KV_HEADS)
        return pltpu.make_async_copy(stage, dst_hbm.at[layer, rows, heads, :], sem.at[0])

    acc = jnp.dot(h_ref[...], w_ref[...], preferred_element_type=F32)
    for hh in range(KV_HEADS):
        a_ref[hh] = acc[:, hh * HEAD_DIM:(hh + 1) * HEAD_DIM]

    @pl.when((j > first_kv) & (j <= last_kv + 1))
    def _():
        head_copy(k_hbm, 0).wait()

    def stage_and_send(dst_hbm):
        flat = stage.reshape(tm * KV_HEADS, HEAD_DIM)
        for hh in range(KV_HEADS):
            flat[pl.ds(hh, tm, stride=KV_HEADS), :] = acc[:, hh * HEAD_DIM:(hh + 1) * HEAD_DIM]
        head_copy(dst_hbm, part).start()

    @pl.when(seg == SEG_K)
    def _():
        stage_and_send(k_hbm)

    @pl.when(seg == SEG_V)
    def _():
        stage_and_send(v_hbm)


def _inproj(h2, w_in, l, depth, kv_prev, *, tm, vmem_mib):
    m_rows = h2.shape[0]
    tn = KV_HEADS * HEAD_DIM
    nj_seg = D_ATT // tn
    in_specs = [
        pl.BlockSpec((tm, D_MODEL), lambda i, j: (i, 0)),
        pl.BlockSpec((D_MODEL, tn), lambda i, j: (0, j)),
    ]
    args = [h2, w_in]
    aliases = {}
    if kv_prev is not None:
        in_specs += [pl.BlockSpec(memory_space=pl.ANY)] * 2
        args += list(kv_prev)
        aliases = {2: 1, 3: 2}
    kv_shape = jax.ShapeDtypeStruct((depth, m_rows, N_HEADS, HEAD_DIM), F32)
    return pl.pallas_call(
        functools.partial(_inproj_kernel, nj_seg=nj_seg, layer=l),
        grid=(m_rows // tm, N_SEG * nj_seg),
        in_specs=in_specs,
        out_specs=[
            pl.BlockSpec((None, KV_HEADS, tm, HEAD_DIM), lambda i, j: (j // nj_seg, j % nj_seg, i, 0)),
            pl.BlockSpec(memory_space=pl.ANY),
            pl.BlockSpec(memory_space=pl.ANY),
        ],
        out_shape=[jax.ShapeDtypeStruct((N_SEG, N_HEADS, m_rows, HEAD_DIM), F32), kv_shape, kv_shape],
        scratch_shapes=[pltpu.VMEM((tm, KV_HEADS, HEAD_DIM), F32), pltpu.SemaphoreType.DMA((1,))],
        input_output_aliases=aliases,
        compiler_params=_params(("arbitrary", "arbitrary"), vmem_mib),
        name="inproj",
    )(*args)


def _attn_branches(q_ref, k_ref, v_ref, bias_ref, qa, ka, va, qb, kb, vb, o_scr, e_scr, *, seq):
    n_blk = seq // L_BAND

    def regroup(dst, src, stride):
        sub = seq // stride
        for r in range(stride):
            dst[r * sub:(r + 1) * sub, :] = src[pl.ds(r, sub, stride=stride), :]

    def branch(gi, dil, q_src, k_src, v_src):
        per_res = n_blk // dil
        grp = ATTN_GROUP
        assert n_blk % grp == 0 and (grp % per_res == 0 or per_res % grp == 0)

        def blocks(src, lo, hi):
            return src[lo * L_BAND:hi * L_BAND, :].astype(BF16).reshape(hi - lo, L_BAND, HEAD_DIM)

        for g0 in range(0, n_blk, grp):
            q3 = blocks(q_src, g0, g0 + grp)
            k3 = blocks(k_src, g0, g0 + grp)
            v3 = blocks(v_src, g0, g0 + grp)
            if per_res > 1:
                if g0 == 0:
                    kp = jnp.concatenate([k3[:1], k3[:-1]], axis=0)
                    vp = jnp.concatenate([v3[:1], v3[:-1]], axis=0)
                else:
                    kp = blocks(k_src, g0 - 1, g0 + grp - 1)
                    vp = blocks(v_src, g0 - 1, g0 + grp - 1)
                kk = jnp.concatenate([kp, k3], axis=1)
                vv = jnp.concatenate([vp, v3], axis=1)
                blk = lax.broadcasted_iota(jnp.int32, (grp, 1, 2 * L_BAND), 0) + g0
                col = lax.broadcasted_iota(jnp.int32, (grp, 1, 2 * L_BAND), 2)
                first = jnp.where((blk % per_res == 0) & (col < L_BAND), MASKED, 0.0)
                bias = bias_ref[gi][None] + first
            else:
                kk, vv = k3, v3
                bias = bias_ref[gi, :, L_BAND:][None]
            s = jnp.einsum('bqd,bkd->bqk', q3, kk, preferred_element_type=F32) * ATT_SCALE + bias
            mx = jnp.max(s, axis=-1, keepdims=True)
            p = jnp.exp(s - mx)
            den = jnp.sum(p, axis=-1, keepdims=True)
            o = jnp.einsum('bqk,bkd->bqd', p.astype(BF16), vv, preferred_element_type=F32) / den
            lse = jnp.broadcast_to(mx + jnp.log(den), (grp, L_BAND, HEAD_DIM))
            o = o.reshape(grp * L_BAND, HEAD_DIM)
            lse = lse.reshape(grp * L_BAND, HEAD_DIM)
            if dil == 1:
                o_scr[gi, g0 * L_BAND:(g0 + grp) * L_BAND, :] = o
                e_scr[gi, g0 * L_BAND:(g0 + grp) * L_BAND, :] = lse
            else:
                sub = seq // dil
                for r in range(g0 // per_res, (g0 + grp) // per_res):
                    part = slice((r * per_res - g0) * L_BAND, ((r + 1) * per_res - g0) * L_BAND)
                    o_scr[gi, pl.ds(r, sub, stride=dil), :] = o[part, :]
                    e_scr[gi, pl.ds(r, sub, stride=dil), :] = lse[part, :]

    (_, d0), (_, d1), (_, d2) = DILATED_GROUPS
    assert d0 == 1 and d2 == d1 * d1
    branch(0, d0, q_ref, k_ref, v_ref)
    for dst, src in ((qa, q_ref), (ka, k_ref), (va, v_ref)):
        regroup(dst, src, d1)
    branch(1, d1, qa, ka, va)
    for dst, src in ((qb, qa), (kb, ka), (vb, va)):
        regroup(dst, src, d1)
    branch(2, d2, qb, kb, vb)


def _attn_combine(g_ref, u_ref, o_scr, e_scr, *, seq):
    chunk = 256

    def combine(c, _):
        rows = pl.ds(pl.multiple_of(c * chunk, chunk), chunk)
        es = [e_scr[gi, rows, :] for gi in range(N_GROUPS)]
        top = functools.reduce(jnp.maximum, es)
        ws = [jnp.exp(e - top) for e in es]
        num = sum(w * o_scr[gi, rows, :] for gi, w in enumerate(ws))
        den = sum(ws)
        g = g_ref[rows, :]
        u_ref[rows, :] = ((num / den) * _silu(g)).astype(u_ref.dtype)
        return 0

    lax.fori_loop(0, seq // chunk, combine, 0)


def _mixers_prompt_kernel(q_ref, k_ref, v_ref, ga_ref, bias_ref, x_ref, gl_ref, wc_ref, bc_ref, wa_ref, ba_ref,
                          wx_ref, bx_ref, lam_ref, wsrc_ref, *rest, seq, cast_next):
    if cast_next:
        wsrc2_ref, ua_ref, ul_ref, h_ref, cs_ref, wdst_ref, wdst2_ref = rest[:7]
        wdst2_ref[...] = wsrc2_ref[...].astype(wdst2_ref.dtype)
        scratch = rest[7:]
    else:
        ua_ref, ul_ref, h_ref, cs_ref, wdst_ref = rest[:5]
        scratch = rest[5:]
    qa, ka, va, qb, kb, vb, o_scr, e_scr, xpad, a_scr, b_scr = scratch
    wdst_ref[...] = wsrc_ref[...].astype(wdst_ref.dtype)
    _lru_prepare(x_ref, wc_ref, bc_ref, wa_ref, ba_ref, wx_ref, bx_ref, lam_ref, xpad, a_scr, b_scr, seq=seq)
    _attn_branches(q_ref, k_ref, v_ref, bias_ref, qa, ka, va, qb, kb, vb, o_scr, e_scr, seq=seq)
    _lru_scan(x_ref, gl_ref, ul_ref, h_ref, cs_ref, a_scr, b_scr, seq=seq)
    _attn_combine(ga_ref, ua_ref, o_scr, e_scr, seq=seq)


def _mixers_prompt(a6, pbias, lru_w, w_out, l, batch, seq, w_in=None):
    assert N_HEADS == N_LRU_BLOCKS
    blk = (None, None, None, seq, HEAD_DIM)
    steps = batch * N_HEADS
    step = lambda b, i: b * N_HEADS + i
    slot_spec = lambda slot: pl.BlockSpec(blk, lambda b, i: (slot, i, b, 0, 0))
    in_specs = [slot_spec(0), slot_spec(1), slot_spec(2), slot_spec(3),
                pl.BlockSpec((N_GROUPS, None, L_BAND, 2 * L_BAND), lambda b, i: (0, i, 0, 0)),
                slot_spec(4), slot_spec(5)] + _lru_weight_specs(l, lambda b, i: i) + [
        pl.BlockSpec((None, D_MODEL // steps, D_MODEL), lambda b, i: (l, step(b, i), 0)),
    ]
    col_spec = pl.BlockSpec((None, seq, HEAD_DIM), lambda b, i: (b, 0, i))
    out_specs = [
        col_spec, col_spec,
        pl.BlockSpec((None, 1, LRU_BLOCK), lambda b, i: (b, 0, i)),
        pl.BlockSpec((None, CONV_W - 1, LRU_BLOCK), lambda b, i: (b, 0, i)),
        pl.BlockSpec((D_MODEL // steps, D_MODEL), lambda b, i: (step(b, i), 0)),
    ]
    out_shape = [
        jax.ShapeDtypeStruct((batch, seq, D_ATT), BF16),
        jax.ShapeDtypeStruct((batch, seq, D_LRU), BF16),
        jax.ShapeDtypeStruct((batch, 1, D_LRU), F32),
        jax.ShapeDtypeStruct((batch, CONV_W - 1, D_LRU), F32),
        jax.ShapeDtypeStruct((D_MODEL, D_MODEL), BF16),
    ]
    args = [a6, a6, a6, a6, pbias, a6, a6, *lru_w, w_out]
    if w_in is not None:
        in_specs.append(pl.BlockSpec((None, D_MODEL // steps, D_IN), lambda b, i: (l + 1, step(b, i), 0)))
        out_specs.append(pl.BlockSpec((D_MODEL // steps, D_IN), lambda b, i: (step(b, i), 0)))
        out_shape.append(jax.ShapeDtypeStruct((D_MODEL, D_IN), BF16))
        args.append(w_in)
    res = pl.pallas_call(
        functools.partial(_mixers_prompt_kernel, seq=seq, cast_next=w_in is not None),
        grid=(batch, N_HEADS),
        in_specs=in_specs,
        out_specs=out_specs,
        out_shape=out_shape,
        scratch_shapes=[pltpu.VMEM((seq, HEAD_DIM), F32)] * 6 + [pltpu.VMEM((N_GROUPS, seq, HEAD_DIM), F32)] * 2 + [
            pltpu.VMEM((seq + (CONV_W - 1) * SCAN_SEGS, LRU_BLOCK), F32),
            pltpu.VMEM((seq, LRU_BLOCK), F32), pltpu.VMEM((seq, LRU_BLOCK), F32)],
        compiler_params=_params(("arbitrary", "arbitrary"), BIG_CALL_VMEM_MIB),
        name="mixers_prompt",
    )(*args)
    return res if w_in is not None else (*res, None)


def _attn_sample_kernel(q_ref, kn_ref, vn_ref, g_ref, bias_ref, *rest, t_new):
    n_str = 2 * t_new
    tail_refs, strided = rest[0:2], rest[2:2 + n_str]
    u_ref, q_scr, k_scr, v_scr = rest[2 + n_str:]
    pad = SAMPLE_DENSE - SAMPLE_TAIL
    rows = pl.ds(pl.program_id(0) * t_new, t_new)
    q_scr[t_new:8, :] = jnp.zeros((8 - t_new, HEAD_DIM), F32)
    k_scr[t_new:pad, :] = jnp.zeros((pad - t_new, HEAD_DIM), F32)
    v_scr[t_new:pad, :] = jnp.zeros((pad - t_new, HEAD_DIM), F32)

    def head_rows(ref, hh):
        n = ref.shape[0]
        return ref.reshape(n * KV_HEADS, HEAD_DIM)[pl.ds(hh, n, stride=KV_HEADS), :].astype(BF16)

    for hh in range(KV_HEADS):
        lanes = slice(hh * HEAD_DIM, (hh + 1) * HEAD_DIM)
        q_scr[0:t_new, :] = q_ref[hh, rows, :]
        keys = []
        for which, (new_ref, scr) in enumerate(((kn_ref, k_scr), (vn_ref, v_scr))):
            scr[0:t_new, :] = new_ref[hh, rows, :]
            parts = [head_rows(tail_refs[which], hh), scr[...].astype(BF16)]
            parts += [head_rows(r, hh) for r in strided[which * t_new:(which + 1) * t_new]]
            keys.append(jnp.concatenate(parts, axis=0))
        kk, vv = keys
        s = lax.dot_general(q_scr[...].astype(BF16), kk, (((1,), (1,)), ((), ())),
                            preferred_element_type=F32) * ATT_SCALE
        ms, ls, ps = [], [], []
        for gi in range(N_GROUPS):
            sg = s + bias_ref[gi, hh]
            mx = jnp.max(sg, axis=-1, keepdims=True)
            p = jnp.exp(sg - mx)
            ms.append(mx)
            ls.append(jnp.sum(p, axis=-1, keepdims=True))
            ps.append(p)
        o_all = jnp.dot(jnp.concatenate(ps, axis=0).astype(BF16), vv, preferred_element_type=F32)
        os_ = [o_all[8 * gi:8 * (gi + 1), :] for gi in range(N_GROUPS)]
        top = functools.reduce(jnp.maximum, ms)
        ws = [jnp.exp(m - top) for m in ms]
        num = sum(w * o for w, o in zip(ws, os_))
        den = sum(w * d for w, d in zip(ws, ls))
        g = g_ref[hh, rows, :]
        u_ref[:, lanes] = (num / den)[0:t_new, :] * _silu(g)


def _attn_sample(a6, cache_k, cache_v, sbias, l, batch, t_new, c_len):
    d_last = DILATED_GROUPS[-1][1]
    assert c_len == WINDOW_MAX and c_len % SAMPLE_TAIL == 0 and c_len % d_last == 0 and t_new == SAMPLE_T
    assert all(w <= SAMPLE_TAIL for w, _ in DILATED_GROUPS[:-1])
    new_blk = (None, KV_HEADS, batch * t_new, HEAD_DIM)
    tail_spec = pl.BlockSpec((None, None, SAMPLE_TAIL, KV_HEADS, HEAD_DIM),
                             lambda b, hg: (l, b, c_len // SAMPLE_TAIL - 1, hg, 0))
    strided_specs = [pl.BlockSpec((None, None, c_len // d_last, None, KV_HEADS, HEAD_DIM),
                                  lambda b, hg, t=t: (l, b, 0, t, hg, 0)) for t in range(t_new)]
    strided_view = (cache_k.shape[0], batch, c_len // d_last, d_last, N_HEADS, HEAD_DIM)
    pad = SAMPLE_DENSE - SAMPLE_TAIL
    return pl.pallas_call(
        functools.partial(_attn_sample_kernel, t_new=t_new),
        grid=(batch, N_HEADS // KV_HEADS),
        in_specs=[
            pl.BlockSpec(new_blk, lambda b, hg: (0, hg, 0, 0)),
            pl.BlockSpec(new_blk, lambda b, hg: (1, hg, 0, 0)),
            pl.BlockSpec(new_blk, lambda b, hg: (2, hg, 0, 0)),
            pl.BlockSpec(new_blk, lambda b, hg: (3, hg, 0, 0)),
            pl.BlockSpec((N_GROUPS, KV_HEADS, 8, SAMPLE_KEYS), lambda b, hg: (0, hg, 0, 0)),
            tail_spec, tail_spec,
        ] + strided_specs + strided_specs,
        out_specs=pl.BlockSpec((None, t_new, KV_HEADS * HEAD_DIM), lambda b, hg: (b, 0, hg)),
        out_shape=jax.ShapeDtypeStruct((batch, t_new, D_ATT), F32),
        scratch_shapes=[pltpu.VMEM((8, HEAD_DIM), F32), pltpu.VMEM((pad, HEAD_DIM), F32),
                        pltpu.VMEM((pad, HEAD_DIM), F32)],
        compiler_params=_params(("arbitrary", "arbitrary"), 48),
        name="attn_sample",
    )(a6, a6, a6, a6, sbias, cache_k, cache_v,
      *([cache_k.reshape(strided_view)] * t_new), *([cache_v.reshape(strided_view)] * t_new))


def _lru_gates(xc, wa_ref, ba_ref, wx_ref, bx_ref, lam_ref):
    xcb = xc.astype(BF16)
    th_r = jnp.tanh(jnp.dot(xcb, (0.5 * wa_ref[...]).astype(BF16), preferred_element_type=F32) + 0.5 * ba_ref[...])
    th_i = jnp.tanh(jnp.dot(xcb, (0.5 * wx_ref[...]).astype(BF16), preferred_element_type=F32) + 0.5 * bx_ref[...])
    nl = -lam_ref[...]
    softplus = jnp.maximum(nl, 0.0) + jnp.log1p(jnp.exp(-jnp.abs(nl)))
    half_log_a = (-0.25 * LRU_C * softplus) * (1.0 + th_r)
    t = jnp.tanh(half_log_a)
    qn = t / (t - 1.0)
    a = 1.0 - 2.0 * qn
    w = qn * (1.0 - qn)
    root = jnp.where(w > 0.0, w * lax.rsqrt(w), 0.0)
    b = root * ((1.0 + th_i) * xc)
    return a, b


def _lru_prepare(x_ref, wc_ref, bc_ref, wa_ref, ba_ref, wx_ref, bx_ref, lam_ref, xpad, a_scr, b_scr, *, seq):
    seg_len = seq // SCAN_SEGS
    lead = (CONV_W - 1) * SCAN_SEGS
    row = lax.broadcasted_iota(jnp.int32, (SCAN_SEGS, LRU_BLOCK), 0)
    for i in range(SCAN_SEGS):
        xpad[pl.ds(lead + i, seg_len, stride=SCAN_SEGS), :] = x_ref[i * seg_len:(i + 1) * seg_len, :]
    for k in range(1, CONV_W):
        tail = xpad[lead + (seg_len - k) * SCAN_SEGS:lead + (seg_len - k + 1) * SCAN_SEGS, :]
        xpad[lead - k * SCAN_SEGS:lead - (k - 1) * SCAN_SEGS, :] = jnp.where(row == 0, 0.0, pltpu.roll(tail, 1, 0))
    xc = bc_ref[...]
    for j in range(CONV_W):
        xc = xc + xpad[j * SCAN_SEGS:j * SCAN_SEGS + seq, :] * wc_ref[j:j + 1, :]
    a, b = _lru_gates(xc, wa_ref, ba_ref, wx_ref, bx_ref, lam_ref)
    a_scr[...] = a
    b_scr[...] = b


def _lru_scan(x_ref, g_ref, u_ref, h_ref, cs_ref, a_scr, b_scr, *, seq):
    seg_len = seq // SCAN_SEGS
    row = lax.broadcasted_iota(jnp.int32, (SCAN_SEGS, LRU_BLOCK), 0)

    def body(c, carry):
        h_loc, a_cum = carry
        rows = pl.ds(pl.multiple_of(c * SCAN_SEGS, SCAN_SEGS), SCAN_SEGS)
        ca = a_scr[rows, :]
        h_loc = ca * h_loc + b_scr[rows, :]
        a_cum = ca * a_cum
        b_scr[rows, :] = h_loc
        a_scr[rows, :] = a_cum
        return h_loc, a_cum

    init = (jnp.zeros((SCAN_SEGS, LRU_BLOCK), F32), jnp.ones((SCAN_SEGS, LRU_BLOCK), F32))
    h_end, a_end = lax.fori_loop(0, seg_len, body, init, unroll=8)
    carry = jnp.zeros((1, LRU_BLOCK), F32)
    carries = jnp.zeros((SCAN_SEGS, LRU_BLOCK), F32)
    for i in range(1, SCAN_SEGS):
        carry = h_end[i - 1:i, :] + a_end[i - 1:i, :] * carry
        carries = jnp.where(row == i, carry, carries)
    shape3 = (seg_len, SCAN_SEGS, LRU_BLOCK)
    y = b_scr[...].reshape(shape3) + a_scr[...].reshape(shape3) * carries[None]
    b_scr[...] = y.reshape(seq, LRU_BLOCK)
    for i in range(SCAN_SEGS):
        rows = slice(i * seg_len, (i + 1) * seg_len)
        g = g_ref[rows, :]
        u_ref[rows, :] = (b_scr[pl.ds(i, seg_len, stride=SCAN_SEGS), :] * _silu(g)).astype(u_ref.dtype)
    h_ref[...] = b_scr[seq - 1:seq, :]
    cs_ref[...] = x_ref[seq - (CONV_W - 1):seq, :]


def _lru_weight_specs(l, n_of):
    return [
        pl.BlockSpec((None, CONV_W, LRU_BLOCK), lambda *g: (l, 0, n_of(*g))),
        pl.BlockSpec((None, 1, LRU_BLOCK), lambda *g: (l, 0, n_of(*g))),
        pl.BlockSpec((None, None, LRU_BLOCK, LRU_BLOCK), lambda *g: (l, n_of(*g), 0, 0)),
        pl.BlockSpec((None, 1, LRU_BLOCK), lambda *g: (l, 0, n_of(*g))),
        pl.BlockSpec((None, None, LRU_BLOCK, LRU_BLOCK), lambda *g: (l, n_of(*g), 0, 0)),
        pl.BlockSpec((None, 1, LRU_BLOCK), lambda *g: (l, 0, n_of(*g))),
        pl.BlockSpec((None, 1, LRU_BLOCK), lambda *g: (l, 0, n_of(*g))),
    ]


def _lru_sample_kernel(x_ref, g_ref, sc_ref, h0_ref, wc_ref, bc_ref, wa_ref, ba_ref, wx_ref, bx_ref, lam_ref,
                       u_ref, h_ref, cs_ref, xc_scr, a_scr, b_scr, y_scr, *, batch, t_new):
    n_state = CONV_W - 1

    def ext_row(b, i):
        if i < n_state:
            return sc_ref[b, i:i + 1, :]
        r = b * t_new + i - n_state
        return x_ref[r:r + 1, :]

    for b in range(batch):
        for t in range(t_new):
            acc = bc_ref[...] + ext_row(b, t) * wc_ref[0:1, :]
            for j in range(1, CONV_W):
                acc = acc + ext_row(b, t + j) * wc_ref[j:j + 1, :]
            xc_scr[b * t_new + t:b * t_new + t + 1, :] = acc
    a, bb = _lru_gates(xc_scr[...], wa_ref, ba_ref, wx_ref, bx_ref, lam_ref)
    a_scr[...] = a
    b_scr[...] = bb
    for b in range(batch):
        h = h0_ref[b:b + 1, :]
        for t in range(t_new):
            r = b * t_new + t
            h = a_scr[r:r + 1, :] * h + b_scr[r:r + 1, :]
            y_scr[r:r + 1, :] = h
        h_ref[b:b + 1, :] = h
        for i in range(n_state):
            cs_ref[b, i:i + 1, :] = ext_row(b, t_new + i)
    for b in range(batch):
        g = g_ref[b * t_new:(b + 1) * t_new, :]
        u_ref[b] = y_scr[b * t_new:(b + 1) * t_new, :] * _silu(g)


def _lru_sample(a6, state_conv, state_h, lru_w, l, batch, t_new):
    blk = (None, None, batch * t_new, LRU_BLOCK)
    n_of = lambda n: n
    rows = batch * t_new
    return pl.pallas_call(
        functools.partial(_lru_sample_kernel, batch=batch, t_new=t_new),
        grid=(N_LRU_BLOCKS,),
        in_specs=[
            pl.BlockSpec(blk, lambda n: (4, n, 0, 0)),
            pl.BlockSpec(blk, lambda n: (5, n, 0, 0)),
            pl.BlockSpec((None, batch, CONV_W - 1, LRU_BLOCK), lambda n: (l, 0, 0, n)),
            pl.BlockSpec((None, batch, LRU_BLOCK), lambda n: (l, 0, n)),
        ] + _lru_weight_specs(l, n_of),
        out_specs=[
            pl.BlockSpec((batch, t_new, LRU_BLOCK), lambda n: (0, 0, n)),
            pl.BlockSpec((batch, LRU_BLOCK), lambda n: (0, n)),
            pl.BlockSpec((batch, CONV_W - 1, LRU_BLOCK), lambda n: (0, 0, n)),
        ],
        out_shape=[
            jax.ShapeDtypeStruct((batch, t_new, D_LRU), F32),
            jax.ShapeDtypeStruct((batch, D_LRU), F32),
            jax.ShapeDtypeStruct((batch, CONV_W - 1, D_LRU), F32),
        ],
        scratch_shapes=[pltpu.VMEM((rows, LRU_BLOCK), F32)] * 4,
        compiler_params=_params(("arbitrary",), 32),
        name="lru_sample",
    )(a6, a6, state_conv, state_h, *lru_w)


def _outproj_kernel(ua_ref, ul_ref, w_ref, x_ref, gate_ref, gp_ref, *rest, n_tiles, has_next):
    if has_next:
        shift_ref, scale_ref, gn_ref, y_ref, h_ref, acc_a, acc_b = rest
    else:
        y_ref, acc_a, acc_b = rest
    i = pl.program_id(0)
    tm = x_ref.shape[0]
    chunk = min(tm, EPILOGUE_CHUNK)
    per_row = gate_ref.shape[0] == tm

    def matmul(acc):
        acc[...] = (jnp.dot(ua_ref[...].astype(BF16), w_ref[0:D_ATT, :], preferred_element_type=F32)
                    + jnp.dot(ul_ref[...].astype(BF16), w_ref[D_ATT:D_MODEL, :], preferred_element_type=F32))

    def epilogue(acc):
        for c in range(tm // chunk):
            rows = slice(c * chunk, (c + 1) * chunk)
            mrows = rows if per_row else slice(None)
            y = acc[rows, :]
            y = y * lax.rsqrt(jnp.mean(y * y, axis=-1, keepdims=True) + EPS)
            out = x_ref[rows, :] + gate_ref[mrows, :] * (y * gp_ref[...])
            y_ref[rows, :] = out
            if has_next:
                hn = out * lax.rsqrt(jnp.mean(out * out, axis=-1, keepdims=True) + EPS)
                hn = hn * gn_ref[...]
                h_ref[rows, :] = (hn * (1.0 + scale_ref[mrows, :]) + shift_ref[mrows, :]).astype(h_ref.dtype)

    accs = (acc_a, acc_b)

    @pl.when(i == 0)
    def _():
        matmul(accs[0])

    for parity in range(2):
        @pl.when((i > 0) & (i < n_tiles) & (i % 2 == parity))
        def _(parity=parity):
            epilogue(accs[1 - parity])
            matmul(accs[parity])

    @pl.when(i == n_tiles)
    def _():
        epilogue(accs[(n_tiles - 1) % 2])


def _outproj(u_att, u_lru, w_out, x2, gate, mod_idx, g_post, l, nxt, *, tm, rows_per_mod, vmem_mib):
    m_rows = x2.shape[0]
    n_tiles = m_rows // tm
    mm_tile = lambda i: jnp.minimum(i, n_tiles - 1)
    ep_tile = lambda i: jnp.maximum(i - 1, 0)
    row_spec = pl.BlockSpec((tm, D_MODEL), lambda i: (ep_tile(i), 0))
    in_specs = [
        pl.BlockSpec((tm, D_ATT), lambda i: (mm_tile(i), 0)),
        pl.BlockSpec((tm, D_LRU), lambda i: (mm_tile(i), 0)),
        pl.BlockSpec((D_MODEL, D_MODEL), lambda i: (0, 0), pipeline_mode=pl.Buffered(1)),
        row_spec,
        _mod_specs(mod_idx, 2, tm, rows_per_mod, ep_tile),
        pl.BlockSpec((None, 1, D_MODEL), lambda i: (l, 0, 0)),
    ]
    args = [u_att, u_lru, w_out, x2, gate, g_post]
    out_specs = [row_spec]
    out_shape = [jax.ShapeDtypeStruct((m_rows, D_MODEL), F32)]
    if nxt is not None:
        shift_n, scale_n, idx_n, g_pre, l_next = nxt
        in_specs += [
            _mod_specs(idx_n, 0, tm, rows_per_mod, ep_tile),
            _mod_specs(idx_n, 1, tm, rows_per_mod, ep_tile),
            pl.BlockSpec((None, 1, D_MODEL), lambda i: (l_next, 0, 0)),
        ]
        args += [shift_n, scale_n, g_pre]
        out_specs.append(row_spec)
        out_shape.append(jax.ShapeDtypeStruct((m_rows, D_MODEL), BF16))
    res = pl.pallas_call(
        functools.partial(_outproj_kernel, n_tiles=n_tiles, has_next=nxt is not None),
        grid=(n_tiles + 1,),
        in_specs=in_specs,
        out_specs=out_specs,
        out_shape=out_shape,
        scratch_shapes=[pltpu.VMEM((tm, D_MODEL), F32)] * 2,
        compiler_params=_params(("arbitrary",), vmem_mib),
        name="outproj",
    )(*args)
    return res if nxt is not None else (res[0], None)


def kernel(x_prompt, x_sample, cache_k, cache_v, state_h, state_conv, c_prompt, c_sample, rel_table, w_ada, b_ada,
           g_pre, w_in, w_conv, b_conv, w_a, b_a, w_x, b_x, lam, w_out, g_post):
    depth = w_in.shape[0]
    bp, seq, _ = x_prompt.shape
    bs, t_new, _ = x_sample.shape
    c_len = cache_k.shape[2]
    assert bp + bs <= MOD_ROWS and t_new <= 8
    assert seq % (L_BAND * DILATED_GROUPS[-1][1]) == 0 and c_len >= WINDOW_MAX

    c_all = jnp.concatenate([c_prompt, c_sample, jnp.zeros((MOD_ROWS - bp - bs, D_MODEL), F32)], axis=0)
    mod = _ada(c_all, w_ada, b_ada)
    mod_tab = mod.reshape(depth * MOD_ROWS * 3, 1, D_MODEL)
    pbias, sbias = _bias_tables(rel_table, c_len, t_new)

    w_in_l = w_in[0].astype(BF16)
    g_pre3 = g_pre.reshape(depth, 1, D_MODEL)
    g_post3 = g_post.reshape(depth, 1, D_MODEL)

    xp = x_prompt.reshape(bp * seq, D_MODEL)
    xs = x_sample.reshape(bs * t_new, D_MODEL)
    rows_s = bs * t_new
    lru_w = (w_conv, b_conv.reshape(depth, 1, D_LRU), w_a, b_a.reshape(depth, 1, D_LRU),
             w_x, b_x.reshape(depth, 1, D_LRU), lam.reshape(depth, 1, D_LRU))
    kv_p = kv_s = None
    hp_l, cp_l, hs_l, cs_l = [], [], [], []
    tm_norm, tm_in, tm_out = PROMPT_ROW_TILES
    p_idx = lambda l: (lambda b, which: (l * MOD_ROWS + b) * 3 + which)
    mod_s = [jnp.repeat(mod[l, bp:bp + bs], t_new, axis=0) for l in range(depth)]
    shift_s = [m[:, :D_MODEL] for m in mod_s]
    scale_s = [m[:, D_MODEL:2 * D_MODEL] for m in mod_s]
    gate_s = [m[:, 2 * D_MODEL:] for m in mod_s]
    hp = _prenorm(xp, mod_tab, mod_tab, p_idx(0), g_pre3, 0, tm=tm_norm, rows_per_mod=seq // tm_norm)
    hs = _prenorm(xs, shift_s[0], scale_s[0], None, g_pre3, 0, tm=rows_s, rows_per_mod=1)
    for l in range(depth):
        more = l + 1 < depth
        a6, kp, vp = _inproj(hp, w_in_l, l, depth, kv_p, tm=tm_in, vmem_mib=BIG_CALL_VMEM_MIB)
        kv_p = (kp, vp)
        a6 = a6.reshape(N_SEG, N_HEADS, bp, seq, HEAD_DIM)
        u_att, u_lru, h_last, conv, w_out_l, w_in_next = _mixers_prompt(a6, pbias, lru_w, w_out, l, bp, seq,
                                                                        w_in if more else None)
        nxt = (mod_tab, mod_tab, p_idx(l + 1), g_pre3, l + 1) if more else None
        xp, hp = _outproj(u_att.reshape(bp * seq, D_ATT), u_lru.reshape(bp * seq, D_LRU), w_out_l, xp, mod_tab,
                          p_idx(l), g_post3, l, nxt, tm=tm_out, rows_per_mod=seq // tm_out,
                          vmem_mib=BIG_CALL_VMEM_MIB)
        hp_l.append(h_last.reshape(bp, D_LRU))
        cp_l.append(conv)
        a6, ks, vs = _inproj(hs, w_in_l, l, depth, kv_s, tm=rows_s, vmem_mib=SMALL_CALL_VMEM_MIB)
        kv_s = (ks, vs)
        u_att = _attn_sample(a6, cache_k, cache_v, sbias, l, bs, t_new, c_len)
        u_lru, h_last, conv = _lru_sample(a6, state_conv, state_h, lru_w, l, bs, t_new)
        nxt = (shift_s[l + 1], scale_s[l + 1], None, g_pre3, l + 1) if more else None
        xs, hs = _outproj(u_att.reshape(rows_s, D_ATT), u_lru.reshape(rows_s, D_LRU), w_out_l, xs, gate_s[l], None,
                          g_post3, l, nxt, tm=rows_s, rows_per_mod=1, vmem_mib=SMALL_CALL_VMEM_MIB)
        hs_l.append(h_last)
        cs_l.append(conv)
        w_in_l = w_in_next

    kp, vp = kv_p
    ks, vs = kv_s
    return (xp.reshape(bp, seq, D_MODEL), xs.reshape(bs, t_new, D_MODEL),
            kp.reshape(depth, bp, seq, N_HEADS, HEAD_DIM), vp.reshape(depth, bp, seq, N_HEADS, HEAD_DIM),
            jnp.stack(hp_l), jnp.stack(cp_l),
            ks.reshape(depth, bs, t_new, N_HEADS, HEAD_DIM), vs.reshape(depth, bs, t_new, N_HEADS, HEAD_DIM),
            jnp.stack(hs_l), jnp.stack(cs_l))
```
